```python
import math
import jax
import jax.numpy as jnp
from jax import lax
import numpy as np

D_MODEL = 1024
BATCH = 8
SEQ = 4096
DEPTH = 2

FOX_HEADS = 4
FOX_HEAD_DIM = D_MODEL // 16
RET_HEADS = 4
RET_HEAD_DIM = D_MODEL // 16
GDN_HEADS = 4
GDN_HEAD_DIM = D_MODEL // 8
FOX_WIDTH = FOX_HEADS * FOX_HEAD_DIM
RET_WIDTH = RET_HEADS * RET_HEAD_DIM
GDN_WIDTH = GDN_HEADS * GDN_HEAD_DIM
D_MIX = FOX_WIDTH + RET_WIDTH + GDN_WIDTH
Q_BLOCK = 128
RET_CHUNK = 128
GDN_CHUNK = 64
CONV_WIDTH = 4
RET_ANGLE_BASE = 10000.0
N_GROUPS = 4
EXPERTS_PER_GROUP = 8
TOP_K_EXPERTS = 2
EXPERT_FF = D_MODEL // 4
NORM_EPS = 1e-6
IN_SPLITS = (FOX_WIDTH, FOX_WIDTH, FOX_WIDTH, FOX_HEADS,
             RET_WIDTH, RET_WIDTH, RET_WIDTH, RET_WIDTH,
             GDN_WIDTH, GDN_WIDTH, GDN_WIDTH, GDN_WIDTH, GDN_HEADS, GDN_HEADS)
D_IN = sum(IN_SPLITS)

kernel_name = "hybrid_fox_retnet_gdn_hmoe"

F32 = jnp.float32


def rms_norm(x, w):
    xf = x.astype(F32)
    y = xf * lax.rsqrt(jnp.mean(xf * xf, axis=-1, keepdims=True) + NORM_EPS)
    return (y * w.astype(F32)).astype(x.dtype)


def head_rms(t):
    return t * lax.rsqrt(jnp.mean(t * t, axis=-1, keepdims=True) + NORM_EPS)


def l2_normalize(t):
    return t * lax.rsqrt(jnp.sum(t * t, axis=-1, keepdims=True) + NORM_EPS)


def split_columns(y):
    cuts = np.cumsum(IN_SPLITS)[:-1].tolist()
    return jnp.split(y, cuts, axis=-1)


def to_heads(t, n_heads):
    b, s, _ = t.shape
    return t.reshape(b, s, n_heads, -1).transpose(0, 2, 1, 3)


def merge_heads(t):
    b, h, s, d = t.shape
    return t.transpose(0, 2, 1, 3).reshape(b, s, h * d)


def rotary(t, pos):
    d = t.shape[-1]
    inv = 1.0 / (RET_ANGLE_BASE ** jnp.linspace(0.0, 1.0, d // 2, dtype=F32))
    ang = pos[:, None] * inv[None, :]
    cos, sin = jnp.cos(ang), jnp.sin(ang)
    t1, t2 = t[..., : d // 2], t[..., d // 2:]
    return jnp.concatenate([t1 * cos - t2 * sin, t1 * sin + t2 * cos], axis=-1)


def fox_attention(q, k, v, log_f):
    b, h, s, d = q.shape
    scale = d ** -0.5
    c = jnp.cumsum(log_f, axis=-1)
    nb = s // Q_BLOCK
    qb = q.reshape(b, h, nb, Q_BLOCK, d).transpose(2, 0, 1, 3, 4)
    cb = c.reshape(b, h, nb, Q_BLOCK).transpose(2, 0, 1, 3)
    kpos = jnp.arange(s)

    def one_block(args):
        q_blk, c_blk, i = args
        logits = jnp.einsum("bhqd,bhkd->bhqk", q_blk, k) * scale + c_blk[..., None] - c[..., None, :]
        qpos = i * Q_BLOCK + jnp.arange(Q_BLOCK)
        logits = jnp.where(kpos[None, :] <= qpos[:, None], logits, -jnp.inf)
        p = jax.nn.softmax(logits, axis=-1)
        return jnp.einsum("bhqk,bhkd->bhqd", p, v)

    o = lax.map(one_block, (qb, cb, jnp.arange(nb)))
    return o.transpose(1, 2, 0, 3, 4).reshape(b, h, s, d)


def retention(q, k, v):
    b, h, s, dk = q.shape
    dv = v.shape[-1]
    C = RET_CHUNK
    n = s // C
    log_g = jnp.log1p(-jnp.exp2(-5.0 - jnp.arange(h, dtype=F32)))
    idx = jnp.arange(C, dtype=F32)
    rel = idx[:, None] - idx[None, :]
    dmat = jnp.where(rel[None] >= 0, jnp.exp(jnp.maximum(rel, 0.0)[None] * log_g[:, None, None]), 0.0)
    qc = q.reshape(b, h, n, C, dk)
    kc = k.reshape(b, h, n, C, dk)
    vc = v.reshape(b, h, n, C, dv)
    scores = jnp.einsum("bhnid,bhnjd->bhnij", qc, kc) * dmat[None, :, None]
    o_intra = jnp.einsum("bhnij,bhnjv->bhniv", scores, vc)
    k_w = kc * jnp.exp((C - 1 - idx)[None, :] * log_g[:, None])[None, :, None, :, None]
    chunk_kv = jnp.einsum("bhncd,bhncv->nbhdv", k_w, vc)
    chunk_decay = jnp.exp(C * log_g)[None, :, None, None]

    def step(state, kv):
        return state * chunk_decay + kv, state

    _, s_prev = lax.scan(step, jnp.zeros((b, h, dk, dv), F32), chunk_kv)
    q_w = qc * jnp.exp((idx + 1.0)[None, :] * log_g[:, None])[None, :, None, :, None]
    o_inter = jnp.einsum("bhncd,nbhdv->bhncv", q_w, s_prev)
    return (o_intra + o_inter).reshape(b, h, s, dv)


def causal_depthwise_conv(x, w):
    kw, ch = w.shape
    return lax.conv_general_dilated(
        x, w[:, None, :].astype(x.dtype), window_strides=(1,), padding=[(kw - 1, 0)],
        dimension_numbers=("NWC", "WIO", "NWC"), feature_group_count=ch)


def gated_delta_rule(q, k, v, g, beta):
    b, h, s, dk = q.shape
    dv = v.shape[-1]
    C = GDN_CHUNK
    n = s // C
    qc = (q * dk ** -0.5).reshape(b, h, n, C, dk)
    kc = k.reshape(b, h, n, C, dk)
    vc = v.reshape(b, h, n, C, dv)
    gc = jnp.cumsum(g.reshape(b, h, n, C), axis=-1)
    bc = beta.reshape(b, h, n, C, 1)
    idx = jnp.arange(C)
    incl = idx[:, None] >= idx[None, :]
    strict = idx[:, None] > idx[None, :]
    diff = gc[..., :, None] - gc[..., None, :]
    decay = jnp.where(incl, jnp.exp(jnp.where(incl, diff, 0.0)), 0.0)
    kb = kc * bc
    lower = jnp.where(strict, jnp.einsum("bhnid,bhnjd->bhnij", kb, kc) * decay, 0.0)
    a_mat = lower + jnp.eye(C, dtype=F32)
    rhs = jnp.concatenate([vc * bc, kb * jnp.exp(gc)[..., None]], axis=-1)
    sol = lax.linalg.triangular_solve(a_mat, rhs, left_side=True, lower=True, unit_diagonal=True)
    u0, w = sol[..., :dv], sol[..., dv:]
    attn = jnp.where(incl, jnp.einsum("bhnid,bhnjd->bhnij", qc, kc) * decay, 0.0)
    g_last = gc[..., -1]
    q_g = qc * jnp.exp(gc)[..., None]
    k_tail = kc * jnp.exp(g_last[..., None] - gc)[..., None]
    xs = (jnp.moveaxis(u0, 2, 0), jnp.moveaxis(w, 2, 0), jnp.moveaxis(attn, 2, 0),
          jnp.moveaxis(q_g, 2, 0), jnp.moveaxis(k_tail, 2, 0), jnp.moveaxis(g_last, 2, 0))

    def step(state, inp):
        u0_i, w_i, attn_i, q_i, k_i, gl_i = inp
        u = u0_i - jnp.einsum("bhcd,bhdv->bhcv", w_i, state)
        o = jnp.einsum("bhcd,bhdv->bhcv", q_i, state) + jnp.einsum("bhij,bhjv->bhiv", attn_i, u)
        state = state * jnp.exp(gl_i)[..., None, None] + jnp.einsum("bhcd,bhcv->bhdv", k_i, u)
        return state, o

    _, o = lax.scan(step, jnp.zeros((b, h, dk, dv), F32), xs)
    return jnp.moveaxis(o, 0, 2).reshape(b, h, s, dv)


def hybrid_mixer(h, w_in, fox_forget_bias, gdn_conv_w, gdn_a_log, gdn_dt_bias, gdn_norm_w, w_out):
    b, s, _ = h.shape
    y = jnp.einsum("bsd,dn->bsn", h, w_in)
    fq, fk, fv, ff, rq, rk, rv, rg, gq, gk, gv, gz, ga, gb = split_columns(y)

    log_f = jax.nn.log_sigmoid(ff.astype(F32) + fox_forget_bias.astype(F32)).transpose(0, 2, 1)
    o_fox = fox_attention(to_heads(fq, FOX_HEADS).astype(F32), to_heads(fk, FOX_HEADS).astype(F32),
                          to_heads(fv, FOX_HEADS).astype(F32), log_f)
    o_fox = merge_heads(o_fox)

    pos = jnp.arange(s, dtype=F32)
    rq_h = rotary(to_heads(rq, RET_HEADS).astype(F32), pos)
    rk_h = rotary(to_heads(rk, RET_HEADS).astype(F32), pos) * RET_HEAD_DIM ** -0.5
    o_ret = retention(rq_h, rk_h, to_heads(rv, RET_HEADS).astype(F32))
    o_ret = merge_heads(head_rms(o_ret)) * jax.nn.silu(rg.astype(F32))

    qkv = jnp.concatenate([gq, gk, gv], axis=-1).astype(F32)
    qkv = jax.nn.silu(causal_depthwise_conv(qkv, gdn_conv_w.astype(F32)))
    cq, ck, cv = jnp.split(qkv, [GDN_WIDTH, 2 * GDN_WIDTH], axis=-1)
    log_a = -jnp.exp(gdn_a_log.astype(F32)) * jax.nn.softplus(ga.astype(F32) + gdn_dt_bias.astype(F32))
    beta = jax.nn.sigmoid(gb.astype(F32))
    o_gdn = gated_delta_rule(l2_normalize(to_heads(cq, GDN_HEADS)), l2_normalize(to_heads(ck, GDN_HEADS)),
                             to_heads(cv, GDN_HEADS), log_a.transpose(0, 2, 1), beta.transpose(0, 2, 1))
    o_gdn = merge_heads(head_rms(o_gdn) * gdn_norm_w.astype(F32)) * jax.nn.silu(gz.astype(F32))

    mix = jnp.concatenate([o_fox, o_ret, o_gdn], axis=-1).astype(h.dtype)
    return jnp.einsum("bsm,md->bsd", mix, w_out).astype(h.dtype)


def hier_moe(h, rg_w, rg_b, re_w, re_b, w1, w3, w2):
    b, s, d = h.shape
    t = h.reshape(b * s, d)
    n_exp = N_GROUPS * EXPERTS_PER_GROUP
    grp_prob = jax.nn.softmax(jnp.einsum("td,dg->tg", t, rg_w).astype(F32) + rg_b.astype(F32), axis=-1)
    grp_p, grp_idx = lax.top_k(grp_prob, 1)
    exp_logits = jnp.einsum("td,dn->tn", t, re_w).astype(F32).reshape(-1, N_GROUPS, EXPERTS_PER_GROUP)
    exp_logits = exp_logits + re_b.astype(F32)
    sel = jnp.take_along_axis(exp_logits, grp_idx[:, :, None], axis=1)[:, 0]
    exp_p, exp_idx = lax.top_k(jax.nn.softmax(sel, axis=-1), TOP_K_EXPERTS)
    gate_w = grp_p * exp_p / jnp.sum(exp_p, axis=-1, keepdims=True)
    flat = grp_idx * EXPERTS_PER_GROUP + exp_idx
    gates = jnp.einsum("tk,tkn->tn", gate_w, jax.nn.one_hot(flat, n_exp, dtype=F32))
    gates = gates.reshape(-1, N_GROUPS, EXPERTS_PER_GROUP)
    out = jnp.zeros((t.shape[0], d), F32)
    for gi in range(N_GROUPS):
        a = jnp.einsum("td,edf->tef", t, w1[gi]).astype(F32)
        u = jnp.einsum("td,edf->tef", t, w3[gi]).astype(F32)
        hid = jax.nn.silu(a) * u * gates[:, gi, :, None]
        out = out + jnp.einsum("tef,efd->td", hid, w2[gi].astype(F32))
    return out.reshape(b, s, d).astype(h.dtype)


def setup_inputs(seed: int = 0) -> dict:
    key = jax.random.key(seed)
    ks = jax.random.split(key, 20)

    def nrm(k, shape, scale):
        return scale * jax.random.normal(k, shape, F32)

    x = jax.random.normal(ks[0], (BATCH, SEQ, D_MODEL), F32)
    norm1_w = 1.0 + nrm(ks[1], (DEPTH, D_MODEL), 0.02)
    w_in = nrm(ks[2], (DEPTH, D_MODEL, D_IN), D_MODEL ** -0.5)
    fox_forget_bias = jnp.linspace(1.0, 5.0, FOX_HEADS, dtype=F32)[None, :] + nrm(ks[3], (DEPTH, FOX_HEADS), 0.1)
    gdn_conv_w = nrm(ks[4], (DEPTH, CONV_WIDTH, 3 * GDN_WIDTH), CONV_WIDTH ** -0.5)
    gdn_a_log = jnp.log(jax.random.uniform(ks[5], (DEPTH, GDN_HEADS), F32, 1.0, 16.0))
    dt = jnp.exp(jax.random.uniform(ks[6], (DEPTH, GDN_HEADS), F32, math.log(1e-3), math.log(0.1)))
    gdn_dt_bias = dt + jnp.log(-jnp.expm1(-dt))
    gdn_norm_w = 1.0 + nrm(ks[7], (DEPTH, GDN_HEAD_DIM), 0.02)
    w_out = nrm(ks[8], (DEPTH, D_MIX, D_MODEL), D_MIX ** -0.5)
    norm2_w = 1.0 + nrm(ks[9], (DEPTH, D_MODEL), 0.02)
    router_group_w = nrm(ks[10], (DEPTH, D_MODEL, N_GROUPS), D_MODEL ** -0.5)
    router_group_b = nrm(ks[11], (DEPTH, N_GROUPS), 0.01)
    router_expert_w = nrm(ks[12], (DEPTH, D_MODEL, N_GROUPS * EXPERTS_PER_GROUP), D_MODEL ** -0.5)
    router_expert_b = nrm(ks[13], (DEPTH, N_GROUPS, EXPERTS_PER_GROUP), 0.01)
    expert_w1 = nrm(ks[14], (DEPTH, N_GROUPS, EXPERTS_PER_GROUP, D_MODEL, EXPERT_FF), D_MODEL ** -0.5)
    expert_w3 = nrm(ks[15], (DEPTH, N_GROUPS, EXPERTS_PER_GROUP, D_MODEL, EXPERT_FF), D_MODEL ** -0.5)
    expert_w2 = nrm(ks[16], (DEPTH, N_GROUPS, EXPERTS_PER_GROUP, EXPERT_FF, D_MODEL), EXPERT_FF ** -0.5)
    final_norm_w = 1.0 + nrm(ks[17], (D_MODEL,), 0.02)
    return {"x": x, "norm1_w": norm1_w, "w_in": w_in, "fox_forget_bias": fox_forget_bias,
            "gdn_conv_w": gdn_conv_w, "gdn_a_log": gdn_a_log, "gdn_dt_bias": gdn_dt_bias,
            "gdn_norm_w": gdn_norm_w, "w_out": w_out, "norm2_w": norm2_w,
            "router_group_w": router_group_w, "router_group_b": router_group_b,
            "router_expert_w": router_expert_w, "router_expert_b": router_expert_b,
            "expert_w1": expert_w1, "expert_w3": expert_w3, "expert_w2": expert_w2,
            "final_norm_w": final_norm_w}


def reference(x, norm1_w, w_in, fox_forget_bias, gdn_conv_w, gdn_a_log, gdn_dt_bias, gdn_norm_w,
              w_out, norm2_w, router_group_w, router_group_b, router_expert_w, router_expert_b,
              expert_w1, expert_w3, expert_w2, final_norm_w):
    for l in range(DEPTH):
        h = rms_norm(x, norm1_w[l])
        x = x + hybrid_mixer(h, w_in[l], fox_forget_bias[l], gdn_conv_w[l], gdn_a_log[l],
                             gdn_dt_bias[l], gdn_norm_w[l], w_out[l])
        h = rms_norm(x, norm2_w[l])
        x = x + hier_moe(h, router_group_w[l], router_group_b[l], router_expert_w[l], router_expert_b[l],
                         expert_w1[l], expert_w3[l], expert_w2[l])
    return rms_norm(x, final_norm_w)
```

```python
import functools
import math

import jax
import jax.numpy as jnp
import numpy as np
from jax import lax
from jax.experimental import pallas as pl
from jax.experimental.pallas import tpu as pltpu

F32 = jnp.float32
BF16 = jnp.bfloat16
HIGHEST = lax.Precision.HIGHEST

D_MODEL = 1024
FOX_HEADS = 4
RET_HEADS = 4
GDN_HEADS = 4
HEAD64 = 64
GDN_DIM = 128
FOX_WIDTH = FOX_HEADS * HEAD64
RET_WIDTH = RET_HEADS * HEAD64
GDN_WIDTH = GDN_HEADS * GDN_DIM
RET_CHUNK = 128
GDN_CHUNK = 64
CONV_WIDTH = 4
RET_ANGLE_BASE = 10000.0
N_GROUPS = 4
EXPERTS_PER_GROUP = 8
N_EXPERTS = N_GROUPS * EXPERTS_PER_GROUP
EXPERT_FF = 256
NORM_EPS = 1e-6
LANES = 128
IN_SPLITS = (FOX_WIDTH, FOX_WIDTH, FOX_WIDTH, FOX_HEADS,
             RET_WIDTH, RET_WIDTH, RET_WIDTH, RET_WIDTH,
             GDN_WIDTH, GDN_WIDTH, GDN_WIDTH, GDN_WIDTH, GDN_HEADS, GDN_HEADS)

REST_RQ, REST_RK, REST_RV, REST_RG = 0, 256, 512, 768
REST_GQ, REST_GK, REST_GV, REST_GZ = 1024, 1536, 2048, 2560
REST_SMALL = 3072
REST_WIDTH = 3200
SMALL_FF, SMALL_GA, SMALL_GB = 0, 4, 8
ROUTER_GRP, ROUTER_EXP = 0, 4

VMEM_LIMIT = 56 * 1024 * 1024


def _params(*sem):
    return pltpu.CompilerParams(dimension_semantics=sem, vmem_limit_bytes=VMEM_LIMIT)


def _mm(a, b):
    return jnp.dot(a.astype(BF16), b.astype(BF16), preferred_element_type=F32)


def _mm_nt(a, b):
    return lax.dot_general(a.astype(BF16), b.astype(BF16), (((1,), (1,)), ((), ())),
                           preferred_element_type=F32)


def _mm_tn(a, b):
    return lax.dot_general(a.astype(BF16), b.astype(BF16), (((0,), (0,)), ((), ())),
                           preferred_element_type=F32)


def _silu(x):
    return x * (1.0 / (1.0 + jnp.exp(-x)))


def _rms(x, w):
    return x * lax.rsqrt(jnp.mean(x * x, axis=-1, keepdims=True) + NORM_EPS) * w


def _inproj_kernel(x_ref, nw_ref, wf_ref, wr_ref, of_ref, or_ref):
    hn = _rms(x_ref[...], nw_ref[...]).astype(BF16)
    of_ref[...] = jnp.dot(hn, wf_ref[...], preferred_element_type=F32).astype(BF16)
    step = 640
    for c in range(0, REST_WIDTH, step):
        or_ref[:, c:c + step] = jnp.dot(hn, wr_ref[:, c:c + step], preferred_element_type=F32)


def _inproj(x, nw, wf, wr, tm=512):
    t = x.shape[0]
    return pl.pallas_call(
        _inproj_kernel,
        grid=(t // tm,),
        in_specs=[pl.BlockSpec((tm, D_MODEL), lambda i: (i, 0)),
                  pl.BlockSpec((1, D_MODEL), lambda i: (0, 0)),
                  pl.BlockSpec((D_MODEL, 3 * FOX_WIDTH), lambda i: (0, 0)),
                  pl.BlockSpec((D_MODEL, REST_WIDTH), lambda i: (0, 0))],
        out_specs=[pl.BlockSpec((tm, 3 * FOX_WIDTH), lambda i: (i, 0)),
                   pl.BlockSpec((tm, REST_WIDTH), lambda i: (i, 0))],
        out_shape=[jax.ShapeDtypeStruct((t, 3 * FOX_WIDTH), BF16),
                   jax.ShapeDtypeStruct((t, REST_WIDTH), F32)],
        compiler_params=_params("parallel"),
        name="inproj",
    )(x, nw, wf, wr)


def _fgate_kernel(ff_ref, bias_ref, tri_ref, blk_ref, c_ref):
    rows = ff_ref.shape[1] * ff_ref.shape[2]
    z = ff_ref[0].reshape(rows, LANES) + bias_ref[...]
    lf = jnp.minimum(z, 0.0) - jnp.log1p(jnp.exp(-jnp.abs(z)))
    within = jnp.dot(lf, tri_ref[...], precision=HIGHEST, preferred_element_type=F32)
    tot = jnp.broadcast_to(within[:, LANES - 1:LANES], (rows, LANES))
    before = jnp.dot(blk_ref[...], tot, precision=HIGHEST, preferred_element_type=F32)
    c_ref[0] = (within + before).reshape(c_ref.shape[1:])


def _fgate(ff4, bias_rows):
    b, h, r, _ = ff4.shape
    rows = h * r
    tri = (np.arange(LANES)[:, None] <= np.arange(LANES)[None, :]).astype(np.float32)
    i = np.arange(rows)
    blk = ((i[:, None] // r == i[None, :] // r) & (i[None, :] < i[:, None])).astype(np.float32)
    return pl.pallas_call(
        _fgate_kernel,
        grid=(b,),
        in_specs=[pl.BlockSpec((1, h, r, LANES), lambda i: (i, 0, 0, 0)),
                  pl.BlockSpec((rows, 1), lambda i: (0, 0)),
                  pl.BlockSpec((LANES, LANES), lambda i: (0, 0)),
                  pl.BlockSpec((rows, rows), lambda i: (0, 0))],
        out_specs=pl.BlockSpec((1, h, r, LANES), lambda i: (i, 0, 0, 0)),
        out_shape=jax.ShapeDtypeStruct((b, h, r, LANES), F32),
        compiler_params=_params("parallel"),
        name="fgate",
    )(ff4, bias_rows, jnp.asarray(tri), jnp.asarray(blk))


def _fox_kernel(q_ref, k_ref, v_ref, c_ref, o_ref, *, tq, tk):
    i = pl.program_id(2)
    lane = lax.broadcasted_iota(jnp.int32, (1, LANES), 1)
    first = lane < HEAD64
    q = q_ref[0] * jnp.asarray(HEAD64 ** -0.5, BF16)
    zero = jnp.zeros_like(q)
    qh = (jnp.where(first, q, zero), jnp.where(first, zero, q))
    nfull = (i * tq) // tk
    cbase = [c_ref[0, hh, nfull][:, 0:1] for hh in range(2)]
    qpos = i * tq + lax.broadcasted_iota(jnp.int32, (tq, 1), 0)

    def step(j, carry, masked):
        k0 = pl.multiple_of(j * tk, tk)
        k = k_ref[0, pl.ds(k0, tk), :]
        v = v_ref[0, pl.ds(k0, tk), :]
        out = []
        for hh in range(2):
            m, l, acc = carry[hh]
            s = lax.dot_general(qh[hh], k, (((1,), (1,)), ((), ())), preferred_element_type=F32)
            s = s + (cbase[hh] - c_ref[0, hh, j])
            if masked:
                kpos = j * tk + lax.broadcasted_iota(jnp.int32, (1, tk), 1)
                s = jnp.where(kpos <= qpos, s, -jnp.inf)
            m_new = jnp.maximum(m, jnp.max(s, axis=-1, keepdims=True))
            alpha = jnp.exp(m - m_new)
            p = jnp.exp(s - m_new)
            l = alpha * l + jnp.sum(p, axis=-1, keepdims=True)
            acc = alpha * acc + jnp.dot(p.astype(BF16), v, preferred_element_type=F32)
            out.append((m_new, l, acc))
        return tuple(out)

    init = tuple((jnp.full((tq, 1), -jnp.inf, F32), jnp.zeros((tq, 1), F32), jnp.zeros((tq, LANES), F32))
                 for _ in range(2))
    carry = lax.fori_loop(0, nfull, functools.partial(step, masked=False), init)
    carry = step(nfull, carry, True)
    o0 = carry[0][2] / carry[0][1]
    o1 = carry[1][2] / carry[1][1]
    o_ref[0] = jnp.where(first, o0, o1).astype(BF16)


def _fox(qkv, c, tq=256, tk=512):
    b, s, _ = qkv.shape
    npair = FOX_HEADS // 2
    kern = functools.partial(_fox_kernel, tq=tq, tk=tk)
    return pl.pallas_call(
        kern,
        grid=(b, npair, s // tq),
        in_specs=[pl.BlockSpec((1, tq, LANES), lambda bi, p, i: (bi, i, p)),
                  pl.BlockSpec((1, s, LANES), lambda bi, p, i: (bi, 0, npair + p)),
                  pl.BlockSpec((1, s, LANES), lambda bi, p, i: (bi, 0, 2 * npair + p)),
                  pl.BlockSpec((1, 2, s // tk, 1, tk), lambda bi, p, i: (bi, p, 0, 0, 0))],
        out_specs=pl.BlockSpec((1, tq, LANES), lambda bi, p, i: (bi, i, p)),
        out_shape=jax.ShapeDtypeStruct((b, s, FOX_WIDTH), BF16),
        compiler_params=_params("parallel", "parallel", "arbitrary"),
        name="fox",
    )(qkv, qkv, qkv, c)


def _ret_kernel(q_ref, k_ref, v_ref, g_ref, cos_ref, sin_ref, dmat_ref, qdec_ref, kdec_ref, cd_ref, bm_ref,
                o_ref, state_ref, *, ts):
    @pl.when(pl.program_id(2) == 0)
    def _():
        state_ref[...] = jnp.zeros_like(state_ref)

    lane = lax.broadcasted_iota(jnp.int32, (1, LANES), 1)
    q_first = (lane % HEAD64) < (HEAD64 // 2)
    v_first = lane < HEAD64
    c_len = RET_CHUNK
    for c in range(ts // c_len):
        rows = slice(c * c_len, (c + 1) * c_len)
        cos = cos_ref[rows, :]
        sin = sin_ref[rows, :]
        q = q_ref[0, rows, :]
        k = k_ref[0, rows, :]
        v = v_ref[0, rows, :]
        qr = q * cos + pltpu.roll(q, LANES // 2, 1) * sin
        kr = k * cos + pltpu.roll(k, LANES // 2, 1) * sin
        q2 = jnp.concatenate([jnp.where(q_first, qr, 0.0), jnp.where(q_first, 0.0, qr)], axis=0)
        s = _mm_nt(q2, kr * (HEAD64 ** -0.5))
        s0 = s[:c_len] * dmat_ref[0]
        s1 = s[c_len:] * dmat_ref[1]
        o = _mm(s0, jnp.where(v_first, v, 0.0)) + _mm(s1, jnp.where(v_first, 0.0, v))
        state = state_ref[...]
        o = o + _mm(qr * qdec_ref[0], state)
        kv = _mm_tn(kr * kdec_ref[0], v)
        state_ref[...] = state * cd_ref[0] + kv * bm_ref[0]
        sq = o * o
        ms0 = jnp.sum(jnp.where(v_first, sq, 0.0), axis=-1, keepdims=True)
        ms1 = jnp.sum(jnp.where(v_first, 0.0, sq), axis=-1, keepdims=True)
        ms = jnp.where(v_first, ms0, ms1) * (1.0 / HEAD64)
        y = o * lax.rsqrt(ms + NORM_EPS) * _silu(g_ref[0, rows, :])
        o_ref[0, rows, :] = y.astype(BF16)


def _ret_tables(s):
    npair = RET_HEADS // 2
    half = HEAD64 // 2
    lane = np.arange(LANES)
    log_g = np.log1p(-np.exp2(-5.0 - np.arange(RET_HEADS, dtype=np.float32))).astype(np.float32)
    idx = np.arange(RET_CHUNK, dtype=np.float32)
    rel = idx[:, None] - idx[None, :]
    dmat = np.where(rel[None] >= 0, np.exp(np.maximum(rel, 0.0)[None] * log_g[:, None, None]), 0.0)
    qdec, kdec, cd, bm = [], [], [], []
    for p in range(npair):
        hq = 2 * p + ((lane % HEAD64) >= half)
        hv = 2 * p + (lane >= HEAD64)
        qdec.append(np.exp((idx[:, None] + 1.0) * log_g[hq][None, :]))
        kdec.append(np.exp((RET_CHUNK - 1 - idx)[:, None] * log_g[hq][None, :]) * HEAD64 ** -0.5)
        cd.append(np.broadcast_to(np.exp(RET_CHUNK * log_g[hq])[:, None], (LANES, LANES)))
        bm.append((hq[:, None] == hv[None, :]).astype(np.float32))
    tabs = [np.stack(a).astype(np.float32) for a in (qdec, kdec, cd, bm)]
    inv = 1.0 / (RET_ANGLE_BASE ** jnp.linspace(0.0, 1.0, half, dtype=F32))
    pos = jnp.arange(s, dtype=F32)
    ang = pos[:, None] * inv[None, :]
    cos = jnp.tile(jnp.cos(ang), (1, LANES // half))
    sin = jnp.tile(jnp.sin(ang), (1, LANES // half))
    sin = jnp.where(jnp.asarray(lane)[None, :] < LANES // 2, -sin, sin)
    return [jnp.asarray(dmat.astype(np.float32))] + [jnp.asarray(a) for a in tabs] + [cos, sin]


def _ret(rest3, tables, ts=1024):
    b, s, _ = rest3.shape
    dmat, qdec, kdec, cd, bm, cos, sin = tables
    npair = RET_HEADS // 2
    col = lambda off: (lambda bi, p, i: (bi, i, off // LANES + p))
    tab = lambda bi, p, i: (p, 0, 0)
    kern = functools.partial(_ret_kernel, ts=ts)
    return pl.pallas_call(
        kern,
        grid=(b, npair, s // ts),
        in_specs=[pl.BlockSpec((1, ts, LANES), col(REST_RQ)),
                  pl.BlockSpec((1, ts, LANES), col(REST_RK)),
                  pl.BlockSpec((1, ts, LANES), col(REST_RV)),
                  pl.BlockSpec((1, ts, LANES), col(REST_RG)),
                  pl.BlockSpec((ts, LANES), lambda bi, p, i: (i, 0)),
                  pl.BlockSpec((ts, LANES), lambda bi, p, i: (i, 0)),
                  pl.BlockSpec((2, RET_CHUNK, RET_CHUNK), tab),
                  pl.BlockSpec((1, RET_CHUNK, LANES), tab),
                  pl.BlockSpec((1, RET_CHUNK, LANES), tab),
                  pl.BlockSpec((1, LANES, LANES), tab),
                  pl.BlockSpec((1, LANES, LANES), tab)],
        out_specs=pl.BlockSpec((1, ts, LANES), lambda bi, p, i: (bi, i, p)),
        out_shape=jax.ShapeDtypeStruct((b, s, RET_WIDTH), BF16),
        scratch_shapes=[pltpu.VMEM((LANES, LANES), F32)],
        compiler_params=_params("parallel", "parallel", "arbitrary"),
        name="retention",
    )(rest3, rest3, rest3, rest3, cos, sin, dmat, qdec, kdec, cd, bm)


def _gdn_prep_kernel(q_ref, k_ref, v_ref, sm_ref, wq_ref, wk_ref, wv_ref, alog_ref, dtb_ref, tri_ref,
                     u0_ref, w_ref, qg_ref, kt_ref, at_ref, eg_ref):
    h = pl.program_id(1)
    c_len = GDN_CHUNK
    n_chunks = q_ref.shape[1] // c_len
    lane = lax.broadcasted_iota(jnp.int32, (1, LANES), 1)
    ri = lax.broadcasted_iota(jnp.int32, (c_len, c_len), 0)
    ci = lax.broadcasted_iota(jnp.int32, (c_len, c_len), 1)
    incl = ri >= ci
    strict = ri > ci
    eye = (ri == ci).astype(F32)
    neg_a = -jnp.exp(alog_ref[...])
    dtb = dtb_ref[...]
    tri = tri_ref[...]

    def conv_silu(ref, w_ref_, n, r0):
        cur = ref[0, pl.ds(r0, c_len), :]
        p0 = pl.multiple_of(jnp.maximum(r0 - 8, 0), 8)
        prev = ref[0, pl.ds(p0, 8), :]
        prev = jnp.where(jnp.broadcast_to(n > 0, prev.shape), prev, 0.0)
        xc = jnp.concatenate([prev, cur], axis=0)
        w = w_ref_[...]
        y = cur * w[CONV_WIDTH - 1:CONV_WIDTH, :]
        for j in range(CONV_WIDTH - 1):
            shifted = pltpu.roll(xc, CONV_WIDTH - 1 - j, 0)[8:, :]
            y = y + shifted * w[j:j + 1, :]
        return _silu(y)

    def chunk(n, carry):
        r0 = pl.multiple_of(n * c_len, c_len)
        cq = conv_silu(q_ref, wq_ref, n, r0)
        ck = conv_silu(k_ref, wk_ref, n, r0)
        cv = conv_silu(v_ref, wv_ref, n, r0)
        qn = cq * lax.rsqrt(jnp.sum(cq * cq, axis=-1, keepdims=True) + NORM_EPS) * (GDN_DIM ** -0.5)
        kn = ck * lax.rsqrt(jnp.sum(ck * ck, axis=-1, keepdims=True) + NORM_EPS)
        sm = sm_ref[0, pl.ds(r0, c_len), :]
        z = sm + dtb
        g_all = neg_a * (jnp.maximum(z, 0.0) + jnp.log1p(jnp.exp(-jnp.abs(z))))
        beta_all = 1.0 / (1.0 + jnp.exp(-sm))
        gc_all = jnp.dot(tri, g_all, precision=HIGHEST, preferred_element_type=F32)
        gc = jnp.sum(jnp.where(lane == SMALL_GA + h, gc_all, 0.0), axis=-1, keepdims=True)
        beta = jnp.sum(jnp.where(lane == SMALL_GB + h, beta_all, 0.0), axis=-1, keepdims=True)
        gc_row = jnp.sum(gc * eye, axis=0, keepdims=True)
        decay = jnp.where(incl, jnp.exp(jnp.where(incl, gc - gc_row, 0.0)), 0.0)
        kb = kn * beta
        both = _mm_nt(jnp.concatenate([kb, qn], axis=0), kn)
        low = jnp.where(strict, both[:c_len] * decay, 0.0)
        attn = jnp.where(incl, both[c_len:] * decay, 0.0)
        inv = eye - low
        pw = low
        for _ in range(int(math.log2(c_len)) - 1):
            pw = _mm(pw, pw)
            inv = inv + _mm(inv, pw)
        eg = jnp.exp(gc)
        sol = _mm(inv, jnp.concatenate([cv * beta, kb * eg], axis=1))
        g_last = gc[c_len - 1:c_len, :]
        u0_ref[0, 0, pl.ds(r0, c_len), :] = sol[:, :GDN_DIM]
        w_ref[0, 0, pl.ds(r0, c_len), :] = sol[:, GDN_DIM:].astype(BF16)
        qg_ref[0, 0, pl.ds(r0, c_len), :] = (qn * eg).astype(BF16)
        kt_ref[0, 0, pl.ds(r0, c_len), :] = (kn * jnp.exp(g_last - gc)).astype(BF16)
        at_ref[0, 0, pl.ds(r0, c_len), :] = attn.astype(BF16)
        eg_ref[0, 0, n] = jnp.broadcast_to(jnp.exp(g_last), (1, LANES))
        return carry

    lax.fori_loop(0, n_chunks, chunk, 0)


def _gdn_prep(rest3, conv_w, alog_l, dtb_l):
    b, s, _ = rest3.shape
    nh = GDN_HEADS
    n_chunks = s // GDN_CHUNK
    col = lambda off: (lambda bi, h: (bi, 0, off // LANES + h))
    wcol = lambda g: (lambda bi, h: (0, g * nh + h))
    const = lambda bi, h: (0, 0)
    tri = jnp.asarray((np.arange(GDN_CHUNK)[:, None] >= np.arange(GDN_CHUNK)[None, :]).astype(np.float32))
    row = pl.BlockSpec((1, 1, s, GDN_DIM), lambda bi, h: (bi, h, 0, 0))
    return pl.pallas_call(
        _gdn_prep_kernel,
        grid=(b, nh),
        in_specs=[pl.BlockSpec((1, s, LANES), col(REST_GQ)),
                  pl.BlockSpec((1, s, LANES), col(REST_GK)),
                  pl.BlockSpec((1, s, LANES), col(REST_GV)),
                  pl.BlockSpec((1, s, LANES), lambda bi, h: (bi, 0, REST_SMALL // LANES)),
                  pl.BlockSpec((CONV_WIDTH, LANES), wcol(0)),
                  pl.BlockSpec((CONV_WIDTH, LANES), wcol(1)),
                  pl.BlockSpec((CONV_WIDTH, LANES), wcol(2)),
                  pl.BlockSpec((1, LANES), const),
                  pl.BlockSpec((1, LANES), const),
                  pl.BlockSpec((GDN_CHUNK, GDN_CHUNK), const)],
        out_specs=[row, row, row, row,
                   pl.BlockSpec((1, 1, s, GDN_CHUNK), lambda bi, h: (bi, h, 0, 0)),
                   pl.BlockSpec((1, 1, n_chunks, 1, LANES), lambda bi, h: (bi, h, 0, 0, 0))],
        out_shape=[jax.ShapeDtypeStruct((b, nh, s, GDN_DIM), F32),
                   jax.ShapeDtypeStruct((b, nh, s, GDN_DIM), BF16),
                   jax.ShapeDtypeStruct((b, nh, s, GDN_DIM), BF16),
                   jax.ShapeDtypeStruct((b, nh, s, GDN_DIM), BF16),
                   jax.ShapeDtypeStruct((b, nh, s, GDN_CHUNK), BF16),
                   jax.ShapeDtypeStruct((b, nh, n_chunks, 1, LANES), F32)],
        compiler_params=_params("parallel", "parallel"),
        name="gdn_prep",
    )(rest3, rest3, rest3, rest3, conv_w, conv_w, conv_w, alog_l, dtb_l, tri)


def _gdn_scan_kernel(u0_ref, w_ref, qg_ref, kt_ref, at_ref, eg_ref, z_ref, nw_ref, o_ref, state_ref, *, ts):
    si = pl.program_id(1)

    @pl.when(si == 0)
    def _():
        state_ref[...] = jnp.zeros_like(state_ref)

    c_len = GDN_CHUNK
    per_tile = ts // c_len
    nw = nw_ref[...]
    for c in range(per_tile):
        rows = slice(c * c_len, (c + 1) * c_len)
        for h in range(GDN_HEADS):
            st = state_ref[h]
            r = _mm(jnp.concatenate([w_ref[0, h, rows, :], qg_ref[0, h, rows, :]], axis=0), st)
            u = (u0_ref[0, h, rows, :] - r[:c_len]).astype(BF16)
            o = r[c_len:] + jnp.dot(at_ref[0, h, rows, :], u, preferred_element_type=F32)
            eg = eg_ref[0, h, si * per_tile + c]
            state_ref[h] = st * eg + _mm_tn(kt_ref[0, h, rows, :], u)
            cols = slice(h * GDN_DIM, (h + 1) * GDN_DIM)
            y = _rms(o, nw) * _silu(z_ref[0, rows, cols])
            o_ref[0, rows, cols] = y.astype(BF16)


def _gdn_scan(u0, w, qg, kt, at, eg, rest3, norm_w, ts=512):
    b, nh, s, _ = u0.shape
    n_chunks = s // GDN_CHUNK
    blk = lambda d: pl.BlockSpec((1, nh, ts, d), lambda bi, i: (bi, 0, i, 0))
    kern = functools.partial(_gdn_scan_kernel, ts=ts)
    return pl.pallas_call(
        kern,
        grid=(b, s // ts),
        in_specs=[blk(GDN_DIM), blk(GDN_DIM), blk(GDN_DIM), blk(GDN_DIM), blk(GDN_CHUNK),
                  pl.BlockSpec((1, nh, n_chunks, 1, LANES), lambda bi, i: (bi, 0, 0, 0, 0)),
                  pl.BlockSpec((1, ts, GDN_WIDTH), lambda bi, i: (bi, i, REST_GZ // GDN_WIDTH)),
                  pl.BlockSpec((1, GDN_DIM), lambda bi, i: (0, 0))],
        out_specs=pl.BlockSpec((1, ts, GDN_WIDTH), lambda bi, i: (bi, i, 0)),
        out_shape=jax.ShapeDtypeStruct((b, s, GDN_WIDTH), BF16),
        scratch_shapes=[pltpu.VMEM((nh, GDN_DIM, GDN_DIM), F32)],
        compiler_params=_params("parallel", "arbitrary"),
        name="gdn_scan",
    )(u0, w, qg, kt, at, eg, rest3, norm_w)


def _outproj_kernel(x_ref, of_ref, or_ref, og_ref, w_ref, o_ref):
    acc = x_ref[...]
    acc = acc + jnp.dot(of_ref[...], w_ref[0:FOX_WIDTH, :], preferred_element_type=F32)
    acc = acc + jnp.dot(or_ref[...], w_ref[FOX_WIDTH:FOX_WIDTH + RET_WIDTH, :], preferred_element_type=F32)
    acc = acc + jnp.dot(og_ref[...], w_ref[FOX_WIDTH + RET_WIDTH:, :], preferred_element_type=F32)
    o_ref[...] = acc


def _outproj(x, o_fox, o_ret, o_gdn, w_out, tm=512):
    t = x.shape[0]
    d_mix = w_out.shape[0]
    return pl.pallas_call(
        _outproj_kernel,
        grid=(t // tm,),
        in_specs=[pl.BlockSpec((tm, D_MODEL), lambda i: (i, 0)),
                  pl.BlockSpec((tm, FOX_WIDTH), lambda i: (i, 0)),
                  pl.BlockSpec((tm, RET_WIDTH), lambda i: (i, 0)),
                  pl.BlockSpec((tm, GDN_WIDTH), lambda i: (i, 0)),
                  pl.BlockSpec((d_mix, D_MODEL), lambda i: (0, 0))],
        out_specs=pl.BlockSpec((tm, D_MODEL), lambda i: (i, 0)),
        out_shape=jax.ShapeDtypeStruct((t, D_MODEL), F32),
        compiler_params=_params("parallel"),
        name="outproj",
    )(x, o_fox, o_ret, o_gdn, w_out)


def _router_kernel(x_ref, nw_ref, w_ref, b_ref, g_ref):
    hn = _rms(x_ref[...], nw_ref[...])
    logits = jnp.dot(hn, w_ref[...], precision=HIGHEST, preferred_element_type=F32) + b_ref[...]
    lane = lax.broadcasted_iota(jnp.int32, logits.shape, 1).astype(F32)
    neg = -jnp.inf
    gl = jnp.where(lane < ROUTER_EXP, logits, neg)
    gmax = jnp.max(gl, axis=-1, keepdims=True)
    gidx = jnp.min(jnp.where(gl == gmax, lane, LANES), axis=-1, keepdims=True)
    grp_p = 1.0 / jnp.sum(jnp.exp(gl - gmax), axis=-1, keepdims=True)
    lo = ROUTER_EXP + gidx * EXPERTS_PER_GROUP
    el = jnp.where((lane >= lo) & (lane < lo + EXPERTS_PER_GROUP), logits, neg)
    m1 = jnp.max(el, axis=-1, keepdims=True)
    i1 = jnp.min(jnp.where(el == m1, lane, LANES), axis=-1, keepdims=True)
    el2 = jnp.where(lane == i1, neg, el)
    m2 = jnp.max(el2, axis=-1, keepdims=True)
    i2 = jnp.min(jnp.where(el2 == m2, lane, LANES), axis=-1, keepdims=True)
    e2 = jnp.exp(m2 - m1)
    w1 = grp_p / (1.0 + e2)
    g_ref[...] = jnp.where(lane == i1, w1, 0.0) + jnp.where(lane == i2, w1 * e2, 0.0)


def _router(x, nw, w_pack, b_pack, tm=512):
    t = x.shape[0]
    return pl.pallas_call(
        _router_kernel,
        grid=(t // tm,),
        in_specs=[pl.BlockSpec((tm, D_MODEL), lambda i: (i, 0)),
                  pl.BlockSpec((1, D_MODEL), lambda i: (0, 0)),
                  pl.BlockSpec((D_MODEL, LANES), lambda i: (0, 0)),
                  pl.BlockSpec((1, LANES), lambda i: (0, 0))],
        out_specs=pl.BlockSpec((tm, LANES), lambda i: (i, 0)),
        out_shape=jax.ShapeDtypeStruct((t, LANES), F32),
        compiler_params=_params("parallel"),
        name="router",
    )(x, nw, w_pack, b_pack)


def _moe_kernel(x_ref, nw_ref, g_ref, w1_ref, w3_ref, w2_ref, fw_ref, o_ref, hn_ref, *, final_norm):
    e = pl.program_id(1)

    @pl.when(e == 0)
    def _():
        x = x_ref[...]
        hn_ref[...] = _rms(x, nw_ref[...]).astype(BF16)
        o_ref[...] = x

    hn = hn_ref[...]
    a = jnp.dot(hn, w1_ref[0], preferred_element_type=F32)
    u = jnp.dot(hn, w3_ref[0], preferred_element_type=F32)
    lane = lax.broadcasted_iota(jnp.int32, (1, LANES), 1)
    gate = jnp.sum(jnp.where(lane == ROUTER_EXP + e, g_ref[...], 0.0), axis=-1, keepdims=True)
    hid = _silu(a) * u * gate
    o_ref[...] += jnp.dot(hid.astype(BF16), w2_ref[0], preferred_element_type=F32)

    if final_norm:
        @pl.when(e == pl.num_programs(1) - 1)
        def _():
            o_ref[...] = _rms(o_ref[...], fw_ref[...])


def _moe(x, nw, gates, w1, w3, w2, fw, final_norm, tm=1024):
    t = x.shape[0]
    kern = functools.partial(_moe_kernel, final_norm=final_norm)
    return pl.pallas_call(
        kern,
        grid=(t // tm, N_EXPERTS),
        in_specs=[pl.BlockSpec((tm, D_MODEL), lambda i, e: (i, 0)),
                  pl.BlockSpec((1, D_MODEL), lambda i, e: (0, 0)),
                  pl.BlockSpec((tm, LANES), lambda i, e: (i, 0)),
                  pl.BlockSpec((1, D_MODEL, EXPERT_FF), lambda i, e: (e, 0, 0)),
                  pl.BlockSpec((1, D_MODEL, EXPERT_FF), lambda i, e: (e, 0, 0)),
                  pl.BlockSpec((1, EXPERT_FF, D_MODEL), lambda i, e: (e, 0, 0)),
                  pl.BlockSpec((1, D_MODEL), lambda i, e: (0, 0))],
        out_specs=pl.BlockSpec((tm, D_MODEL), lambda i, e: (i, 0)),
        out_shape=jax.ShapeDtypeStruct((t, D_MODEL), F32),
        scratch_shapes=[pltpu.VMEM((tm, D_MODEL), BF16)],
        compiler_params=_params("parallel", "arbitrary"),
        name="moe",
    )(x, nw, gates, w1, w3, w2, fw)


def _pack_in_weights(w_in_l):
    cuts = np.cumsum(IN_SPLITS)[:-1].tolist()
    fq, fk, fv, ff, rq, rk, rv, rg, gq, gk, gv, gz, ga, gb = jnp.split(w_in_l, cuts, axis=1)
    half = HEAD64 // 2
    perm = np.concatenate([(2 * p + hh) * HEAD64 + lo * half + np.arange(half)
                           for p in range(RET_HEADS // 2) for lo in range(2) for hh in range(2)])
    small = jnp.concatenate([ff, ga, gb, jnp.zeros((D_MODEL, LANES - 12), w_in_l.dtype)], axis=1)
    wf = jnp.concatenate([fq, fk, fv], axis=1).astype(BF16)
    wr = jnp.concatenate([rq[:, perm], rk[:, perm], rv, rg, gq, gk, gv, gz, small], axis=1).astype(BF16)
    return wf, wr


def _lane_row(vals, offset):
    return jnp.zeros((1, LANES), F32).at[0, offset:offset + vals.shape[0]].set(vals.astype(F32))


def kernel(x, norm1_w, w_in, fox_forget_bias, gdn_conv_w, gdn_a_log, gdn_dt_bias, gdn_norm_w, w_out, norm2_w,
           router_group_w, router_group_b, router_expert_w, router_expert_b, expert_w1, expert_w3, expert_w2,
           final_norm_w):
    b, s, d = x.shape
    t = b * s
    depth = w_in.shape[0]
    fox_tk = 512
    xt = x.reshape(t, d)
    ret_tables = _ret_tables(s)
    for l in range(depth):
        wf, wr = _pack_in_weights(w_in[l])
        qkv, rest = _inproj(xt, norm1_w[l].reshape(1, d), wf, wr)
        rest3 = rest.reshape(b, s, REST_WIDTH)
        ff = rest3[:, :, REST_SMALL + SMALL_FF:REST_SMALL + SMALL_FF + FOX_HEADS]
        ff4 = ff.transpose(0, 2, 1).reshape(b, FOX_HEADS, s // LANES, LANES)
        bias_rows = jnp.repeat(fox_forget_bias[l].astype(F32), s // LANES).reshape(-1, 1)
        c = _fgate(ff4, bias_rows).reshape(b, FOX_HEADS, s // fox_tk, 1, fox_tk)
        o_fox = _fox(qkv.reshape(b, s, 3 * FOX_WIDTH), c, tk=fox_tk)
        o_ret = _ret(rest3, ret_tables)
        prep = _gdn_prep(rest3, gdn_conv_w[l].astype(F32), _lane_row(gdn_a_log[l], SMALL_GA),
                         _lane_row(gdn_dt_bias[l], SMALL_GA))
        o_gdn = _gdn_scan(*prep, rest3, gdn_norm_w[l].reshape(1, GDN_DIM).astype(F32))
        xt = _outproj(xt, o_fox.reshape(t, FOX_WIDTH), o_ret.reshape(t, RET_WIDTH),
                      o_gdn.reshape(t, GDN_WIDTH), w_out[l].astype(BF16))
        w_pack = jnp.concatenate([router_group_w[l], router_expert_w[l],
                                  jnp.zeros((d, LANES - N_GROUPS - N_EXPERTS), F32)], axis=1)
        b_pack = jnp.concatenate([router_group_b[l].reshape(-1), router_expert_b[l].reshape(-1),
                                  jnp.zeros((LANES - N_GROUPS - N_EXPERTS,), F32)]).reshape(1, LANES)
        gates = _router(xt, norm2_w[l].reshape(1, d), w_pack, b_pack)
        xt = _moe(xt, norm2_w[l].reshape(1, d), gates,
                  expert_w1[l].reshape(N_EXPERTS, d, EXPERT_FF).astype(BF16),
                  expert_w3[l].reshape(N_EXPERTS, d, EXPERT_FF).astype(BF16),
                  expert_w2[l].reshape(N_EXPERTS, EXPERT_FF, d).astype(BF16),
                  final_norm_w.reshape(1, d), final_norm=(l == depth - 1))
    return xt.reshape(b, s, d)
```

```python
import functools
import math

import jax
import jax.numpy as jnp
import numpy as np
from jax import lax
from jax.experimental import pallas as pl
from jax.experimental.pallas import tpu as pltpu

F32 = jnp.float32
BF16 = jnp.bfloat16
HIGHEST = lax.Precision.HIGHEST

D_MODEL = 1024
FOX_HEADS = 4
RET_HEADS = 4
GDN_HEADS = 4
HEAD64 = 64
GDN_DIM = 128
FOX_WIDTH = FOX_HEADS * HEAD64
RET_WIDTH = RET_HEADS * HEAD64
GDN_WIDTH = GDN_HEADS * GDN_DIM
RET_CHUNK = 128
GDN_CHUNK = 64
GDN_PREP_GROUP = 8
CONV_WIDTH = 4
RET_ANGLE_BASE = 10000.0
N_GROUPS = 4
EXPERTS_PER_GROUP = 8
N_EXPERTS = N_GROUPS * EXPERTS_PER_GROUP
EXPERT_FF = 256
NORM_EPS = 1e-6
LANES = 128
IN_SPLITS = (FOX_WIDTH, FOX_WIDTH, FOX_WIDTH, FOX_HEADS,
             RET_WIDTH, RET_WIDTH, RET_WIDTH, RET_WIDTH,
             GDN_WIDTH, GDN_WIDTH, GDN_WIDTH, GDN_WIDTH, GDN_HEADS, GDN_HEADS)

REST_RQ, REST_RK, REST_RV, REST_RG = 0, 256, 512, 768
REST_GQ, REST_GK, REST_GV, REST_GZ = 1024, 1536, 2048, 2560
REST_SMALL = 3072
REST_WIDTH = 3200
SMALL_FF, SMALL_GA, SMALL_GB = 0, 4, 8
ROUTER_GRP, ROUTER_EXP = 0, 4

VMEM_LIMIT = 56 * 1024 * 1024


def _params(*sem):
    return pltpu.CompilerParams(dimension_semantics=sem, vmem_limit_bytes=VMEM_LIMIT)


def _mm(a, b):
    return jnp.dot(a.astype(BF16), b.astype(BF16), preferred_element_type=F32)


def _mm_nt(a, b):
    return lax.dot_general(a.astype(BF16), b.astype(BF16), (((1,), (1,)), ((), ())),
                           preferred_element_type=F32)


def _mm_tn(a, b):
    return lax.dot_general(a.astype(BF16), b.astype(BF16), (((0,), (0,)), ((), ())),
                           preferred_element_type=F32)


def _silu(x):
    return x * (1.0 / (1.0 + jnp.exp(-x)))


def _rms(x, w):
    return x * lax.rsqrt(jnp.mean(x * x, axis=-1, keepdims=True) + NORM_EPS) * w


def _inproj_kernel(x_ref, nw_ref, wf_ref, wr_ref, of_ref, or_ref):
    hn = _rms(x_ref[...], nw_ref[...]).astype(BF16)
    of_ref[...] = jnp.dot(hn, wf_ref[...], preferred_element_type=F32).astype(BF16)
    step = 640
    for c in range(0, REST_WIDTH, step):
        or_ref[:, c:c + step] = jnp.dot(hn, wr_ref[:, c:c + step], preferred_element_type=F32)


def _inproj(x, nw, wf, wr, tm=512):
    t = x.shape[0]
    return pl.pallas_call(
        _inproj_kernel,
        grid=(t // tm,),
        in_specs=[pl.BlockSpec((tm, D_MODEL), lambda i: (i, 0)),
                  pl.BlockSpec((1, D_MODEL), lambda i: (0, 0)),
                  pl.BlockSpec((D_MODEL, 3 * FOX_WIDTH), lambda i: (0, 0)),
                  pl.BlockSpec((D_MODEL, REST_WIDTH), lambda i: (0, 0))],
        out_specs=[pl.BlockSpec((tm, 3 * FOX_WIDTH), lambda i: (i, 0)),
                   pl.BlockSpec((tm, REST_WIDTH), lambda i: (i, 0))],
        out_shape=[jax.ShapeDtypeStruct((t, 3 * FOX_WIDTH), BF16),
                   jax.ShapeDtypeStruct((t, REST_WIDTH), F32)],
        compiler_params=_params("parallel"),
        name="inproj",
    )(x, nw, wf, wr)


def _fgate_kernel(ff_ref, bias_ref, tri_ref, blk_ref, c_ref):
    rows = ff_ref.shape[1] * ff_ref.shape[2]
    z = ff_ref[0].reshape(rows, LANES) + bias_ref[...]
    lf = jnp.minimum(z, 0.0) - jnp.log1p(jnp.exp(-jnp.abs(z)))
    within = jnp.dot(lf, tri_ref[...], precision=HIGHEST, preferred_element_type=F32)
    tot = jnp.broadcast_to(within[:, LANES - 1:LANES], (rows, LANES))
    before = jnp.dot(blk_ref[...], tot, precision=HIGHEST, preferred_element_type=F32)
    c_ref[0] = (within + before).reshape(c_ref.shape[1:])


def _fgate(ff4, bias_rows):
    b, h, r, _ = ff4.shape
    rows = h * r
    tri = (np.arange(LANES)[:, None] <= np.arange(LANES)[None, :]).astype(np.float32)
    i = np.arange(rows)
    blk = ((i[:, None] // r == i[None, :] // r) & (i[None, :] < i[:, None])).astype(np.float32)
    return pl.pallas_call(
        _fgate_kernel,
        grid=(b,),
        in_specs=[pl.BlockSpec((1, h, r, LANES), lambda i: (i, 0, 0, 0)),
                  pl.BlockSpec((rows, 1), lambda i: (0, 0)),
                  pl.BlockSpec((LANES, LANES), lambda i: (0, 0)),
                  pl.BlockSpec((rows, rows), lambda i: (0, 0))],
        out_specs=pl.BlockSpec((1, h, r, LANES), lambda i: (i, 0, 0, 0)),
        out_shape=jax.ShapeDtypeStruct((b, h, r, LANES), F32),
        compiler_params=_params("parallel"),
        name="fgate",
    )(ff4, bias_rows, jnp.asarray(tri), jnp.asarray(blk))


def _fox_kernel(q_ref, k_ref, v_ref, c_ref, o_ref, *, tq, tk):
    i = pl.program_id(2)
    lane = lax.broadcasted_iota(jnp.int32, (1, LANES), 1)
    first = lane < HEAD64
    q = q_ref[0] * jnp.asarray(HEAD64 ** -0.5, BF16)
    zero = jnp.zeros_like(q)
    qh = (jnp.where(first, q, zero), jnp.where(first, zero, q))
    nfull = (i * tq) // tk
    cbase = [c_ref[0, hh, nfull][:, 0:1] for hh in range(2)]
    qpos = i * tq + lax.broadcasted_iota(jnp.int32, (tq, 1), 0)

    def step(j, carry, masked):
        k0 = pl.multiple_of(j * tk, tk)
        k = k_ref[0, pl.ds(k0, tk), :]
        v = v_ref[0, pl.ds(k0, tk), :]
        out = []
        for hh in range(2):
            m, l, acc = carry[hh]
            s = lax.dot_general(qh[hh], k, (((1,), (1,)), ((), ())), preferred_element_type=F32)
            s = s + (cbase[hh] - c_ref[0, hh, j])
            if masked:
                kpos = j * tk + lax.broadcasted_iota(jnp.int32, (1, tk), 1)
                s = jnp.where(kpos <= qpos, s, -jnp.inf)
            m_new = jnp.maximum(m, jnp.max(s, axis=-1, keepdims=True))
            alpha = jnp.exp(m - m_new)
            p = jnp.exp(s - m_new)
            l = alpha * l + jnp.sum(p, axis=-1, keepdims=True)
            acc = alpha * acc + jnp.dot(p.astype(BF16), v, preferred_element_type=F32)
            out.append((m_new, l, acc))
        return tuple(out)

    init = tuple((jnp.full((tq, 1), -jnp.inf, F32), jnp.zeros((tq, 1), F32), jnp.zeros((tq, LANES), F32))
                 for _ in range(2))
    carry = lax.fori_loop(0, nfull, functools.partial(step, masked=False), init)
    carry = step(nfull, carry, True)
    o0 = carry[0][2] / carry[0][1]
    o1 = carry[1][2] / carry[1][1]
    o_ref[0] = jnp.where(first, o0, o1).astype(BF16)


def _fox(qkv, c, tq=256, tk=512):
    b, s, _ = qkv.shape
    npair = FOX_HEADS // 2
    kern = functools.partial(_fox_kernel, tq=tq, tk=tk)
    return pl.pallas_call(
        kern,
        grid=(b, npair, s // tq),
        in_specs=[pl.BlockSpec((1, tq, LANES), lambda bi, p, i: (bi, i, p)),
                  pl.BlockSpec((1, s, LANES), lambda bi, p, i: (bi, 0, npair + p)),
                  pl.BlockSpec((1, s, LANES), lambda bi, p, i: (bi, 0, 2 * npair + p)),
                  pl.BlockSpec((1, 2, s // tk, 1, tk), lambda bi, p, i: (bi, p, 0, 0, 0))],
        out_specs=pl.BlockSpec((1, tq, LANES), lambda bi, p, i: (bi, i, p)),
        out_shape=jax.ShapeDtypeStruct((b, s, FOX_WIDTH), BF16),
        compiler_params=_params("parallel", "parallel", "arbitrary"),
        name="fox",
    )(qkv, qkv, qkv, c)


def _ret_kernel(q_ref, k_ref, v_ref, g_ref, cos_ref, sin_ref, dmat_ref, qdec_ref, kdec_ref, cd_ref, bm_ref,
                o_ref, state_ref, *, ts):
    @pl.when(pl.program_id(2) == 0)
    def _():
        state_ref[...] = jnp.zeros_like(state_ref)

    lane = lax.broadcasted_iota(jnp.int32, (1, LANES), 1)
    q_first = (lane % HEAD64) < (HEAD64 // 2)
    v_first = lane < HEAD64
    c_len = RET_CHUNK
    for c in range(ts // c_len):
        rows = slice(c * c_len, (c + 1) * c_len)
        cos = cos_ref[rows, :]
        sin = sin_ref[rows, :]
        q = q_ref[0, rows, :]
        k = k_ref[0, rows, :]
        v = v_ref[0, rows, :]
        qr = q * cos + pltpu.roll(q, LANES // 2, 1) * sin
        kr = k * cos + pltpu.roll(k, LANES // 2, 1) * sin
        q2 = jnp.concatenate([jnp.where(q_first, qr, 0.0), jnp.where(q_first, 0.0, qr)], axis=0)
        s = _mm_nt(q2, kr * (HEAD64 ** -0.5))
        s0 = s[:c_len] * dmat_ref[0]
        s1 = s[c_len:] * dmat_ref[1]
        o = _mm(s0, jnp.where(v_first, v, 0.0)) + _mm(s1, jnp.where(v_first, 0.0, v))
        state = state_ref[...]
        o = o + _mm(qr * qdec_ref[0], state)
        kv = _mm_tn(kr * kdec_ref[0], v)
        state_ref[...] = state * cd_ref[0] + kv * bm_ref[0]
        sq = o * o
        ms0 = jnp.sum(jnp.where(v_first, sq, 0.0), axis=-1, keepdims=True)
        ms1 = jnp.sum(jnp.where(v_first, 0.0, sq), axis=-1, keepdims=True)
        ms = jnp.where(v_first, ms0, ms1) * (1.0 / HEAD64)
        y = o * lax.rsqrt(ms + NORM_EPS) * _silu(g_ref[0, rows, :])
        o_ref[0, rows, :] = y.astype(BF16)


def _ret_tables(s):
    npair = RET_HEADS // 2
    half = HEAD64 // 2
    lane = np.arange(LANES)
    log_g = np.log1p(-np.exp2(-5.0 - np.arange(RET_HEADS, dtype=np.float32))).astype(np.float32)
    idx = np.arange(RET_CHUNK, dtype=np.float32)
    rel = idx[:, None] - idx[None, :]
    dmat = np.where(rel[None] >= 0, np.exp(np.maximum(rel, 0.0)[None] * log_g[:, None, None]), 0.0)
    qdec, kdec, cd, bm = [], [], [], []
    for p in range(npair):
        hq = 2 * p + ((lane % HEAD64) >= half)
        hv = 2 * p + (lane >= HEAD64)
        qdec.append(np.exp((idx[:, None] + 1.0) * log_g[hq][None, :]))
        kdec.append(np.exp((RET_CHUNK - 1 - idx)[:, None] * log_g[hq][None, :]) * HEAD64 ** -0.5)
        cd.append(np.broadcast_to(np.exp(RET_CHUNK * log_g[hq])[:, None], (LANES, LANES)))
        bm.append((hq[:, None] == hv[None, :]).astype(np.float32))
    tabs = [np.stack(a).astype(np.float32) for a in (qdec, kdec, cd, bm)]
    inv = 1.0 / (RET_ANGLE_BASE ** jnp.linspace(0.0, 1.0, half, dtype=F32))
    pos = jnp.arange(s, dtype=F32)
    ang = pos[:, None] * inv[None, :]
    cos = jnp.tile(jnp.cos(ang), (1, LANES // half))
    sin = jnp.tile(jnp.sin(ang), (1, LANES // half))
    sin = jnp.where(jnp.asarray(lane)[None, :] < LANES // 2, -sin, sin)
    return [jnp.asarray(dmat.astype(np.float32))] + [jnp.asarray(a) for a in tabs] + [cos, sin]


def _ret(rest3, tables, ts=1024):
    b, s, _ = rest3.shape
    dmat, qdec, kdec, cd, bm, cos, sin = tables
    npair = RET_HEADS // 2
    col = lambda off: (lambda bi, p, i: (bi, i, off // LANES + p))
    tab = lambda bi, p, i: (p, 0, 0)
    kern = functools.partial(_ret_kernel, ts=ts)
    return pl.pallas_call(
        kern,
        grid=(b, npair, s // ts),
        in_specs=[pl.BlockSpec((1, ts, LANES), col(REST_RQ)),
                  pl.BlockSpec((1, ts, LANES), col(REST_RK)),
                  pl.BlockSpec((1, ts, LANES), col(REST_RV)),
                  pl.BlockSpec((1, ts, LANES), col(REST_RG)),
                  pl.BlockSpec((ts, LANES), lambda bi, p, i: (i, 0)),
                  pl.BlockSpec((ts, LANES), lambda bi, p, i: (i, 0)),
                  pl.BlockSpec((2, RET_CHUNK, RET_CHUNK), tab),
                  pl.BlockSpec((1, RET_CHUNK, LANES), tab),
                  pl.BlockSpec((1, RET_CHUNK, LANES), tab),
                  pl.BlockSpec((1, LANES, LANES), tab),
                  pl.BlockSpec((1, LANES, LANES), tab)],
        out_specs=pl.BlockSpec((1, ts, LANES), lambda bi, p, i: (bi, i, p)),
        out_shape=jax.ShapeDtypeStruct((b, s, RET_WIDTH), BF16),
        scratch_shapes=[pltpu.VMEM((LANES, LANES), F32)],
        compiler_params=_params("parallel", "parallel", "arbitrary"),
        name="retention",
    )(rest3, rest3, rest3, rest3, cos, sin, dmat, qdec, kdec, cd, bm)


def _gdn_prep_kernel(q_ref, k_ref, v_ref, sm_ref, wq_ref, wk_ref, wv_ref, alog_ref, dtb_ref,
                     u0_ref, w_ref, qg_ref, kt_ref, at_ref, eg_ref):
    h = pl.program_id(1)
    c_len = GDN_CHUNK
    n_chunks = q_ref.shape[1] // c_len
    lane = lax.broadcasted_iota(jnp.int32, (1, LANES), 1)
    ri = lax.broadcasted_iota(jnp.int32, (c_len, c_len), 0)
    ci = lax.broadcasted_iota(jnp.int32, (c_len, c_len), 1)
    incl = ri >= ci
    strict = ri > ci
    eye = (ri == ci).astype(F32)
    neg_a = -jnp.exp(alog_ref[...])
    dtb = dtb_ref[...]

    grp = GDN_PREP_GROUP
    rows = grp * c_len

    def conv_silu(ref, w_ref_, n, r0):
        cur = ref[0, pl.ds(r0, rows), :]
        p0 = pl.multiple_of(jnp.maximum(r0 - 8, 0), 8)
        prev = ref[0, pl.ds(p0, 8), :]
        prev = jnp.where(jnp.broadcast_to(n > 0, prev.shape), prev, 0.0)
        xc = jnp.concatenate([prev, cur], axis=0)
        w = w_ref_[...]
        y = cur * w[CONV_WIDTH - 1:CONV_WIDTH, :]
        for j in range(CONV_WIDTH - 1):
            shifted = pltpu.roll(xc, CONV_WIDTH - 1 - j, 0)[8:, :]
            y = y + shifted * w[j:j + 1, :]
        return _silu(y)

    def group(n, carry):
        r0 = pl.multiple_of(n * rows, rows)
        cq = conv_silu(q_ref, wq_ref, n, r0)
        ck = conv_silu(k_ref, wk_ref, n, r0)
        cv = conv_silu(v_ref, wv_ref, n, r0)
        qn = cq * lax.rsqrt(jnp.sum(cq * cq, axis=-1, keepdims=True) + NORM_EPS) * (GDN_DIM ** -0.5)
        kn = ck * lax.rsqrt(jnp.sum(ck * ck, axis=-1, keepdims=True) + NORM_EPS)
        sm = sm_ref[0, pl.ds(r0, rows), :]
        z = sm + dtb
        g_all = neg_a * (jnp.maximum(z, 0.0) + jnp.log1p(jnp.exp(-jnp.abs(z))))
        beta_all = 1.0 / (1.0 + jnp.exp(-sm))
        g_col = jnp.sum(jnp.where(lane == SMALL_GA + h, g_all, 0.0), axis=-1, keepdims=True)
        beta = jnp.sum(jnp.where(lane == SMALL_GB + h, beta_all, 0.0), axis=-1, keepdims=True)
        kb = kn * beta
        vb = cv * beta
        chunks = [slice(g * c_len, (g + 1) * c_len) for g in range(grp)]
        g_row = [jnp.sum(g_col[c] * eye, axis=0, keepdims=True) for c in chunks]
        gc = [jnp.sum(jnp.where(incl, g_row[i], 0.0), axis=-1, keepdims=True) for i in range(grp)]
        gc_row = [jnp.sum(jnp.where(ri <= ci, g_col[c], 0.0), axis=0, keepdims=True) for c in chunks]
        decay = [jnp.where(incl, jnp.exp(jnp.where(incl, gc[i] - gc_row[i], 0.0)), 0.0) for i in range(grp)]
        both = [_mm_nt(jnp.concatenate([kb[c], qn[c]], axis=0), kn[c]) for c in chunks]
        low = [jnp.where(strict, both[i][:c_len] * decay[i], 0.0) for i in range(grp)]
        attn = [jnp.where(incl, both[i][c_len:] * decay[i], 0.0) for i in range(grp)]
        inv = [eye - low[i] for i in range(grp)]
        pw = low
        for _ in range(int(math.log2(c_len)) - 1):
            pw = [_mm(pw[i], pw[i]) for i in range(grp)]
            inv = [inv[i] + _mm(inv[i], pw[i]) for i in range(grp)]
        eg = [jnp.exp(gc[i]) for i in range(grp)]
        sol = [_mm(inv[i], jnp.concatenate([vb[c], kb[c] * eg[i]], axis=1)) for i, c in enumerate(chunks)]
        for i, c in enumerate(chunks):
            dst = pl.ds(pl.multiple_of(r0 + i * c_len, c_len), c_len)
            g_last = gc[i][c_len - 1:c_len, :]
            u0_ref[0, 0, dst, :] = sol[i][:, :GDN_DIM]
            w_ref[0, 0, dst, :] = sol[i][:, GDN_DIM:].astype(BF16)
            qg_ref[0, 0, dst, :] = (qn[c] * eg[i]).astype(BF16)
            kt_ref[0, 0, dst, :] = (kn[c] * jnp.exp(g_last - gc[i])).astype(BF16)
            at_ref[0, 0, dst, :] = attn[i].astype(BF16)
            eg_ref[0, 0, n * grp + i] = jnp.broadcast_to(jnp.exp(g_last), (1, LANES))
        return carry

    lax.fori_loop(0, n_chunks // grp, group, 0)


def _gdn_prep(rest3, conv_w, alog_l, dtb_l):
    b, s, _ = rest3.shape
    nh = GDN_HEADS
    n_chunks = s // GDN_CHUNK
    col = lambda off: (lambda bi, h: (bi, 0, off // LANES + h))
    wcol = lambda g: (lambda bi, h: (0, g * nh + h))
    const = lambda bi, h: (0, 0)
    row =pl.BlockSpec((1, 1, s, GDN_DIM), lambda bi, h: (bi, h, 0, 0))
    return pl.pallas_call(
        _gdn_prep_kernel,
        grid=(b, nh),
        in_specs=[pl.BlockSpec((1, s, LANES), col(REST_GQ)),
                  pl.BlockSpec((1, s, LANES), col(REST_GK)),
                  pl.BlockSpec((1, s, LANES), col(REST_GV)),
                  pl.BlockSpec((1, s, LANES), lambda bi, h: (bi, 0, REST_SMALL // LANES)),
                  pl.BlockSpec((CONV_WIDTH, LANES), wcol(0)),
                  pl.BlockSpec((CONV_WIDTH, LANES), wcol(1)),
                  pl.BlockSpec((CONV_WIDTH, LANES), wcol(2)),
                  pl.BlockSpec((1, LANES), const),
                  pl.BlockSpec((1, LANES), const)],
        out_specs=[row, row, row, row,
                   pl.BlockSpec((1, 1, s, GDN_CHUNK), lambda bi, h: (bi, h, 0, 0)),
                   pl.BlockSpec((1, 1, n_chunks, 1, LANES), lambda bi, h: (bi, h, 0, 0, 0))],
        out_shape=[jax.ShapeDtypeStruct((b, nh, s, GDN_DIM), F32),
                   jax.ShapeDtypeStruct((b, nh, s, GDN_DIM), BF16),
                   jax.ShapeDtypeStruct((b, nh, s, GDN_DIM), BF16),
                   jax.ShapeDtypeStruct((b, nh, s, GDN_DIM), BF16),
                   jax.ShapeDtypeStruct((b, nh, s, GDN_CHUNK), BF16),
                   jax.ShapeDtypeStruct((b, nh, n_chunks, 1, LANES), F32)],
        compiler_params=_params("parallel", "parallel"),
        name="gdn_prep",
    )(rest3, rest3, rest3, rest3, conv_w, conv_w, conv_w, alog_l, dtb_l)


def _gdn_scan_kernel(u0_ref, w_ref, qg_ref, kt_ref, at_ref, eg_ref, z_ref, nw_ref, o_ref, state_ref, *, ts):
    si = pl.program_id(1)

    @pl.when(si == 0)
    def _():
        state_ref[...] = jnp.zeros_like(state_ref)

    c_len = GDN_CHUNK
    per_tile = ts // c_len
    nw = nw_ref[...]
    for c in range(per_tile):
        rows = slice(c * c_len, (c + 1) * c_len)
        heads = range(GDN_HEADS)
        st = [state_ref[h] for h in heads]
        r = [_mm(jnp.concatenate([w_ref[0, h, rows, :], qg_ref[0, h, rows, :]], axis=0), st[h]) for h in heads]
        u = [(u0_ref[0, h, rows, :] - r[h][:c_len]).astype(BF16) for h in heads]
        ku = [_mm_tn(kt_ref[0, h, rows, :], u[h]) for h in heads]
        au = [jnp.dot(at_ref[0, h, rows, :], u[h], preferred_element_type=F32) for h in heads]
        for h in heads:
            state_ref[h] = st[h] * eg_ref[0, h, si * per_tile + c] + ku[h]
        for h in heads:
            cols = slice(h * GDN_DIM, (h + 1) * GDN_DIM)
            y = _rms(r[h][c_len:] + au[h], nw) * _silu(z_ref[0, rows, cols])
            o_ref[0, rows, cols] = y.astype(BF16)


def _gdn_scan(u0, w, qg, kt, at, eg, rest3, norm_w, ts=512):
    b, nh, s, _ = u0.shape
    n_chunks = s // GDN_CHUNK
    blk = lambda d: pl.BlockSpec((1, nh, ts, d), lambda bi, i: (bi, 0, i, 0))
    kern = functools.partial(_gdn_scan_kernel, ts=ts)
    return pl.pallas_call(
        kern,
        grid=(b, s // ts),
        in_specs=[blk(GDN_DIM), blk(GDN_DIM), blk(GDN_DIM), blk(GDN_DIM), blk(GDN_CHUNK),
                  pl.BlockSpec((1, nh, n_chunks, 1, LANES), lambda bi, i: (bi, 0, 0, 0, 0)),
                  pl.BlockSpec((1, ts, GDN_WIDTH), lambda bi, i: (bi, i, REST_GZ // GDN_WIDTH)),
                  pl.BlockSpec((1, GDN_DIM), lambda bi, i: (0, 0))],
        out_specs=pl.BlockSpec((1, ts, GDN_WIDTH), lambda bi, i: (bi, i, 0)),
        out_shape=jax.ShapeDtypeStruct((b, s, GDN_WIDTH), BF16),
        scratch_shapes=[pltpu.VMEM((nh, GDN_DIM, GDN_DIM), F32)],
        compiler_params=_params("parallel", "arbitrary"),
        name="gdn_scan",
    )(u0, w, qg, kt, at, eg, rest3, norm_w)


def _outproj_kernel(x_ref, of_ref, or_ref, og_ref, w_ref, o_ref):
    acc = x_ref[...]
    acc = acc + jnp.dot(of_ref[...], w_ref[0:FOX_WIDTH, :], preferred_element_type=F32)
    acc = acc + jnp.dot(or_ref[...], w_ref[FOX_WIDTH:FOX_WIDTH + RET_WIDTH, :], preferred_element_type=F32)
    acc = acc + jnp.dot(og_ref[...], w_ref[FOX_WIDTH + RET_WIDTH:, :], preferred_element_type=F32)
    o_ref[...] = acc


def _outproj(x, o_fox, o_ret, o_gdn, w_out, tm=512):
    t = x.shape[0]
    d_mix = w_out.shape[0]
    return pl.pallas_call(
        _outproj_kernel,
        grid=(t // tm,),
        in_specs=[pl.BlockSpec((tm, D_MODEL), lambda i: (i, 0)),
                  pl.BlockSpec((tm, FOX_WIDTH), lambda i: (i, 0)),
                  pl.BlockSpec((tm, RET_WIDTH), lambda i: (i, 0)),
                  pl.BlockSpec((tm, GDN_WIDTH), lambda i: (i, 0)),
                  pl.BlockSpec((d_mix, D_MODEL), lambda i: (0, 0))],
        out_specs=pl.BlockSpec((tm, D_MODEL), lambda i: (i, 0)),
        out_shape=jax.ShapeDtypeStruct((t, D_MODEL), F32),
        compiler_params=_params("parallel"),
        name="outproj",
    )(x, o_fox, o_ret, o_gdn, w_out)


def _router_kernel(x_ref, nw_ref, w_ref, b_ref, g_ref):
    hn = _rms(x_ref[...], nw_ref[...])
    logits = jnp.dot(hn, w_ref[...], precision=HIGHEST, preferred_element_type=F32) + b_ref[...]
    lane = lax.broadcasted_iota(jnp.int32, logits.shape, 1).astype(F32)
    neg = -jnp.inf
    gl = jnp.where(lane < ROUTER_EXP, logits, neg)
    gmax = jnp.max(gl, axis=-1, keepdims=True)
    gidx = jnp.min(jnp.where(gl == gmax, lane, LANES), axis=-1, keepdims=True)
    grp_p = 1.0 / jnp.sum(jnp.exp(gl - gmax), axis=-1, keepdims=True)
    lo = ROUTER_EXP + gidx * EXPERTS_PER_GROUP
    el = jnp.where((lane >= lo) & (lane < lo + EXPERTS_PER_GROUP), logits, neg)
    m1 = jnp.max(el, axis=-1, keepdims=True)
    i1 = jnp.min(jnp.where(el == m1, lane, LANES), axis=-1, keepdims=True)
    el2 = jnp.where(lane == i1, neg, el)
    m2 = jnp.max(el2, axis=-1, keepdims=True)
    i2 = jnp.min(jnp.where(el2 == m2, lane, LANES), axis=-1, keepdims=True)
    e2 = jnp.exp(m2 - m1)
    w1 = grp_p / (1.0 + e2)
    g_ref[...] = jnp.where(lane == i1, w1, 0.0) + jnp.where(lane == i2, w1 * e2, 0.0)


def _router(x, nw, w_pack, b_pack, tm=512):
    t = x.shape[0]
    return pl.pallas_call(
        _router_kernel,
        grid=(t // tm,),
        in_specs=[pl.BlockSpec((tm, D_MODEL), lambda i: (i, 0)),
                  pl.BlockSpec((1, D_MODEL), lambda i: (0, 0)),
                  pl.BlockSpec((D_MODEL, LANES), lambda i: (0, 0)),
                  pl.BlockSpec((1, LANES), lambda i: (0, 0))],
        out_specs=pl.BlockSpec((tm, LANES), lambda i: (i, 0)),
        out_shape=jax.ShapeDtypeStruct((t, LANES), F32),
        compiler_params=_params("parallel"),
        name="router",
    )(x, nw, w_pack, b_pack)


def _moe_kernel(x_ref, nw_ref, g_ref, w1_ref, w3_ref, w2_ref, fw_ref, o_ref, hn_ref, *, final_norm):
    e = pl.program_id(1)

    @pl.when(e == 0)
    def _():
        x = x_ref[...]
        hn_ref[...] = _rms(x, nw_ref[...]).astype(BF16)
        o_ref[...] = x

    hn = hn_ref[...]
    a = jnp.dot(hn, w1_ref[0], preferred_element_type=F32)
    u = jnp.dot(hn, w3_ref[0], preferred_element_type=F32)
    lane = lax.broadcasted_iota(jnp.int32, (1, LANES), 1)
    gate = jnp.sum(jnp.where(lane == ROUTER_EXP + e, g_ref[...], 0.0), axis=-1, keepdims=True)
    hid = _silu(a) * u * gate
    o_ref[...] += jnp.dot(hid.astype(BF16), w2_ref[0], preferred_element_type=F32)

    if final_norm:
        @pl.when(e == pl.num_programs(1) - 1)
        def _():
            o_ref[...] = _rms(o_ref[...], fw_ref[...])


def _moe(x, nw, gates, w1, w3, w2, fw, final_norm, tm=1024):
    t = x.shape[0]
    kern = functools.partial(_moe_kernel, final_norm=final_norm)
    return pl.pallas_call(
        kern,
        grid=(t // tm, N_EXPERTS),
        in_specs=[pl.BlockSpec((tm, D_MODEL), lambda i, e: (i, 0)),
                  pl.BlockSpec((1, D_MODEL), lambda i, e: (0, 0)),
                  pl.BlockSpec((tm, LANES), lambda i, e: (i, 0)),
                  pl.BlockSpec((1, D_MODEL, EXPERT_FF), lambda i, e: (e, 0, 0)),
                  pl.BlockSpec((1, D_MODEL, EXPERT_FF), lambda i, e: (e, 0, 0)),
                  pl.BlockSpec((1, EXPERT_FF, D_MODEL), lambda i, e: (e, 0, 0)),
                  pl.BlockSpec((1, D_MODEL), lambda i, e: (0, 0))],
        out_specs=pl.BlockSpec((tm, D_MODEL), lambda i, e: (i, 0)),
        out_shape=jax.ShapeDtypeStruct((t, D_MODEL), F32),
        scratch_shapes=[pltpu.VMEM((tm, D_MODEL), BF16)],
        compiler_params=_params("parallel", "arbitrary"),
        name="moe",
    )(x, nw, gates, w1, w3, w2, fw)


def _pack_in_weights(w_in_l):
    cuts = np.cumsum(IN_SPLITS)[:-1].tolist()
    fq, fk, fv, ff, rq, rk, rv, rg, gq, gk, gv, gz, ga, gb = jnp.split(w_in_l, cuts, axis=1)
    half = HEAD64 // 2
    perm = np.concatenate([(2 * p + hh) * HEAD64 + lo * half + np.arange(half)
                           for p in range(RET_HEADS // 2) for lo in range(2) for hh in range(2)])
    small = jnp.concatenate([ff, ga, gb, jnp.zeros((D_MODEL, LANES - 12), w_in_l.dtype)], axis=1)
    wf = jnp.concatenate([fq, fk, fv], axis=1).astype(BF16)
    wr = jnp.concatenate([rq[:, perm], rk[:, perm], rv, rg, gq, gk, gv, gz, small], axis=1).astype(BF16)
    return wf, wr


def _lane_row(vals, offset):
    return jnp.zeros((1, LANES), F32).at[0, offset:offset + vals.shape[0]].set(vals.astype(F32))


def kernel(x, norm1_w, w_in, fox_forget_bias, gdn_conv_w, gdn_a_log, gdn_dt_bias, gdn_norm_w, w_out, norm2_w,
           router_group_w, router_group_b, router_expert_w, router_expert_b, expert_w1, expert_w3, expert_w2,
           final_norm_w):
    b, s, d = x.shape
    t = b * s
    depth = w_in.shape[0]
    fox_tk = 512
    xt = x.reshape(t, d)
    ret_tables = _ret_tables(s)
    for l in range(depth):
        wf, wr = _pack_in_weights(w_in[l])
        qkv, rest = _inproj(xt, norm1_w[l].reshape(1, d), wf, wr)
        rest3 = rest.reshape(b, s, REST_WIDTH)
        ff = rest3[:, :, REST_SMALL + SMALL_FF:REST_SMALL + SMALL_FF + FOX_HEADS]
        ff4 = ff.transpose(0, 2, 1).reshape(b, FOX_HEADS, s // LANES, LANES)
        bias_rows = jnp.repeat(fox_forget_bias[l].astype(F32), s // LANES).reshape(-1, 1)
        c = _fgate(ff4, bias_rows).reshape(b, FOX_HEADS, s // fox_tk, 1, fox_tk)
        o_fox = _fox(qkv.reshape(b, s, 3 * FOX_WIDTH), c, tk=fox_tk)
        o_ret = _ret(rest3, ret_tables)
        prep = _gdn_prep(rest3, gdn_conv_w[l].astype(F32), _lane_row(gdn_a_log[l], SMALL_GA),
                         _lane_row(gdn_dt_bias[l], SMALL_GA))
        o_gdn = _gdn_scan(*prep, rest3, gdn_norm_w[l].reshape(1, GDN_DIM).astype(F32))
        xt = _outproj(xt, o_fox.reshape(t, FOX_WIDTH), o_ret.reshape(t, RET_WIDTH),
                      o_gdn.reshape(t, GDN_WIDTH), w_out[l].astype(BF16))
        w_pack = jnp.concatenate([router_group_w[l], router_expert_w[l],
                                  jnp.zeros((d, LANES - N_GROUPS - N_EXPERTS), F32)], axis=1)
        b_pack = jnp.concatenate([router_group_b[l].reshape(-1), router_expert_b[l].reshape(-1),
                                  jnp.zeros((LANES - N_GROUPS - N_EXPERTS,), F32)]).reshape(1, LANES)
        gates = _router(xt, norm2_w[l].reshape(1, d), w_pack, b_pack)
        xt = _moe(xt, norm2_w[l].reshape(1, d), gates,
                  expert_w1[l].reshape(N_EXPERTS, d, EXPERT_FF).astype(BF16),
                  expert_w3[l].reshape(N_EXPERTS, d, EXPERT_FF).astype(BF16),
                  expert_w2[l].reshape(N_EXPERTS, EXPERT_FF, d).astype(BF16),
                  final_norm_w.reshape(1, d), final_norm=(l == depth - 1))
    return xt.reshape(b, s, d)
```

```python
import functools
import math

import jax
import jax.numpy as jnp
import numpy as np
from jax import lax
from jax.experimental import pallas as pl
from jax.experimental.pallas import tpu as pltpu

F32 = jnp.float32
BF16 = jnp.bfloat16
HIGHEST = lax.Precision.HIGHEST

D_MODEL = 1024
FOX_HEADS = 4
RET_HEADS = 4
GDN_HEADS = 4
HEAD64 = 64
GDN_DIM = 128
FOX_WIDTH = FOX_HEADS * HEAD64
RET_WIDTH = RET_HEADS * HEAD64
GDN_WIDTH = GDN_HEADS * GDN_DIM
RET_CHUNK = 128
GDN_CHUNK = 64
GDN_PREP_GROUP = 8
CONV_WIDTH = 4
RET_ANGLE_BASE = 10000.0
N_GROUPS = 4
EXPERTS_PER_GROUP = 8
N_EXPERTS = N_GROUPS * EXPERTS_PER_GROUP
EXPERT_FF = 256
NORM_EPS = 1e-6
LOG2E = math.log2(math.e)
LANES = 128
IN_SPLITS = (FOX_WIDTH, FOX_WIDTH, FOX_WIDTH, FOX_HEADS,
             RET_WIDTH, RET_WIDTH, RET_WIDTH, RET_WIDTH,
             GDN_WIDTH, GDN_WIDTH, GDN_WIDTH, GDN_WIDTH, GDN_HEADS, GDN_HEADS)

REST_RQ, REST_RK, REST_RV, REST_RG = 0, 256, 512, 768
REST_GQ, REST_GK, REST_GV, REST_GZ = 1024, 1536, 2048, 2560
REST_SMALL = 3072
REST_WIDTH = 3200
SMALL_FF, SMALL_GA, SMALL_GB = 0, 4, 8
ROUTER_GRP, ROUTER_EXP = 0, 4
ROUTE_EXP, ROUTE_GATE, ROUTE_RANK = 0, 2, 4
TOP_K = 2
MOE_TILE = 256

VMEM_LIMIT = 56 * 1024 * 1024


def _params(*sem):
    return pltpu.CompilerParams(dimension_semantics=sem, vmem_limit_bytes=VMEM_LIMIT)


def _mm(a, b):
    return jnp.dot(a.astype(BF16), b.astype(BF16), preferred_element_type=F32)


def _mm_nt(a, b):
    return lax.dot_general(a.astype(BF16), b.astype(BF16), (((1,), (1,)), ((), ())),
                           preferred_element_type=F32)


def _mm_tn(a, b):
    return lax.dot_general(a.astype(BF16), b.astype(BF16), (((0,), (0,)), ((), ())),
                           preferred_element_type=F32)


def _silu(x):
    return x * (1.0 / (1.0 + jnp.exp(-x)))


def _rms(x, w):
    return x * lax.rsqrt(jnp.mean(x * x, axis=-1, keepdims=True) + NORM_EPS) * w


def _inproj_kernel(x_ref, nw_ref, wf_ref, wr_ref, of_ref, or_ref):
    hn = _rms(x_ref[...], nw_ref[...]).astype(BF16)
    of_ref[...] = jnp.dot(hn, wf_ref[...], preferred_element_type=F32).astype(BF16)
    step = 640
    for c in range(0, REST_WIDTH, step):
        or_ref[:, c:c + step] = jnp.dot(hn, wr_ref[:, c:c + step], preferred_element_type=F32)


def _inproj(x, nw, wf, wr, tm=512):
    t = x.shape[0]
    return pl.pallas_call(
        _inproj_kernel,
        grid=(t // tm,),
        in_specs=[pl.BlockSpec((tm, D_MODEL), lambda i: (i, 0)),
                  pl.BlockSpec((1, D_MODEL), lambda i: (0, 0)),
                  pl.BlockSpec((D_MODEL, 3 * FOX_WIDTH), lambda i: (0, 0)),
                  pl.BlockSpec((D_MODEL, REST_WIDTH), lambda i: (0, 0))],
        out_specs=[pl.BlockSpec((tm, 3 * FOX_WIDTH), lambda i: (i, 0)),
                   pl.BlockSpec((tm, REST_WIDTH), lambda i: (i, 0))],
        out_shape=[jax.ShapeDtypeStruct((t, 3 * FOX_WIDTH), BF16),
                   jax.ShapeDtypeStruct((t, REST_WIDTH), F32)],
        compiler_params=_params("parallel"),
        name="inproj",
    )(x, nw, wf, wr)


def _fgate_kernel(sm_ref, bias_ref, sel_ref, tri_ref, c_ref):
    n_blk = sm_ref.shape[1] // LANES
    z = sm_ref[0] + bias_ref[...]
    lf = jnp.minimum(z, 0.0) - jnp.log1p(jnp.exp(-jnp.abs(z)))
    sel = sel_ref[...]
    tri = tri_ref[...]
    within = []
    for j in range(n_blk):
        blk = lf[j * LANES:(j + 1) * LANES, :]
        x = lax.dot_general(sel, blk, (((1,), (1,)), ((), ())), precision=HIGHEST, preferred_element_type=F32)
        within.append(jnp.dot(x, tri, precision=HIGHEST, preferred_element_type=F32))
    carry = jnp.zeros((sel.shape[0], 1), F32)
    for j in range(n_blk):
        cj = within[j] + carry
        for h in range(FOX_HEADS):
            c_ref[0, h, :, j * LANES:(j + 1) * LANES] = cj[h:h + 1, :]
        carry = cj[:, LANES - 1:LANES]


def _fgate(rest3, bias_row):
    b, s, _ = rest3.shape
    tri = (np.arange(LANES)[:, None] <= np.arange(LANES)[None, :]).astype(np.float32)
    sel = (np.arange(8)[:, None] == np.arange(LANES)[None, :]).astype(np.float32)
    sel[FOX_HEADS:] = 0.0
    return pl.pallas_call(
        _fgate_kernel,
        grid=(b,),
        in_specs=[pl.BlockSpec((1, s, LANES), lambda i: (i, 0, REST_SMALL // LANES)),
                  pl.BlockSpec((1, LANES), lambda i: (0, 0)),
                  pl.BlockSpec((8, LANES), lambda i: (0, 0)),
                  pl.BlockSpec((LANES, LANES), lambda i: (0, 0))],
        out_specs=pl.BlockSpec((1, FOX_HEADS, 1, s), lambda i: (i, 0, 0, 0)),
        out_shape=jax.ShapeDtypeStruct((b, FOX_HEADS, 1, s), F32),
        compiler_params=_params("parallel"),
        name="fgate",
    )(rest3, bias_row, jnp.asarray(sel), jnp.asarray(tri))


def _fox_kernel(q_ref, k_ref, v_ref, c_ref, o_ref, sa_ref, sb_ref, *, tq, tk):
    i = pl.program_id(2)
    lane = lax.broadcasted_iota(jnp.int32, (1, LANES), 1)
    first = lane < HEAD64
    q = q_ref[0]
    zero = jnp.zeros_like(q)
    qh = (jnp.where(first, q, zero), jnp.where(first, zero, q))
    nfull = (i * tq) // tk
    cbase = [c_ref[0, hh, nfull][:, 0:1] for hh in range(2)]
    qpos = i * tq + lax.broadcasted_iota(jnp.int32, (tq, 1), 0)
    den = (HEAD64, 0)
    lane_v = lax.broadcasted_iota(jnp.int32, (tk, LANES), 1)
    keep = (lane_v < HEAD64, lane_v >= HEAD64)
    ones_col = tuple(jnp.where(lane_v == d, 1.0, 0.0).astype(BF16) for d in den)

    def scores(j, s_ref):
        k0 = pl.multiple_of(j * tk, tk)
        k = k_ref[0, pl.ds(k0, tk), :]
        for hh in range(2):
            s = lax.dot_general(qh[hh], k, (((1,), (1,)), ((), ())), preferred_element_type=F32)
            s_ref[hh] = s + (cbase[hh] - c_ref[0, hh, j]) * LOG2E

    def update(j, s_ref, carry, masked):
        k0 = pl.multiple_of(j * tk, tk)
        v = v_ref[0, pl.ds(k0, tk), :]
        vh = tuple(jnp.where(keep[hh], v, ones_col[hh]) for hh in range(2))
        out = []
        for hh in range(2):
            m, acc = carry[hh]
            s = s_ref[hh]
            if masked:
                kpos = j * tk + lax.broadcasted_iota(jnp.int32, (1, tk), 1)
                s = jnp.where(kpos <= qpos, s, -jnp.inf)
            m_new = jnp.maximum(m, jnp.max(s, axis=-1, keepdims=True))
            alpha = jnp.exp2(m - m_new)
            p = jnp.exp2(s - m_new)
            acc = alpha * acc + jnp.dot(p.astype(BF16), vh[hh], preferred_element_type=F32)
            out.append((m_new, acc))
        return tuple(out)

    def pair(jj, carry):
        j = 2 * jj
        scores(j + 1, sb_ref)
        carry = update(j, sa_ref, carry, False)
        scores(j + 2, sa_ref)
        return update(j + 1, sb_ref, carry, False)

    def tail_even(carry):
        return update(nfull, sa_ref, carry, True)

    def tail_odd(carry):
        scores(nfull, sb_ref)
        carry = update(nfull - 1, sa_ref, carry, False)
        return update(nfull, sb_ref, carry, True)

    init = tuple((jnp.full((tq, 1), -jnp.inf, F32), jnp.zeros((tq, LANES), F32)) for _ in range(2))
    scores(0, sa_ref)
    carry = lax.fori_loop(0, nfull // 2, pair, init)
    carry = lax.cond(nfull % 2 == 1, tail_odd, tail_even, carry)
    acc0, acc1 = carry[0][1], carry[1][1]
    o0 = acc0 / acc0[:, den[0]:den[0] + 1]
    o1 = acc1 / acc1[:, den[1]:den[1] + 1]
    o_ref[0] = jnp.where(first, o0, o1).astype(BF16)


def _fox(qkv, c, tq=256, tk=512):
    b, s, _ = qkv.shape
    npair = FOX_HEADS // 2
    kern = functools.partial(_fox_kernel, tq=tq, tk=tk)
    return pl.pallas_call(
        kern,
        grid=(b, npair, s // tq),
        in_specs=[pl.BlockSpec((1, tq, LANES), lambda bi, p, i: (bi, i, p)),
                  pl.BlockSpec((1, s, LANES), lambda bi, p, i: (bi, 0, npair + p)),
                  pl.BlockSpec((1, s, LANES), lambda bi, p, i: (bi, 0, 2 * npair + p)),
                  pl.BlockSpec((1, 2, s // tk, 1, tk), lambda bi, p, i: (bi, p, 0, 0, 0))],
        out_specs=pl.BlockSpec((1, tq, LANES), lambda bi, p, i: (bi, i, p)),
        out_shape=jax.ShapeDtypeStruct((b, s, FOX_WIDTH), BF16),
        scratch_shapes=[pltpu.VMEM((2, tq, tk), F32), pltpu.VMEM((2, tq, tk), F32)],
        compiler_params=_params("parallel", "parallel", "arbitrary"),
        name="fox",
    )(qkv, qkv, qkv, c)


def _ret_kernel(q_ref, k_ref, v_ref, g_ref, cos_ref, sin_ref, dmat_ref, qdec_ref, kdec_ref, cd_ref, bm_ref,
                o_ref, state_ref, *, ts):
    @pl.when(pl.program_id(2) == 0)
    def _():
        state_ref[...] = jnp.zeros_like(state_ref)

    lane = lax.broadcasted_iota(jnp.int32, (1, LANES), 1)
    q_first = (lane % HEAD64) < (HEAD64 // 2)
    v_first = lane < HEAD64
    c_len = RET_CHUNK
    for c in range(ts // c_len):
        rows = slice(c * c_len, (c + 1) * c_len)
        cos = cos_ref[rows, :]
        sin = sin_ref[rows, :]
        q = q_ref[0, rows, :]
        k = k_ref[0, rows, :]
        v = v_ref[0, rows, :]
        qr = q * cos + pltpu.roll(q, LANES // 2, 1) * sin
        kr = k * cos + pltpu.roll(k, LANES // 2, 1) * sin
        q2 = jnp.concatenate([jnp.where(q_first, qr, 0.0), jnp.where(q_first, 0.0, qr)], axis=0)
        s = _mm_nt(q2, kr * (HEAD64 ** -0.5))
        s0 = s[:c_len] * dmat_ref[0]
        s1 = s[c_len:] * dmat_ref[1]
        o = _mm(s0, jnp.where(v_first, v, 0.0)) + _mm(s1, jnp.where(v_first, 0.0, v))
        state = state_ref[...]
        o = o + _mm(qr * qdec_ref[0], state)
        kv = _mm_tn(kr * kdec_ref[0], v)
        state_ref[...] = state * cd_ref[0] + kv * bm_ref[0]
        sq = o * o
        ms0 = jnp.sum(jnp.where(v_first, sq, 0.0), axis=-1, keepdims=True)
        ms1 = jnp.sum(jnp.where(v_first, 0.0, sq), axis=-1, keepdims=True)
        ms = jnp.where(v_first, ms0, ms1) * (1.0 / HEAD64)
        y = o * lax.rsqrt(ms + NORM_EPS) * _silu(g_ref[0, rows, :])
        o_ref[0, rows, :] = y.astype(BF16)


def _ret_tables(s):
    npair = RET_HEADS // 2
    half = HEAD64 // 2
    lane = np.arange(LANES)
    log_g = np.log1p(-np.exp2(-5.0 - np.arange(RET_HEADS, dtype=np.float32))).astype(np.float32)
    idx = np.arange(RET_CHUNK, dtype=np.float32)
    rel = idx[:, None] - idx[None, :]
    dmat = np.where(rel[None] >= 0, np.exp(np.maximum(rel, 0.0)[None] * log_g[:, None, None]), 0.0)
    qdec, kdec, cd, bm = [], [], [], []
    for p in range(npair):
        hq = 2 * p + ((lane % HEAD64) >= half)
        hv = 2 * p + (lane >= HEAD64)
        qdec.append(np.exp((idx[:, None] + 1.0) * log_g[hq][None, :]))
        kdec.append(np.exp((RET_CHUNK - 1 - idx)[:, None] * log_g[hq][None, :]) * HEAD64 ** -0.5)
        cd.append(np.broadcast_to(np.exp(RET_CHUNK * log_g[hq])[:, None], (LANES, LANES)))
        bm.append((hq[:, None] == hv[None, :]).astype(np.float32))
    tabs = [np.stack(a).astype(np.float32) for a in (qdec, kdec, cd, bm)]
    inv = 1.0 / (RET_ANGLE_BASE ** jnp.linspace(0.0, 1.0, half, dtype=F32))
    pos = jnp.arange(s, dtype=F32)
    ang = pos[:, None] * inv[None, :]
    cos = jnp.tile(jnp.cos(ang), (1, LANES // half))
    sin = jnp.tile(jnp.sin(ang), (1, LANES // half))
    sin = jnp.where(jnp.asarray(lane)[None, :] < LANES // 2, -sin, sin)
    return [jnp.asarray(dmat.astype(np.float32))] + [jnp.asarray(a) for a in tabs] + [cos, sin]


def _ret(rest3, tables, ts=1024):
    b, s, _ = rest3.shape
    dmat, qdec, kdec, cd, bm, cos, sin = tables
    npair = RET_HEADS // 2
    col = lambda off: (lambda bi, p, i: (bi, i, off // LANES + p))
    tab = lambda bi, p, i: (p, 0, 0)
    kern = functools.partial(_ret_kernel, ts=ts)
    return pl.pallas_call(
        kern,
        grid=(b, npair, s // ts),
        in_specs=[pl.BlockSpec((1, ts, LANES), col(REST_RQ)),
                  pl.BlockSpec((1, ts, LANES), col(REST_RK)),
                  pl.BlockSpec((1, ts, LANES), col(REST_RV)),
                  pl.BlockSpec((1, ts, LANES), col(REST_RG)),
                  pl.BlockSpec((ts, LANES), lambda bi, p, i: (i, 0)),
                  pl.BlockSpec((ts, LANES), lambda bi, p, i: (i, 0)),
                  pl.BlockSpec((2, RET_CHUNK, RET_CHUNK), tab),
                  pl.BlockSpec((1, RET_CHUNK, LANES), tab),
                  pl.BlockSpec((1, RET_CHUNK, LANES), tab),
                  pl.BlockSpec((1, LANES, LANES), tab),
                  pl.BlockSpec((1, LANES, LANES), tab)],
        out_specs=pl.BlockSpec((1, ts, LANES), lambda bi, p, i: (bi, i, p)),
        out_shape=jax.ShapeDtypeStruct((b, s, RET_WIDTH), BF16),
        scratch_shapes=[pltpu.VMEM((LANES, LANES), F32)],
        compiler_params=_params("parallel", "parallel", "arbitrary"),
        name="retention",
    )(rest3, rest3, rest3, rest3, cos, sin, dmat, qdec, kdec, cd, bm)


def _gdn_prep_kernel(q_ref, k_ref, v_ref, sm_ref, wq_ref, wk_ref, wv_ref, alog_ref, dtb_ref,
                     u0_ref, w_ref, qg_ref, kt_ref, at_ref, eg_ref):
    h = pl.program_id(1)
    c_len = GDN_CHUNK
    n_chunks = q_ref.shape[1] // c_len
    lane = lax.broadcasted_iota(jnp.int32, (1, LANES), 1)
    ri = lax.broadcasted_iota(jnp.int32, (c_len, c_len), 0)
    ci = lax.broadcasted_iota(jnp.int32, (c_len, c_len), 1)
    incl = ri >= ci
    strict = ri > ci
    eye = (ri == ci).astype(F32)
    neg_a = -jnp.exp(alog_ref[...])
    dtb = dtb_ref[...]

    grp = GDN_PREP_GROUP
    rows = grp * c_len

    def conv_silu(ref, w_ref_, n, r0):
        cur = ref[0, pl.ds(r0, rows), :]
        p0 = pl.multiple_of(jnp.maximum(r0 - 8, 0), 8)
        prev = ref[0, pl.ds(p0, 8), :]
        prev = jnp.where(jnp.broadcast_to(n > 0, prev.shape), prev, 0.0)
        xc = jnp.concatenate([prev, cur], axis=0)
        w = w_ref_[...]
        y = cur * w[CONV_WIDTH - 1:CONV_WIDTH, :]
        for j in range(CONV_WIDTH - 1):
            shifted = pltpu.roll(xc, CONV_WIDTH - 1 - j, 0)[8:, :]
            y = y + shifted * w[j:j + 1, :]
        return _silu(y)

    def group(n, carry):
        r0 = pl.multiple_of(n * rows, rows)
        cq = conv_silu(q_ref, wq_ref, n, r0)
        ck = conv_silu(k_ref, wk_ref, n, r0)
        cv = conv_silu(v_ref, wv_ref, n, r0)
        qn = cq * lax.rsqrt(jnp.sum(cq * cq, axis=-1, keepdims=True) + NORM_EPS) * (GDN_DIM ** -0.5)
        kn = ck * lax.rsqrt(jnp.sum(ck * ck, axis=-1, keepdims=True) + NORM_EPS)
        sm = sm_ref[0, pl.ds(r0, rows), :]
        z = sm + dtb
        g_all = neg_a * (jnp.maximum(z, 0.0) + jnp.log1p(jnp.exp(-jnp.abs(z))))
        beta_all = 1.0 / (1.0 + jnp.exp(-sm))
        g_col = jnp.sum(jnp.where(lane == SMALL_GA + h, g_all, 0.0), axis=-1, keepdims=True)
        beta = jnp.sum(jnp.where(lane == SMALL_GB + h, beta_all, 0.0), axis=-1, keepdims=True)
        kb = kn * beta
        vb = cv * beta
        chunks = [slice(g * c_len, (g + 1) * c_len) for g in range(grp)]
        g_row = [jnp.sum(g_col[c] * eye, axis=0, keepdims=True) for c in chunks]
        gc = [jnp.sum(jnp.where(incl, g_row[i], 0.0), axis=-1, keepdims=True) for i in range(grp)]
        gc_row = [jnp.sum(jnp.where(ri <= ci, g_col[c], 0.0), axis=0, keepdims=True) for c in chunks]
        decay = [jnp.where(incl, jnp.exp(jnp.where(incl, gc[i] - gc_row[i], 0.0)), 0.0) for i in range(grp)]
        both = [_mm_nt(jnp.concatenate([kb[c], qn[c]], axis=0), kn[c]) for c in chunks]
        low = [jnp.where(strict, both[i][:c_len] * decay[i], 0.0) for i in range(grp)]
        attn = [jnp.where(incl, both[i][c_len:] * decay[i], 0.0) for i in range(grp)]
        inv = [eye - low[i] for i in range(grp)]
        pw = low
        for _ in range(int(math.log2(c_len)) - 1):
            pw = [_mm(pw[i], pw[i]) for i in range(grp)]
            inv = [inv[i] + _mm(inv[i], pw[i]) for i in range(grp)]
        eg = [jnp.exp(gc[i]) for i in range(grp)]
        sol = [_mm(inv[i], jnp.concatenate([vb[c], kb[c] * eg[i]], axis=1)) for i, c in enumerate(chunks)]
        for i, c in enumerate(chunks):
            dst = pl.ds(pl.multiple_of(r0 + i * c_len, c_len), c_len)
            g_last = gc[i][c_len - 1:c_len, :]
            u0_ref[0, 0, dst, :] = sol[i][:, :GDN_DIM]
            w_ref[0, 0, dst, :] = sol[i][:, GDN_DIM:].astype(BF16)
            qg_ref[0, 0, dst, :] = (qn[c] * eg[i]).astype(BF16)
            kt_ref[0, 0, dst, :] = (kn[c] * jnp.exp(g_last - gc[i])).astype(BF16)
            at_ref[0, 0, dst, :] = attn[i].astype(BF16)
            eg_ref[0, 0, n * grp + i] = jnp.broadcast_to(jnp.exp(g_last), (1, LANES))
        return carry

    lax.fori_loop(0, n_chunks // grp, group, 0)


def _gdn_prep(rest3, conv_w, alog_l, dtb_l):
    b, s, _ = rest3.shape
    nh = GDN_HEADS
    n_chunks = s // GDN_CHUNK
    col = lambda off: (lambda bi, h: (bi, 0, off // LANES + h))
    wcol = lambda g: (lambda bi, h: (0, g * nh + h))
    const = lambda bi, h: (0, 0)
    row =pl.BlockSpec((1, 1, s, GDN_DIM), lambda bi, h: (bi, h, 0, 0))
    return pl.pallas_call(
        _gdn_prep_kernel,
        grid=(b, nh),
        in_specs=[pl.BlockSpec((1, s, LANES), col(REST_GQ)),
                  pl.BlockSpec((1, s, LANES), col(REST_GK)),
                  pl.BlockSpec((1, s, LANES), col(REST_GV)),
                  pl.BlockSpec((1, s, LANES), lambda bi, h: (bi, 0, REST_SMALL // LANES)),
                  pl.BlockSpec((CONV_WIDTH, LANES), wcol(0)),
                  pl.BlockSpec((CONV_WIDTH, LANES), wcol(1)),
                  pl.BlockSpec((CONV_WIDTH, LANES), wcol(2)),
                  pl.BlockSpec((1, LANES), const),
                  pl.BlockSpec((1, LANES), const)],
        out_specs=[row, row, row, row,
                   pl.BlockSpec((1, 1, s, GDN_CHUNK), lambda bi, h: (bi, h, 0, 0)),
                   pl.BlockSpec((1, 1, n_chunks, 1, LANES), lambda bi, h: (bi, h, 0, 0, 0))],
        out_shape=[jax.ShapeDtypeStruct((b, nh, s, GDN_DIM), F32),
                   jax.ShapeDtypeStruct((b, nh, s, GDN_DIM), BF16),
                   jax.ShapeDtypeStruct((b, nh, s, GDN_DIM), BF16),
                   jax.ShapeDtypeStruct((b, nh, s, GDN_DIM), BF16),
                   jax.ShapeDtypeStruct((b, nh, s, GDN_CHUNK), BF16),
                   jax.ShapeDtypeStruct((b, nh, n_chunks, 1, LANES), F32)],
        compiler_params=_params("parallel", "parallel"),
        name="gdn_prep",
    )(rest3, rest3, rest3, rest3, conv_w, conv_w, conv_w, alog_l, dtb_l)


def _gdn_scan_kernel(u0_ref, w_ref, qg_ref, kt_ref, at_ref, eg_ref, z_ref, nw_ref, o_ref, state_ref, *, ts):
    si = pl.program_id(1)

    @pl.when(si == 0)
    def _():
        state_ref[...] = jnp.zeros_like(state_ref)

    c_len = GDN_CHUNK
    per_tile = ts // c_len
    nw = nw_ref[...]
    for c in range(per_tile):
        rows = slice(c * c_len, (c + 1) * c_len)
        heads = range(GDN_HEADS)
        st = [state_ref[h] for h in heads]
        r = [_mm(jnp.concatenate([w_ref[0, h, rows, :], qg_ref[0, h, rows, :]], axis=0), st[h]) for h in heads]
        u = [(u0_ref[0, h, rows, :] - r[h][:c_len]).astype(BF16) for h in heads]
        ku = [_mm_tn(kt_ref[0, h, rows, :], u[h]) for h in heads]
        au = [jnp.dot(at_ref[0, h, rows, :], u[h], preferred_element_type=F32) for h in heads]
        for h in heads:
            state_ref[h] = st[h] * eg_ref[0, h, si * per_tile + c] + ku[h]
        for h in heads:
            cols = slice(h * GDN_DIM, (h + 1) * GDN_DIM)
            y = _rms(r[h][c_len:] + au[h], nw) * _silu(z_ref[0, rows, cols])
            o_ref[0, rows, cols] = y.astype(BF16)


def _gdn_scan(u0, w, qg, kt, at, eg, rest3, norm_w, ts=512):
    b, nh, s, _ = u0.shape
    n_chunks = s // GDN_CHUNK
    blk = lambda d: pl.BlockSpec((1, nh, ts, d), lambda bi, i: (bi, 0, i, 0))
    kern = functools.partial(_gdn_scan_kernel, ts=ts)
    return pl.pallas_call(
        kern,
        grid=(b, s // ts),
        in_specs=[blk(GDN_DIM), blk(GDN_DIM), blk(GDN_DIM), blk(GDN_DIM), blk(GDN_CHUNK),
                  pl.BlockSpec((1, nh, n_chunks, 1, LANES), lambda bi, i: (bi, 0, 0, 0, 0)),
                  pl.BlockSpec((1, ts, GDN_WIDTH), lambda bi, i: (bi, i, REST_GZ // GDN_WIDTH)),
                  pl.BlockSpec((1, GDN_DIM), lambda bi, i: (0, 0))],
        out_specs=pl.BlockSpec((1, ts, GDN_WIDTH), lambda bi, i: (bi, i, 0)),
        out_shape=jax.ShapeDtypeStruct((b, s, GDN_WIDTH), BF16),
        scratch_shapes=[pltpu.VMEM((nh, GDN_DIM, GDN_DIM), F32)],
        compiler_params=_params("parallel", "arbitrary"),
        name="gdn_scan",
    )(u0, w, qg, kt, at, eg, rest3, norm_w)


def _outproj_kernel(x_ref, of_ref, or_ref, og_ref, w_ref, o_ref):
    acc = x_ref[...]
    acc = acc + jnp.dot(of_ref[...], w_ref[0:FOX_WIDTH, :], preferred_element_type=F32)
    acc = acc + jnp.dot(or_ref[...], w_ref[FOX_WIDTH:FOX_WIDTH + RET_WIDTH, :], preferred_element_type=F32)
    acc = acc + jnp.dot(og_ref[...], w_ref[FOX_WIDTH + RET_WIDTH:, :], preferred_element_type=F32)
    o_ref[...] = acc


def _outproj(x, o_fox, o_ret, o_gdn, w_out, tm=512):
    t = x.shape[0]
    d_mix = w_out.shape[0]
    return pl.pallas_call(
        _outproj_kernel,
        grid=(t // tm,),
        in_specs=[pl.BlockSpec((tm, D_MODEL), lambda i: (i, 0)),
                  pl.BlockSpec((tm, FOX_WIDTH), lambda i: (i, 0)),
                  pl.BlockSpec((tm, RET_WIDTH), lambda i: (i, 0)),
                  pl.BlockSpec((tm, GDN_WIDTH), lambda i: (i, 0)),
                  pl.BlockSpec((d_mix, D_MODEL), lambda i: (0, 0))],
        out_specs=pl.BlockSpec((tm, D_MODEL), lambda i: (i, 0)),
        out_shape=jax.ShapeDtypeStruct((t, D_MODEL), F32),
        compiler_params=_params("parallel"),
        name="outproj",
    )(x, o_fox, o_ret, o_gdn, w_out)


def _router_kernel(x_ref, nw_ref, w_ref, b_ref, tri_ref, sel_ref, route_ref, idx_ref, cnt_ref, carry_ref):
    @pl.when(pl.program_id(0) == 0)
    def _():
        carry_ref[...] = jnp.zeros_like(carry_ref)

    tm = x_ref.shape[0]
    hn = _rms(x_ref[...], nw_ref[...])
    logits = jnp.dot(hn, w_ref[...], precision=HIGHEST, preferred_element_type=F32) + b_ref[...]
    lane = lax.broadcasted_iota(jnp.int32, logits.shape, 1).astype(F32)
    neg = -jnp.inf
    gl = jnp.where(lane < ROUTER_EXP, logits, neg)
    gmax = jnp.max(gl, axis=-1, keepdims=True)
    gidx = jnp.min(jnp.where(gl == gmax, lane, LANES), axis=-1, keepdims=True)
    grp_p = 1.0 / jnp.sum(jnp.exp(gl - gmax), axis=-1, keepdims=True)
    lo = ROUTER_EXP + gidx * EXPERTS_PER_GROUP
    el = jnp.where((lane >= lo) & (lane < lo + EXPERTS_PER_GROUP), logits, neg)
    m1 = jnp.max(el, axis=-1, keepdims=True)
    i1 = jnp.min(jnp.where(el == m1, lane, LANES), axis=-1, keepdims=True)
    el2 = jnp.where(lane == i1, neg, el)
    m2 = jnp.max(el2, axis=-1, keepdims=True)
    i2 = jnp.min(jnp.where(el2 == m2, lane, LANES), axis=-1, keepdims=True)
    e2 = jnp.exp(m2 - m1)
    w1 = grp_p / (1.0 + e2)
    oh1 = (lane == i1).astype(F32)
    oh2 = (lane == i2).astype(F32)
    oh = oh1 + oh2
    cum = jnp.dot(tri_ref[...], oh.astype(BF16), preferred_element_type=F32) + carry_ref[...]
    before = cum - oh
    r1 = jnp.sum(oh1 * before, axis=-1, keepdims=True)
    r2 = jnp.sum(oh2 * before, axis=-1, keepdims=True)
    carry_ref[...] = cum[tm - 1:tm, :]
    cnt_ref[...] = cum[tm - 1:tm, :]
    cols = ((ROUTE_EXP, i1 - ROUTER_EXP), (ROUTE_EXP + 1, i2 - ROUTER_EXP), (ROUTE_GATE, w1), (ROUTE_GATE + 1, w1 * e2),
            (ROUTE_RANK, r1), (ROUTE_RANK + 1, r2))
    out = jnp.zeros_like(logits)
    for col, val in cols:
        out = jnp.where(lane == col, val, out)
    route_ref[...] = out
    idx = lax.dot_general(sel_ref[...], out, (((1,), (1,)), ((), ())), precision=HIGHEST,
                          preferred_element_type=F32)
    idx_ref[...] = idx.astype(jnp.int32)


def _router(x, nw, w_pack, b_pack, tm=512):
    t = x.shape[0]
    tri = jnp.asarray((np.arange(tm)[:, None] >= np.arange(tm)[None, :]).astype(np.float32), dtype=BF16)
    sel = np.zeros((8, LANES), np.float32)
    for row, lane in enumerate((ROUTE_EXP, ROUTE_EXP + 1, ROUTE_RANK, ROUTE_RANK + 1)):
        sel[row, lane] = 1.0
    return pl.pallas_call(
        _router_kernel,
        grid=(t // tm,),
        in_specs=[pl.BlockSpec((tm, D_MODEL), lambda i: (i, 0)),
                  pl.BlockSpec((1, D_MODEL), lambda i: (0, 0)),
                  pl.BlockSpec((D_MODEL, LANES), lambda i: (0, 0)),
                  pl.BlockSpec((1, LANES), lambda i: (0, 0)),
                  pl.BlockSpec((tm, tm), lambda i: (0, 0)),
                  pl.BlockSpec((8, LANES), lambda i: (0, 0))],
        out_specs=[pl.BlockSpec((tm, LANES), lambda i: (i, 0)),
                   pl.BlockSpec((8, tm), lambda i: (0, i)),
                   pl.BlockSpec((1, LANES), lambda i: (0, 0))],
        out_shape=[jax.ShapeDtypeStruct((t, LANES), F32),
                   jax.ShapeDtypeStruct((8, t), jnp.int32),
                   jax.ShapeDtypeStruct((1, LANES), F32)],
        scratch_shapes=[pltpu.VMEM((1, LANES), F32)],
        compiler_params=_params("arbitrary"),
        name="router",
    )(x, nw, w_pack, b_pack, tri, jnp.asarray(sel))


def _round_up_tile(v):
    shift = MOE_TILE.bit_length() - 1
    return lax.shift_left(lax.shift_right_logical(v + (MOE_TILE - 1), shift), shift)


def _dispatch_kernel(off_ref, cnt_ref, tot_ref, idx_ref, x_ref, xs_ref, zero_ref, sem_ref, *, tm, n_tiles):
    i = pl.program_id(0)
    n = pl.num_programs(0)
    slot = i % 2

    def row_copy(s, t, p):
        return pltpu.make_async_copy(x_ref.at[pl.ds(t, 1)], xs_ref.at[pl.ds(p, 1)], sem_ref.at[s])

    def pad_row(p):
        return pltpu.make_async_copy(zero_ref.at[pl.ds(0, 1)], xs_ref.at[pl.ds(p, 1)], sem_ref.at[2])

    def pad_tile(j):
        return pltpu.make_async_copy(zero_ref, xs_ref.at[pl.ds(j * MOE_TILE, MOE_TILE)], sem_ref.at[3])

    def pad_rows(fn):
        for e in range(N_EXPERTS):
            lo = cnt_ref[e]
            lax.fori_loop(lo, _round_up_tile(lo), functools.partial(fn, off_ref[e]), 0)

    @pl.when(i == 0)
    def _():
        zero_ref[...] = jnp.zeros_like(zero_ref)

        def start_row(base, r, c):
            pad_row(base + r).start()
            return c

        def wait_row(base, r, c):
            pad_row(0).wait()
            return c

        def start_tile(j, c):
            pad_tile(j).start()
            return c

        def wait_tile(j, c):
            pad_tile(0).wait()
            return c

        pad_rows(start_row)
        lax.fori_loop(tot_ref[0], n_tiles, start_tile, 0)
        pad_rows(wait_row)
        lax.fori_loop(tot_ref[0], n_tiles, wait_tile, 0)

    def issue(r, c):
        for k in range(TOP_K):
            p = off_ref[idx_ref[k, r]] + idx_ref[TOP_K + k, r]
            row_copy(slot, i * tm + r, p).start()
        return c

    lax.fori_loop(0, tm, issue, 0, unroll=8)

    def drain(s):
        def wait(r, c):
            row_copy(s, 0, 0).wait()
            return c

        lax.fori_loop(0, TOP_K * tm, wait, 0, unroll=8)

    @pl.when(i > 0)
    def _():
        drain(1 - slot)

    @pl.when(i == n - 1)
    def _():
        drain(slot)


def _dispatch(x, off, cnt, total, idx_tiles, tm, n_tiles):
    t = x.shape[0]
    kern = functools.partial(_dispatch_kernel, tm=tm, n_tiles=n_tiles)
    return pl.pallas_call(
        kern,
        grid_spec=pltpu.PrefetchScalarGridSpec(
            num_scalar_prefetch=3,
            grid=(t // tm,),
            in_specs=[pl.BlockSpec((8, tm), lambda i, *_: (0, i), memory_space=pltpu.SMEM),
                      pl.BlockSpec(memory_space=pl.ANY)],
            out_specs=pl.BlockSpec(memory_space=pl.ANY),
            scratch_shapes=[pltpu.VMEM((MOE_TILE, D_MODEL), F32),
                            pltpu.SemaphoreType.DMA((4,))]),
        out_shape=jax.ShapeDtypeStruct((n_tiles * MOE_TILE, D_MODEL), F32),
        compiler_params=_params("arbitrary"),
        name="dispatch",
    )(off, cnt, total, idx_tiles, x)


def _ffn_kernel(te_ref, first_ref, tot_ref, xs_ref, nw_ref, w1_ref, w3_ref, w2_ref, ys_ref, w1b, w3b, w2b):
    j = pl.program_id(0)

    @pl.when(first_ref[j] == 1)
    def _():
        w1b[...] = w1_ref[0].astype(BF16)
        w3b[...] = w3_ref[0].astype(BF16)
        w2b[...] = w2_ref[0].astype(BF16)

    @pl.when(j < tot_ref[0])
    def _():
        hn = _rms(xs_ref[...], nw_ref[...]).astype(BF16)
        a = jnp.dot(hn, w1b[...], preferred_element_type=F32)
        u = jnp.dot(hn, w3b[...], preferred_element_type=F32)
        hid = (_silu(a) * u).astype(BF16)
        ys_ref[...] = jnp.dot(hid, w2b[...], preferred_element_type=F32)

    @pl.when(j >= tot_ref[0])
    def _():
        ys_ref[...] = jnp.zeros_like(ys_ref)


def _ffn(xs, nw, w1, w3, w2, tile_e, tile_first, total):
    n_tiles = tile_e.shape[0]
    tok = lambda j, te, fi, tot: (j, 0)
    wgt = lambda j, te, fi, tot: (te[j], 0, 0)
    return pl.pallas_call(
        _ffn_kernel,
        grid_spec=pltpu.PrefetchScalarGridSpec(
            num_scalar_prefetch=3,
            grid=(n_tiles,),
            in_specs=[pl.BlockSpec((MOE_TILE, D_MODEL), tok),
                      pl.BlockSpec((1, D_MODEL), lambda j, te, fi, tot: (0, 0)),
                      pl.BlockSpec((1, D_MODEL, EXPERT_FF), wgt),
                      pl.BlockSpec((1, D_MODEL, EXPERT_FF), wgt),
                      pl.BlockSpec((1, EXPERT_FF, D_MODEL), wgt)],
            out_specs=pl.BlockSpec((MOE_TILE, D_MODEL), tok),
            scratch_shapes=[pltpu.VMEM((D_MODEL, EXPERT_FF), BF16),
                            pltpu.VMEM((D_MODEL, EXPERT_FF), BF16),
                            pltpu.VMEM((EXPERT_FF, D_MODEL), BF16)]),
        out_shape=jax.ShapeDtypeStruct((n_tiles * MOE_TILE, D_MODEL), F32),
        compiler_params=_params("arbitrary"),
        name="ffn",
    )(tile_e, tile_first, total, xs, nw, w1, w3, w2)


def _combine_kernel(off_ref, idx_ref, idxn_ref, x_ref, route_ref, fw_ref, ys_ref, o_ref, ybuf, sem_ref, *,
                    tm, final_norm):
    i = pl.program_id(0)
    n = pl.num_programs(0)
    slot = i % 2

    def row_copy(s, k, r, p):
        return pltpu.make_async_copy(ys_ref.at[pl.ds(p, 1)], ybuf.at[s, k, pl.ds(r, 1)], sem_ref.at[s])

    def issue(i_ref, s):
        def body(r, c):
            for k in range(TOP_K):
                p = off_ref[i_ref[k, r]] + i_ref[TOP_K + k, r]
                row_copy(s, k, r, p).start()
            return c

        lax.fori_loop(0, tm, body, 0, unroll=8)

    @pl.when(i == 0)
    def _():
        issue(idx_ref, 0)

    @pl.when(i + 1 < n)
    def _():
        issue(idxn_ref, 1 - slot)

    def wait(r, c):
        row_copy(slot, 0, 0, 0).wait()
        return c

    lax.fori_loop(0, TOP_K * tm, wait, 0, unroll=8)
    g1 = route_ref[:, ROUTE_GATE:ROUTE_GATE + 1]
    g2 = route_ref[:, ROUTE_GATE + 1:ROUTE_GATE + 2]
    out = x_ref[...] + g1 * ybuf[slot, 0] + g2 * ybuf[slot, 1]
    if final_norm:
        out = _rms(out, fw_ref[...])
    o_ref[...] = out


def _combine(x, route, ys, off, idx_tiles, fw, final_norm, tm):
    t = x.shape[0]
    n = t // tm
    kern = functools.partial(_combine_kernel, tm=tm, final_norm=final_norm)
    smem = lambda imap: pl.BlockSpec((8, tm), imap, memory_space=pltpu.SMEM)
    return pl.pallas_call(
        kern,
        grid_spec=pltpu.PrefetchScalarGridSpec(
            num_scalar_prefetch=1,
            grid=(n,),
            in_specs=[smem(lambda i, off: (0, i)),
                      smem(lambda i, off: (0, jnp.minimum(i + 1, n - 1))),
                      pl.BlockSpec((tm, D_MODEL), lambda i, off: (i, 0)),
                      pl.BlockSpec((tm, LANES), lambda i, off: (i, 0)),
                      pl.BlockSpec((1, D_MODEL), lambda i, off: (0, 0)),
                      pl.BlockSpec(memory_space=pl.ANY)],
            out_specs=pl.BlockSpec((tm, D_MODEL), lambda i, off: (i, 0)),
            scratch_shapes=[pltpu.VMEM((2, TOP_K, tm, D_MODEL), F32),
                            pltpu.SemaphoreType.DMA((2,))]),
        out_shape=jax.ShapeDtypeStruct((t, D_MODEL), F32),
        compiler_params=_params("arbitrary"),
        name="combine",
    )(off, idx_tiles, idx_tiles, x, route, fw, ys)


def _moe(x, nw, w_pack, b_pack, w1, w3, w2, fw, final_norm, tm=256):
    t = x.shape[0]
    route, idx_tiles, cnt_row = _router(x, nw, w_pack, b_pack)
    cnt = cnt_row[0, ROUTER_EXP:ROUTER_EXP + N_EXPERTS].astype(jnp.int32)
    n_tiles = TOP_K * t // MOE_TILE + N_EXPERTS
    nblk = (cnt + MOE_TILE - 1) // MOE_TILE
    cend = jnp.cumsum(nblk)
    total = cend[-1:]
    off = (cend - nblk) * MOE_TILE
    j = jnp.arange(n_tiles, dtype=jnp.int32)
    tile_e = jnp.minimum(jnp.searchsorted(cend, j, side="right"), N_EXPERTS - 1).astype(jnp.int32)
    tile_first = ((j < total) & (j * MOE_TILE == off[tile_e])).astype(jnp.int32)
    xs = _dispatch(x, off, cnt, total, idx_tiles, tm, n_tiles)
    ys = _ffn(xs, nw, w1, w3, w2, tile_e, tile_first, total)
    return _combine(x, route, ys, off, idx_tiles, fw, final_norm, tm)


def _pack_in_weights(w_in_l):
    off = np.concatenate([[0], np.cumsum(IN_SPLITS)]).tolist()
    fq, fk, fv, ff, rq, rk, rv, rg, gq, gk, gv, gz, ga, gb = [(off[i], IN_SPLITS[i]) for i in range(len(IN_SPLITS))]
    half = HEAD64 // 2

    def permuted(seg):
        return [(seg[0] + (2 * p + hh) * HEAD64 + lo * half, half)
                for p in range(RET_HEADS // 2) for lo in range(2) for hh in range(2)]

    col_scale = np.ones((1, w_in_l.shape[1]), np.float32)
    col_scale[:, fq[0]:fq[0] + fq[1]] = LOG2E * HEAD64 ** -0.5
    w_bf = (w_in_l * col_scale).astype(BF16)
    cols = lambda segs: [w_bf[:, a:a + n] for a, n in segs]
    wf = jnp.concatenate(cols([fq, fk, fv]), axis=1)
    pad = jnp.zeros((D_MODEL, LANES - (FOX_HEADS + 2 * GDN_HEADS)), BF16)
    wr = jnp.concatenate(cols(permuted(rq) + permuted(rk) + [rv, rg, gq, gk, gv, gz, ff, ga, gb]) + [pad], axis=1)
    return wf, wr


def _lane_row(vals, offset):
    return jnp.zeros((1, LANES), F32).at[0, offset:offset + vals.shape[0]].set(vals.astype(F32))


def kernel(x, norm1_w, w_in, fox_forget_bias, gdn_conv_w, gdn_a_log, gdn_dt_bias, gdn_norm_w, w_out, norm2_w,
           router_group_w, router_group_b, router_expert_w, router_expert_b, expert_w1, expert_w3, expert_w2,
           final_norm_w):
    b, s, d = x.shape
    t = b * s
    depth = w_in.shape[0]
    fox_tk = 512
    xt = x.reshape(t, d)
    ret_tables = _ret_tables(s)
    for l in range(depth):
        wf, wr = _pack_in_weights(w_in[l])
        qkv, rest = _inproj(xt, norm1_w[l].reshape(1, d), wf, wr)
        rest3 = rest.reshape(b, s, REST_WIDTH)
        c = _fgate(rest3, _lane_row(fox_forget_bias[l], SMALL_FF)).reshape(b, FOX_HEADS, s // fox_tk, 1, fox_tk)
        o_fox = _fox(qkv.reshape(b, s, 3 * FOX_WIDTH), c, tk=fox_tk)
        o_ret = _ret(rest3, ret_tables)
        prep = _gdn_prep(rest3, gdn_conv_w[l].astype(F32), _lane_row(gdn_a_log[l], SMALL_GA),
                         _lane_row(gdn_dt_bias[l], SMALL_GA))
        o_gdn = _gdn_scan(*prep, rest3, gdn_norm_w[l].reshape(1, GDN_DIM).astype(F32))
        xt = _outproj(xt, o_fox.reshape(t, FOX_WIDTH), o_ret.reshape(t, RET_WIDTH),
                      o_gdn.reshape(t, GDN_WIDTH), w_out[l].astype(BF16))
        w_pack = jnp.concatenate([router_group_w[l], router_expert_w[l],
                                  jnp.zeros((d, LANES - N_GROUPS - N_EXPERTS), F32)], axis=1)
        b_pack = jnp.concatenate([router_group_b[l].reshape(-1), router_expert_b[l].reshape(-1),
                                  jnp.zeros((LANES - N_GROUPS - N_EXPERTS,), F32)]).reshape(1, LANES)
        xt = _moe(xt, norm2_w[l].reshape(1, d), w_pack, b_pack,
                  expert_w1[l].reshape(N_EXPERTS, d, EXPERT_FF),
                  expert_w3[l].reshape(N_EXPERTS, d, EXPERT_FF),
                  expert_w2[l].reshape(N_EXPERTS, EXPERT_FF, d),
                  final_norm_w.reshape(1, d), final_norm=(l == depth - 1))
    return xt.reshape(b, s, d)
```

```python
import functools
import math

import jax
import jax.numpy as jnp
import numpy as np
from jax import lax
from jax.experimental import pallas as pl
from jax.experimental.pallas import tpu as pltpu

F32 = jnp.float32
BF16 = jnp.bfloat16
HIGHEST = lax.Precision.HIGHEST

D_MODEL = 1024
FOX_HEADS = 4
RET_HEADS = 4
GDN_HEADS = 4
HEAD64 = 64
GDN_DIM = 128
FOX_WIDTH = FOX_HEADS * HEAD64
RET_WIDTH = RET_HEADS * HEAD64
GDN_WIDTH = GDN_HEADS * GDN_DIM
RET_CHUNK = 128
GDN_CHUNK = 64
GDN_PREP_GROUP = 8
CONV_WIDTH = 4
RET_ANGLE_BASE = 10000.0
N_GROUPS = 4
EXPERTS_PER_GROUP = 8
N_EXPERTS = N_GROUPS * EXPERTS_PER_GROUP
EXPERT_FF = 256
NORM_EPS = 1e-6
LOG2E = math.log2(math.e)
LANES = 128
IN_SPLITS = (FOX_WIDTH, FOX_WIDTH, FOX_WIDTH, FOX_HEADS,
             RET_WIDTH, RET_WIDTH, RET_WIDTH, RET_WIDTH,
             GDN_WIDTH, GDN_WIDTH, GDN_WIDTH, GDN_WIDTH, GDN_HEADS, GDN_HEADS)

REST_RQ, REST_RK, REST_RV, REST_RG = 0, 256, 512, 768
REST_GQ, REST_GK, REST_GV, REST_GZ = 1024, 1536, 2048, 2560
REST_SMALL = 3072
REST_WIDTH = 3200
SMALL_FF, SMALL_GA, SMALL_GB = 0, 4, 8
ROUTER_GRP, ROUTER_EXP = 0, 4
ROUTE_EXP, ROUTE_GATE, ROUTE_RANK = 0, 2, 4
TOP_K = 2
MOE_TILE = 256

VMEM_LIMIT = 56 * 1024 * 1024


def _params(*sem):
    return pltpu.CompilerParams(dimension_semantics=sem, vmem_limit_bytes=VMEM_LIMIT)


def _mm(a, b):
    return jnp.dot(a.astype(BF16), b.astype(BF16), preferred_element_type=F32)


def _mm_nt(a, b):
    return lax.dot_general(a.astype(BF16), b.astype(BF16), (((1,), (1,)), ((), ())),
                           preferred_element_type=F32)


def _mm_tn(a, b):
    return lax.dot_general(a.astype(BF16), b.astype(BF16), (((0,), (0,)), ((), ())),
                           preferred_element_type=F32)


def _silu(x):
    return x * (1.0 / (1.0 + jnp.exp(-x)))


def _rms(x, w):
    return x * lax.rsqrt(jnp.mean(x * x, axis=-1, keepdims=True) + NORM_EPS) * w


def _inproj_kernel(x_ref, nw_ref, wf_ref, wr_ref, of_ref, or_ref):
    hn = _rms(x_ref[...], nw_ref[...]).astype(BF16)
    of_ref[...] = jnp.dot(hn, wf_ref[...], preferred_element_type=F32).astype(BF16)
    step = 640
    for c in range(0, REST_WIDTH, step):
        or_ref[:, c:c + step] = jnp.dot(hn, wr_ref[:, c:c + step], preferred_element_type=F32)


def _inproj(x, nw, wf, wr, tm=512):
    t = x.shape[0]
    return pl.pallas_call(
        _inproj_kernel,
        grid=(t // tm,),
        in_specs=[pl.BlockSpec((tm, D_MODEL), lambda i: (i, 0)),
                  pl.BlockSpec((1, D_MODEL), lambda i: (0, 0)),
                  pl.BlockSpec((D_MODEL, 3 * FOX_WIDTH), lambda i: (0, 0)),
                  pl.BlockSpec((D_MODEL, REST_WIDTH), lambda i: (0, 0))],
        out_specs=[pl.BlockSpec((tm, 3 * FOX_WIDTH), lambda i: (i, 0)),
                   pl.BlockSpec((tm, REST_WIDTH), lambda i: (i, 0))],
        out_shape=[jax.ShapeDtypeStruct((t, 3 * FOX_WIDTH), BF16),
                   jax.ShapeDtypeStruct((t, REST_WIDTH), F32)],
        compiler_params=_params("parallel"),
        name="inproj",
    )(x, nw, wf, wr)


def _fgate_kernel(sm_ref, bias_ref, sel_ref, tri_ref, c_ref):
    n_blk = sm_ref.shape[1] // LANES
    z = sm_ref[0] + bias_ref[...]
    lf = jnp.minimum(z, 0.0) - jnp.log1p(jnp.exp(-jnp.abs(z)))
    sel = sel_ref[...]
    tri = tri_ref[...]
    within = []
    for j in range(n_blk):
        blk = lf[j * LANES:(j + 1) * LANES, :]
        x = lax.dot_general(sel, blk, (((1,), (1,)), ((), ())), precision=HIGHEST, preferred_element_type=F32)
        within.append(jnp.dot(x, tri, precision=HIGHEST, preferred_element_type=F32))
    carry = jnp.zeros((sel.shape[0], 1), F32)
    for j in range(n_blk):
        cj = within[j] + carry
        for h in range(FOX_HEADS):
            c_ref[0, h, :, j * LANES:(j + 1) * LANES] = cj[h:h + 1, :]
        carry = cj[:, LANES - 1:LANES]


def _fgate(rest3, bias_row):
    b, s, _ = rest3.shape
    tri = (np.arange(LANES)[:, None] <= np.arange(LANES)[None, :]).astype(np.float32)
    sel = (np.arange(8)[:, None] == np.arange(LANES)[None, :]).astype(np.float32)
    sel[FOX_HEADS:] = 0.0
    return pl.pallas_call(
        _fgate_kernel,
        grid=(b,),
        in_specs=[pl.BlockSpec((1, s, LANES), lambda i: (i, 0, REST_SMALL // LANES)),
                  pl.BlockSpec((1, LANES), lambda i: (0, 0)),
                  pl.BlockSpec((8, LANES), lambda i: (0, 0)),
                  pl.BlockSpec((LANES, LANES), lambda i: (0, 0))],
        out_specs=pl.BlockSpec((1, FOX_HEADS, 1, s), lambda i: (i, 0, 0, 0)),
        out_shape=jax.ShapeDtypeStruct((b, FOX_HEADS, 1, s), F32),
        compiler_params=_params("parallel"),
        name="fgate",
    )(rest3, bias_row, jnp.asarray(sel), jnp.asarray(tri))


def _fox_kernel(q_ref, k_ref, v_ref, c_ref, o_ref, sa_ref, sb_ref, *, tq, tk):
    i = pl.program_id(2)
    lane = lax.broadcasted_iota(jnp.int32, (1, LANES), 1)
    first = lane < HEAD64
    q = q_ref[0]
    zero = jnp.zeros_like(q)
    qh = (jnp.where(first, q, zero), jnp.where(first, zero, q))
    nfull = (i * tq) // tk
    cbase = [c_ref[0, hh, nfull][:, 0:1] for hh in range(2)]
    qpos = i * tq + lax.broadcasted_iota(jnp.int32, (tq, 1), 0)
    den = (HEAD64, 0)
    lane_v = lax.broadcasted_iota(jnp.int32, (tk, LANES), 1)
    keep = (lane_v < HEAD64, lane_v >= HEAD64)
    ones_col = tuple(jnp.where(lane_v == d, 1.0, 0.0).astype(BF16) for d in den)

    def scores(j, s_ref):
        k0 = pl.multiple_of(j * tk, tk)
        k = k_ref[0, pl.ds(k0, tk), :]
        for hh in range(2):
            s = lax.dot_general(qh[hh], k, (((1,), (1,)), ((), ())), preferred_element_type=F32)
            s_ref[hh] = s + (cbase[hh] - c_ref[0, hh, j]) * LOG2E

    def update(j, s_ref, carry, masked):
        k0 = pl.multiple_of(j * tk, tk)
        v = v_ref[0, pl.ds(k0, tk), :]
        vh = tuple(jnp.where(keep[hh], v, ones_col[hh]) for hh in range(2))
        out = []
        for hh in range(2):
            m, acc = carry[hh]
            s = s_ref[hh]
            if masked:
                kpos = j * tk + lax.broadcasted_iota(jnp.int32, (1, tk), 1)
                s = jnp.where(kpos <= qpos, s, -jnp.inf)
            m_new = jnp.maximum(m, jnp.max(s, axis=-1, keepdims=True))
            alpha = jnp.exp2(m - m_new)
            p = jnp.exp2(s - m_new)
            acc = alpha * acc + jnp.dot(p.astype(BF16), vh[hh], preferred_element_type=F32)
            out.append((m_new, acc))
        return tuple(out)

    def pair(jj, carry):
        j = 2 * jj
        scores(j + 1, sb_ref)
        carry = update(j, sa_ref, carry, False)
        scores(j + 2, sa_ref)
        return update(j + 1, sb_ref, carry, False)

    def tail_even(carry):
        return update(nfull, sa_ref, carry, True)

    def tail_odd(carry):
        scores(nfull, sb_ref)
        carry = update(nfull - 1, sa_ref, carry, False)
        return update(nfull, sb_ref, carry, True)

    init = tuple((jnp.full((tq, 1), -jnp.inf, F32), jnp.zeros((tq, LANES), F32)) for _ in range(2))
    scores(0, sa_ref)
    carry = lax.fori_loop(0, nfull // 2, pair, init)
    carry = lax.cond(nfull % 2 == 1, tail_odd, tail_even, carry)
    acc0, acc1 = carry[0][1], carry[1][1]
    o0 = acc0 / acc0[:, den[0]:den[0] + 1]
    o1 = acc1 / acc1[:, den[1]:den[1] + 1]
    o_ref[0] = jnp.where(first, o0, o1).astype(BF16)


def _fox(qkv, c, tq=256, tk=512):
    b, s, _ = qkv.shape
    npair = FOX_HEADS // 2
    kern = functools.partial(_fox_kernel, tq=tq, tk=tk)
    return pl.pallas_call(
        kern,
        grid=(b, npair, s // tq),
        in_specs=[pl.BlockSpec((1, tq, LANES), lambda bi, p, i: (bi, i, p)),
                  pl.BlockSpec((1, s, LANES), lambda bi, p, i: (bi, 0, npair + p)),
                  pl.BlockSpec((1, s, LANES), lambda bi, p, i: (bi, 0, 2 * npair + p)),
                  pl.BlockSpec((1, 2, s // tk, 1, tk), lambda bi, p, i: (bi, p, 0, 0, 0))],
        out_specs=pl.BlockSpec((1, tq, LANES), lambda bi, p, i: (bi, i, p)),
        out_shape=jax.ShapeDtypeStruct((b, s, FOX_WIDTH), BF16),
        scratch_shapes=[pltpu.VMEM((2, tq, tk), F32), pltpu.VMEM((2, tq, tk), F32)],
        compiler_params=_params("parallel", "parallel", "arbitrary"),
        name="fox",
    )(qkv, qkv, qkv, c)


def _ret_kernel(q_ref, k_ref, v_ref, g_ref, cos_ref, sin_ref, dmat_ref, qdec_ref, kdec_ref, cd_ref, bm_ref,
                o_ref, state_ref, *, ts):
    @pl.when(pl.program_id(2) == 0)
    def _():
        state_ref[...] = jnp.zeros_like(state_ref)

    lane = lax.broadcasted_iota(jnp.int32, (1, LANES), 1)
    q_first = (lane % HEAD64) < (HEAD64 // 2)
    v_first = lane < HEAD64
    c_len = RET_CHUNK
    for c in range(ts // c_len):
        rows = slice(c * c_len, (c + 1) * c_len)
        cos = cos_ref[rows, :]
        sin = sin_ref[rows, :]
        q = q_ref[0, rows, :]
        k = k_ref[0, rows, :]
        v = v_ref[0, rows, :]
        qr = q * cos + pltpu.roll(q, LANES // 2, 1) * sin
        kr = k * cos + pltpu.roll(k, LANES // 2, 1) * sin
        q2 = jnp.concatenate([jnp.where(q_first, qr, 0.0), jnp.where(q_first, 0.0, qr)], axis=0)
        s = _mm_nt(q2, kr * (HEAD64 ** -0.5))
        s0 = s[:c_len] * dmat_ref[0]
        s1 = s[c_len:] * dmat_ref[1]
        o = _mm(s0, jnp.where(v_first, v, 0.0)) + _mm(s1, jnp.where(v_first, 0.0, v))
        state = state_ref[...]
        o = o + _mm(qr * qdec_ref[0], state)
        kv = _mm_tn(kr * kdec_ref[0], v)
        state_ref[...] = state * cd_ref[0] + kv * bm_ref[0]
        sq = o * o
        ms0 = jnp.sum(jnp.where(v_first, sq, 0.0), axis=-1, keepdims=True)
        ms1 = jnp.sum(jnp.where(v_first, 0.0, sq), axis=-1, keepdims=True)
        ms = jnp.where(v_first, ms0, ms1) * (1.0 / HEAD64)
        y = o * lax.rsqrt(ms + NORM_EPS) * _silu(g_ref[0, rows, :])
        o_ref[0, rows, :] = y.astype(BF16)


def _ret_tables(s):
    npair = RET_HEADS // 2
    half = HEAD64 // 2
    lane = np.arange(LANES)
    log_g = np.log1p(-np.exp2(-5.0 - np.arange(RET_HEADS, dtype=np.float32))).astype(np.float32)
    idx = np.arange(RET_CHUNK, dtype=np.float32)
    rel = idx[:, None] - idx[None, :]
    dmat = np.where(rel[None] >= 0, np.exp(np.maximum(rel, 0.0)[None] * log_g[:, None, None]), 0.0)
    qdec, kdec, cd, bm = [], [], [], []
    for p in range(npair):
        hq = 2 * p + ((lane % HEAD64) >= half)
        hv = 2 * p + (lane >= HEAD64)
        qdec.append(np.exp((idx[:, None] + 1.0) * log_g[hq][None, :]))
        kdec.append(np.exp((RET_CHUNK - 1 - idx)[:, None] * log_g[hq][None, :]) * HEAD64 ** -0.5)
        cd.append(np.broadcast_to(np.exp(RET_CHUNK * log_g[hq])[:, None], (LANES, LANES)))
        bm.append((hq[:, None] == hv[None, :]).astype(np.float32))
    tabs = [np.stack(a).astype(np.float32) for a in (qdec, kdec, cd, bm)]
    inv = 1.0 / (RET_ANGLE_BASE ** jnp.linspace(0.0, 1.0, half, dtype=F32))
    pos = jnp.arange(s, dtype=F32)
    ang = pos[:, None] * inv[None, :]
    cos = jnp.tile(jnp.cos(ang), (1, LANES // half))
    sin = jnp.tile(jnp.sin(ang), (1, LANES // half))
    sin = jnp.where(jnp.asarray(lane)[None, :] < LANES // 2, -sin, sin)
    return [jnp.asarray(dmat.astype(np.float32))] + [jnp.asarray(a) for a in tabs] + [cos, sin]


def _ret(rest3, tables, ts=1024):
    b, s, _ = rest3.shape
    dmat, qdec, kdec, cd, bm, cos, sin = tables
    npair = RET_HEADS // 2
    col = lambda off: (lambda bi, p, i: (bi, i, off // LANES + p))
    tab = lambda bi, p, i: (p, 0, 0)
    kern = functools.partial(_ret_kernel, ts=ts)
    return pl.pallas_call(
        kern,
        grid=(b, npair, s // ts),
        in_specs=[pl.BlockSpec((1, ts, LANES), col(REST_RQ)),
                  pl.BlockSpec((1, ts, LANES), col(REST_RK)),
                  pl.BlockSpec((1, ts, LANES), col(REST_RV)),
                  pl.BlockSpec((1, ts, LANES), col(REST_RG)),
                  pl.BlockSpec((ts, LANES), lambda bi, p, i: (i, 0)),
                  pl.BlockSpec((ts, LANES), lambda bi, p, i: (i, 0)),
                  pl.BlockSpec((2, RET_CHUNK, RET_CHUNK), tab),
                  pl.BlockSpec((1, RET_CHUNK, LANES), tab),
                  pl.BlockSpec((1, RET_CHUNK, LANES), tab),
                  pl.BlockSpec((1, LANES, LANES), tab),
                  pl.BlockSpec((1, LANES, LANES), tab)],
        out_specs=pl.BlockSpec((1, ts, LANES), lambda bi, p, i: (bi, i, p)),
        out_shape=jax.ShapeDtypeStruct((b, s, RET_WIDTH), BF16),
        scratch_shapes=[pltpu.VMEM((LANES, LANES), F32)],
        compiler_params=_params("parallel", "parallel", "arbitrary"),
        name="retention",
    )(rest3, rest3, rest3, rest3, cos, sin, dmat, qdec, kdec, cd, bm)


def _gdn_prep_kernel(q_ref, k_ref, v_ref, sm_ref, wq_ref, wk_ref, wv_ref, alog_ref, dtb_ref,
                     u0_ref, w_ref, qg_ref, kt_ref, at_ref, eg_ref):
    h = pl.program_id(1)
    c_len = GDN_CHUNK
    n_chunks = q_ref.shape[1] // c_len
    lane = lax.broadcasted_iota(jnp.int32, (1, LANES), 1)
    ri = lax.broadcasted_iota(jnp.int32, (c_len, c_len), 0)
    ci = lax.broadcasted_iota(jnp.int32, (c_len, c_len), 1)
    incl = ri >= ci
    strict = ri > ci
    eye = (ri == ci).astype(F32)
    neg_a = -jnp.exp(alog_ref[...])
    dtb = dtb_ref[...]

    grp = GDN_PREP_GROUP
    rows = grp * c_len

    def conv_silu(ref, w_ref_, n, r0):
        cur = ref[0, pl.ds(r0, rows), :]
        p0 = pl.multiple_of(jnp.maximum(r0 - 8, 0), 8)
        prev = ref[0, pl.ds(p0, 8), :]
        prev = jnp.where(jnp.broadcast_to(n > 0, prev.shape), prev, 0.0)
        xc = jnp.concatenate([prev, cur], axis=0)
        w = w_ref_[...]
        y = cur * w[CONV_WIDTH - 1:CONV_WIDTH, :]
        for j in range(CONV_WIDTH - 1):
            shifted = pltpu.roll(xc, CONV_WIDTH - 1 - j, 0)[8:, :]
            y = y + shifted * w[j:j + 1, :]
        return _silu(y)

    def group(n, carry):
        r0 = pl.multiple_of(n * rows, rows)
        cq = conv_silu(q_ref, wq_ref, n, r0)
        ck = conv_silu(k_ref, wk_ref, n, r0)
        cv = conv_silu(v_ref, wv_ref, n, r0)
        qn = cq * lax.rsqrt(jnp.sum(cq * cq, axis=-1, keepdims=True) + NORM_EPS) * (GDN_DIM ** -0.5)
        kn = ck * lax.rsqrt(jnp.sum(ck * ck, axis=-1, keepdims=True) + NORM_EPS)
        sm = sm_ref[0, pl.ds(r0, rows), :]
        z = sm + dtb
        g_all = neg_a * (jnp.maximum(z, 0.0) + jnp.log1p(jnp.exp(-jnp.abs(z))))
        beta_all = 1.0 / (1.0 + jnp.exp(-sm))
        g_col = jnp.sum(jnp.where(lane == SMALL_GA + h, g_all, 0.0), axis=-1, keepdims=True)
        beta = jnp.sum(jnp.where(lane == SMALL_GB + h, beta_all, 0.0), axis=-1, keepdims=True)
        kb = kn * beta
        vb = cv * beta
        chunks = [slice(g * c_len, (g + 1) * c_len) for g in range(grp)]
        g_row = [jnp.sum(g_col[c] * eye, axis=0, keepdims=True) for c in chunks]
        gc = [jnp.sum(jnp.where(incl, g_row[i], 0.0), axis=-1, keepdims=True) for i in range(grp)]
        gc_row = [jnp.sum(jnp.where(ri <= ci, g_col[c], 0.0), axis=0, keepdims=True) for c in chunks]
        decay = [jnp.where(incl, jnp.exp(jnp.where(incl, gc[i] - gc_row[i], 0.0)), 0.0) for i in range(grp)]
        both = [_mm_nt(jnp.concatenate([kb[c], qn[c]], axis=0), kn[c]) for c in chunks]
        low = [jnp.where(strict, both[i][:c_len] * decay[i], 0.0) for i in range(grp)]
        attn = [jnp.where(incl, both[i][c_len:] * decay[i], 0.0) for i in range(grp)]
        inv = [eye - low[i] for i in range(grp)]
        pw = low
        for _ in range(int(math.log2(c_len)) - 1):
            pw = [_mm(pw[i], pw[i]) for i in range(grp)]
            inv = [inv[i] + _mm(inv[i], pw[i]) for i in range(grp)]
        eg = [jnp.exp(gc[i]) for i in range(grp)]
        sol = [_mm(inv[i], jnp.concatenate([vb[c], kb[c] * eg[i]], axis=1)) for i, c in enumerate(chunks)]
        for i, c in enumerate(chunks):
            dst = pl.ds(pl.multiple_of(r0 + i * c_len, c_len), c_len)
            g_last = gc[i][c_len - 1:c_len, :]
            u0_ref[0, 0, dst, :] = sol[i][:, :GDN_DIM]
            w_ref[0, 0, dst, :] = sol[i][:, GDN_DIM:].astype(BF16)
            qg_ref[0, 0, dst, :] = (qn[c] * eg[i]).astype(BF16)
            kt_ref[0, 0, dst, :] = (kn[c] * jnp.exp(g_last - gc[i])).astype(BF16)
            at_ref[0, 0, dst, :] = attn[i].astype(BF16)
            eg_ref[0, 0, n * grp + i] = jnp.broadcast_to(jnp.exp(g_last), (1, LANES))
        return carry

    lax.fori_loop(0, n_chunks // grp, group, 0)


def _gdn_prep(rest3, conv_w, alog_l, dtb_l):
    b, s, _ = rest3.shape
    nh = GDN_HEADS
    n_chunks = s // GDN_CHUNK
    col = lambda off: (lambda bi, h: (bi, 0, off // LANES + h))
    wcol = lambda g: (lambda bi, h: (0, g * nh + h))
    const = lambda bi, h: (0, 0)
    row =pl.BlockSpec((1, 1, s, GDN_DIM), lambda bi, h: (bi, h, 0, 0))
    return pl.pallas_call(
        _gdn_prep_kernel,
        grid=(b, nh),
        in_specs=[pl.BlockSpec((1, s, LANES), col(REST_GQ)),
                  pl.BlockSpec((1, s, LANES), col(REST_GK)),
                  pl.BlockSpec((1, s, LANES), col(REST_GV)),
                  pl.BlockSpec((1, s, LANES), lambda bi, h: (bi, 0, REST_SMALL // LANES)),
                  pl.BlockSpec((CONV_WIDTH, LANES), wcol(0)),
                  pl.BlockSpec((CONV_WIDTH, LANES), wcol(1)),
                  pl.BlockSpec((CONV_WIDTH, LANES), wcol(2)),
                  pl.BlockSpec((1, LANES), const),
                  pl.BlockSpec((1, LANES), const)],
        out_specs=[row, row, row, row,
                   pl.BlockSpec((1, 1, s, GDN_CHUNK), lambda bi, h: (bi, h, 0, 0)),
                   pl.BlockSpec((1, 1, n_chunks, 1, LANES), lambda bi, h: (bi, h, 0, 0, 0))],
        out_shape=[jax.ShapeDtypeStruct((b, nh, s, GDN_DIM), F32),
                   jax.ShapeDtypeStruct((b, nh, s, GDN_DIM), BF16),
                   jax.ShapeDtypeStruct((b, nh, s, GDN_DIM), BF16),
                   jax.ShapeDtypeStruct((b, nh, s, GDN_DIM), BF16),
                   jax.ShapeDtypeStruct((b, nh, s, GDN_CHUNK), BF16),
                   jax.ShapeDtypeStruct((b, nh, n_chunks, 1, LANES), F32)],
        compiler_params=_params("parallel", "parallel"),
        name="gdn_prep",
    )(rest3, rest3, rest3, rest3, conv_w, conv_w, conv_w, alog_l, dtb_l)


def _gdn_scan_kernel(u0_ref, w_ref, qg_ref, kt_ref, at_ref, eg_ref, z_ref, nw_ref, o_ref, state_ref, *, ts):
    si = pl.program_id(1)

    @pl.when(si == 0)
    def _():
        state_ref[...] = jnp.zeros_like(state_ref)

    c_len = GDN_CHUNK
    per_tile = ts // c_len
    nw = nw_ref[...]
    for c in range(per_tile):
        rows = slice(c * c_len, (c + 1) * c_len)
        heads = range(GDN_HEADS)
        st = [state_ref[h] for h in heads]
        r = [_mm(jnp.concatenate([w_ref[0, h, rows, :], qg_ref[0, h, rows, :]], axis=0), st[h]) for h in heads]
        u = [(u0_ref[0, h, rows, :] - r[h][:c_len]).astype(BF16) for h in heads]
        ku = [_mm_tn(kt_ref[0, h, rows, :], u[h]) for h in heads]
        au = [jnp.dot(at_ref[0, h, rows, :], u[h], preferred_element_type=F32) for h in heads]
        for h in heads:
            state_ref[h] = st[h] * eg_ref[0, h, si * per_tile + c] + ku[h]
        for h in heads:
            cols = slice(h * GDN_DIM, (h + 1) * GDN_DIM)
            y = _rms(r[h][c_len:] + au[h], nw) * _silu(z_ref[0, rows, cols])
            o_ref[0, rows, cols] = y.astype(BF16)


def _gdn_scan(u0, w, qg, kt, at, eg, rest3, norm_w, ts=512):
    b, nh, s, _ = u0.shape
    n_chunks = s // GDN_CHUNK
    blk = lambda d: pl.BlockSpec((1, nh, ts, d), lambda bi, i: (bi, 0, i, 0))
    kern = functools.partial(_gdn_scan_kernel, ts=ts)
    return pl.pallas_call(
        kern,
        grid=(b, s // ts),
        in_specs=[blk(GDN_DIM), blk(GDN_DIM), blk(GDN_DIM), blk(GDN_DIM), blk(GDN_CHUNK),
                  pl.BlockSpec((1, nh, n_chunks, 1, LANES), lambda bi, i: (bi, 0, 0, 0, 0)),
                  pl.BlockSpec((1, ts, GDN_WIDTH), lambda bi, i: (bi, i, REST_GZ // GDN_WIDTH)),
                  pl.BlockSpec((1, GDN_DIM), lambda bi, i: (0, 0))],
        out_specs=pl.BlockSpec((1, ts, GDN_WIDTH), lambda bi, i: (bi, i, 0)),
        out_shape=jax.ShapeDtypeStruct((b, s, GDN_WIDTH), BF16),
        scratch_shapes=[pltpu.VMEM((nh, GDN_DIM, GDN_DIM), F32)],
        compiler_params=_params("parallel", "arbitrary"),
        name="gdn_scan",
    )(u0, w, qg, kt, at, eg, rest3, norm_w)


def _outproj_kernel(x_ref, of_ref, or_ref, og_ref, w_ref, o_ref):
    acc = x_ref[...]
    acc = acc + jnp.dot(of_ref[...], w_ref[0:FOX_WIDTH, :], preferred_element_type=F32)
    acc = acc + jnp.dot(or_ref[...], w_ref[FOX_WIDTH:FOX_WIDTH + RET_WIDTH, :], preferred_element_type=F32)
    acc = acc + jnp.dot(og_ref[...], w_ref[FOX_WIDTH + RET_WIDTH:, :], preferred_element_type=F32)
    o_ref[...] = acc


def _outproj(x, o_fox, o_ret, o_gdn, w_out, tm=512):
    t = x.shape[0]
    d_mix = w_out.shape[0]
    return pl.pallas_call(
        _outproj_kernel,
        grid=(t // tm,),
        in_specs=[pl.BlockSpec((tm, D_MODEL), lambda i: (i, 0)),
                  pl.BlockSpec((tm, FOX_WIDTH), lambda i: (i, 0)),
                  pl.BlockSpec((tm, RET_WIDTH), lambda i: (i, 0)),
                  pl.BlockSpec((tm, GDN_WIDTH), lambda i: (i, 0)),
                  pl.BlockSpec((d_mix, D_MODEL), lambda i: (0, 0))],
        out_specs=pl.BlockSpec((tm, D_MODEL), lambda i: (i, 0)),
        out_shape=jax.ShapeDtypeStruct((t, D_MODEL), F32),
        compiler_params=_params("parallel"),
        name="outproj",
    )(x, o_fox, o_ret, o_gdn, w_out)


def _router_kernel(x_ref, nw_ref, w_ref, b_ref, tri_ref, sel_ref, route_ref, idx_ref, cnt_ref, carry_ref):
    @pl.when(pl.program_id(0) == 0)
    def _():
        carry_ref[...] = jnp.zeros_like(carry_ref)

    tm = x_ref.shape[0]
    hn = _rms(x_ref[...], nw_ref[...])
    logits = jnp.dot(hn, w_ref[...], precision=HIGHEST, preferred_element_type=F32) + b_ref[...]
    lane = lax.broadcasted_iota(jnp.int32, logits.shape, 1).astype(F32)
    neg = -jnp.inf
    gl = jnp.where(lane < ROUTER_EXP, logits, neg)
    gmax = jnp.max(gl, axis=-1, keepdims=True)
    gidx = jnp.min(jnp.where(gl == gmax, lane, LANES), axis=-1, keepdims=True)
    grp_p = 1.0 / jnp.sum(jnp.exp(gl - gmax), axis=-1, keepdims=True)
    lo = ROUTER_EXP + gidx * EXPERTS_PER_GROUP
    el = jnp.where((lane >= lo) & (lane < lo + EXPERTS_PER_GROUP), logits, neg)
    m1 = jnp.max(el, axis=-1, keepdims=True)
    i1 = jnp.min(jnp.where(el == m1, lane, LANES), axis=-1, keepdims=True)
    el2 = jnp.where(lane == i1, neg, el)
    m2 = jnp.max(el2, axis=-1, keepdims=True)
    i2 = jnp.min(jnp.where(el2 == m2, lane, LANES), axis=-1, keepdims=True)
    e2 = jnp.exp(m2 - m1)
    w1 = grp_p / (1.0 + e2)
    oh1 = (lane == i1).astype(F32)
    oh2 = (lane == i2).astype(F32)
    oh = oh1 + oh2
    cum = jnp.dot(tri_ref[...], oh.astype(BF16), preferred_element_type=F32) + carry_ref[...]
    before = cum - oh
    r1 = jnp.sum(oh1 * before, axis=-1, keepdims=True)
    r2 = jnp.sum(oh2 * before, axis=-1, keepdims=True)
    carry_ref[...] = cum[tm - 1:tm, :]
    cnt_ref[...] = cum[tm - 1:tm, :]
    cols = ((ROUTE_EXP, i1 - ROUTER_EXP), (ROUTE_EXP + 1, i2 - ROUTER_EXP), (ROUTE_GATE, w1), (ROUTE_GATE + 1, w1 * e2),
            (ROUTE_RANK, r1), (ROUTE_RANK + 1, r2))
    out = jnp.zeros_like(logits)
    for col, val in cols:
        out = jnp.where(lane == col, val, out)
    route_ref[...] = out
    idx = lax.dot_general(sel_ref[...], out, (((1,), (1,)), ((), ())), precision=HIGHEST,
                          preferred_element_type=F32)
    idx_ref[...] = idx.astype(jnp.int32)


def _router(x, nw, w_pack, b_pack, tm=512):
    t = x.shape[0]
    tri = jnp.asarray((np.arange(tm)[:, None] >= np.arange(tm)[None, :]).astype(np.float32), dtype=BF16)
    sel = np.zeros((8, LANES), np.float32)
    for row, lane in enumerate((ROUTE_EXP, ROUTE_EXP + 1, ROUTE_RANK, ROUTE_RANK + 1)):
        sel[row, lane] = 1.0
    return pl.pallas_call(
        _router_kernel,
        grid=(t // tm,),
        in_specs=[pl.BlockSpec((tm, D_MODEL), lambda i: (i, 0)),
                  pl.BlockSpec((1, D_MODEL), lambda i: (0, 0)),
                  pl.BlockSpec((D_MODEL, LANES), lambda i: (0, 0)),
                  pl.BlockSpec((1, LANES), lambda i: (0, 0)),
                  pl.BlockSpec((tm, tm), lambda i: (0, 0)),
                  pl.BlockSpec((8, LANES), lambda i: (0, 0))],
        out_specs=[pl.BlockSpec((tm, LANES), lambda i: (i, 0)),
                   pl.BlockSpec((8, tm), lambda i: (0, i)),
                   pl.BlockSpec((1, LANES), lambda i: (0, 0))],
        out_shape=[jax.ShapeDtypeStruct((t, LANES), F32),
                   jax.ShapeDtypeStruct((8, t), jnp.int32),
                   jax.ShapeDtypeStruct((1, LANES), F32)],
        scratch_shapes=[pltpu.VMEM((1, LANES), F32)],
        compiler_params=_params("arbitrary"),
        name="router",
    )(x, nw, w_pack, b_pack, tri, jnp.asarray(sel))


def _round_up_tile(v):
    shift = MOE_TILE.bit_length() - 1
    return lax.shift_left(lax.shift_right_logical(v + (MOE_TILE - 1), shift), shift)


def _dispatch_kernel(off_ref, cnt_ref, tot_ref, idx_ref, x_ref, xs_ref, ring_ref, zero_ref, sem_ref, *, tm, n_tiles):
    i = pl.program_id(0)
    n = pl.num_programs(0)
    slot = i % 2

    def row_copy(s, t, p):
        return pltpu.make_async_copy(ring_ref.at[s, pl.ds(t, 1)], xs_ref.at[pl.ds(p, 1)], sem_ref.at[s])

    def pad_row(p):
        return pltpu.make_async_copy(zero_ref.at[pl.ds(0, 1)], xs_ref.at[pl.ds(p, 1)], sem_ref.at[2])

    def pad_tile(j):
        return pltpu.make_async_copy(zero_ref, xs_ref.at[pl.ds(j * MOE_TILE, MOE_TILE)], sem_ref.at[3])

    def pad_rows(fn):
        for e in range(N_EXPERTS):
            lo = cnt_ref[e]
            lax.fori_loop(lo, _round_up_tile(lo), functools.partial(fn, off_ref[e]), 0)

    @pl.when(i == 0)
    def _():
        zero_ref[...] = jnp.zeros_like(zero_ref)

        def start_row(base, r, c):
            pad_row(base + r).start()
            return c

        def wait_row(base, r, c):
            pad_row(0).wait()
            return c

        def start_tile(j, c):
            pad_tile(j).start()
            return c

        def wait_tile(j, c):
            pad_tile(0).wait()
            return c

        pad_rows(start_row)
        lax.fori_loop(tot_ref[0], n_tiles, start_tile, 0)
        pad_rows(wait_row)
        lax.fori_loop(tot_ref[0], n_tiles, wait_tile, 0)

    ring_ref[slot] = x_ref[...]

    def issue(r, c):
        for k in range(TOP_K):
            p = off_ref[idx_ref[k, r]] + idx_ref[TOP_K + k, r]
            row_copy(slot, r, p).start()
        return c

    lax.fori_loop(0, tm, issue, 0, unroll=8)

    def drain(s):
        def wait(r, c):
            row_copy(s, 0, 0).wait()
            return c

        lax.fori_loop(0, TOP_K * tm, wait, 0, unroll=8)

    @pl.when(i > 0)
    def _():
        drain(1 - slot)

    @pl.when(i == n - 1)
    def _():
        drain(slot)


def _dispatch(x, off, cnt, total, idx_tiles, tm, n_tiles):
    t = x.shape[0]
    kern = functools.partial(_dispatch_kernel, tm=tm, n_tiles=n_tiles)
    return pl.pallas_call(
        kern,
        grid_spec=pltpu.PrefetchScalarGridSpec(
            num_scalar_prefetch=3,
            grid=(t // tm,),
            in_specs=[pl.BlockSpec((8, tm), lambda i, *_: (0, i), memory_space=pltpu.SMEM),
                      pl.BlockSpec((tm, D_MODEL), lambda i, *_: (i, 0))],
            out_specs=pl.BlockSpec(memory_space=pl.ANY),
            scratch_shapes=[pltpu.VMEM((2, tm, D_MODEL), F32),
                            pltpu.VMEM((MOE_TILE, D_MODEL), F32),
                            pltpu.SemaphoreType.DMA((4,))]),
        out_shape=jax.ShapeDtypeStruct((n_tiles * MOE_TILE, D_MODEL), F32),
        compiler_params=_params("arbitrary"),
        name="dispatch",
    )(off, cnt, total, idx_tiles, x)


def _ffn_kernel(te_ref, first_ref, tot_ref, xs_ref, nw_ref, w1_ref, w3_ref, w2_ref, ys_ref, w1b, w3b, w2b):
    j = pl.program_id(0)

    @pl.when(first_ref[j] == 1)
    def _():
        w1b[...] = w1_ref[0].astype(BF16)
        w3b[...] = w3_ref[0].astype(BF16)
        w2b[...] = w2_ref[0].astype(BF16)

    @pl.when(j < tot_ref[0])
    def _():
        hn = _rms(xs_ref[...], nw_ref[...]).astype(BF16)
        a = jnp.dot(hn, w1b[...], preferred_element_type=F32)
        u = jnp.dot(hn, w3b[...], preferred_element_type=F32)
        hid = (_silu(a) * u).astype(BF16)
        ys_ref[...] = jnp.dot(hid, w2b[...], preferred_element_type=F32)

    @pl.when(j >= tot_ref[0])
    def _():
        ys_ref[...] = jnp.zeros_like(ys_ref)


def _ffn(xs, nw, w1, w3, w2, tile_e, tile_first, total):
    n_tiles = tile_e.shape[0]
    tok = lambda j, te, fi, tot: (j, 0)
    wgt = lambda j, te, fi, tot: (te[j], 0, 0)
    return pl.pallas_call(
        _ffn_kernel,
        grid_spec=pltpu.PrefetchScalarGridSpec(
            num_scalar_prefetch=3,
            grid=(n_tiles,),
            in_specs=[pl.BlockSpec((MOE_TILE, D_MODEL), tok),
                      pl.BlockSpec((1, D_MODEL), lambda j, te, fi, tot: (0, 0)),
                      pl.BlockSpec((1, D_MODEL, EXPERT_FF), wgt),
                      pl.BlockSpec((1, D_MODEL, EXPERT_FF), wgt),
                      pl.BlockSpec((1, EXPERT_FF, D_MODEL), wgt)],
            out_specs=pl.BlockSpec((MOE_TILE, D_MODEL), tok),
            scratch_shapes=[pltpu.VMEM((D_MODEL, EXPERT_FF), BF16),
                            pltpu.VMEM((D_MODEL, EXPERT_FF), BF16),
                            pltpu.VMEM((EXPERT_FF, D_MODEL), BF16)]),
        out_shape=jax.ShapeDtypeStruct((n_tiles * MOE_TILE, D_MODEL), F32),
        compiler_params=_params("arbitrary"),
        name="ffn",
    )(tile_e, tile_first, total, xs, nw, w1, w3, w2)


def _combine_kernel(off_ref, idx_ref, idxn_ref, x_ref, route_ref, fw_ref, ys_ref, o_ref, ybuf, sem_ref, *,
                    tm, final_norm):
    i = pl.program_id(0)
    n = pl.num_programs(0)
    slot = i % 2

    def row_copy(s, k, r, p):
        return pltpu.make_async_copy(ys_ref.at[pl.ds(p, 1)], ybuf.at[s, k, pl.ds(r, 1)], sem_ref.at[s])

    def issue(i_ref, s):
        def body(r, c):
            for k in range(TOP_K):
                p = off_ref[i_ref[k, r]] + i_ref[TOP_K + k, r]
                row_copy(s, k, r, p).start()
            return c

        lax.fori_loop(0, tm, body, 0, unroll=8)

    @pl.when(i == 0)
    def _():
        issue(idx_ref, 0)

    @pl.when(i + 1 < n)
    def _():
        issue(idxn_ref, 1 - slot)

    def wait(r, c):
        row_copy(slot, 0, 0, 0).wait()
        return c

    lax.fori_loop(0, TOP_K * tm, wait, 0, unroll=8)
    g1 = route_ref[:, ROUTE_GATE:ROUTE_GATE + 1]
    g2 = route_ref[:, ROUTE_GATE + 1:ROUTE_GATE + 2]
    out = x_ref[...] + g1 * ybuf[slot, 0] + g2 * ybuf[slot, 1]
    if final_norm:
        out = _rms(out, fw_ref[...])
    o_ref[...] = out


def _combine(x, route, ys, off, idx_tiles, fw, final_norm, tm):
    t = x.shape[0]
    n = t // tm
    kern = functools.partial(_combine_kernel, tm=tm, final_norm=final_norm)
    smem = lambda imap: pl.BlockSpec((8, tm), imap, memory_space=pltpu.SMEM)
    return pl.pallas_call(
        kern,
        grid_spec=pltpu.PrefetchScalarGridSpec(
            num_scalar_prefetch=1,
            grid=(n,),
            in_specs=[smem(lambda i, off: (0, i)),
                      smem(lambda i, off: (0, jnp.minimum(i + 1, n - 1))),
                      pl.BlockSpec((tm, D_MODEL), lambda i, off: (i, 0)),
                      pl.BlockSpec((tm, LANES), lambda i, off: (i, 0)),
                      pl.BlockSpec((1, D_MODEL), lambda i, off: (0, 0)),
                      pl.BlockSpec(memory_space=pl.ANY)],
            out_specs=pl.BlockSpec((tm, D_MODEL), lambda i, off: (i, 0)),
            scratch_shapes=[pltpu.VMEM((2, TOP_K, tm, D_MODEL), F32),
                            pltpu.SemaphoreType.DMA((2,))]),
        out_shape=jax.ShapeDtypeStruct((t, D_MODEL), F32),
        compiler_params=_params("arbitrary"),
        name="combine",
    )(off, idx_tiles, idx_tiles, x, route, fw, ys)


def _moe(x, nw, w_pack, b_pack, w1, w3, w2, fw, final_norm, tm=256):
    t = x.shape[0]
    route, idx_tiles, cnt_row = _router(x, nw, w_pack, b_pack)
    cnt = cnt_row[0, ROUTER_EXP:ROUTER_EXP + N_EXPERTS].astype(jnp.int32)
    n_tiles = TOP_K * t // MOE_TILE + N_EXPERTS
    nblk = (cnt + MOE_TILE - 1) // MOE_TILE
    cend = jnp.cumsum(nblk)
    total = cend[-1:]
    off = (cend - nblk) * MOE_TILE
    j = jnp.arange(n_tiles, dtype=jnp.int32)
    tile_e = jnp.minimum(jnp.sum((j[:, None] >= cend[None, :]).astype(jnp.int32), axis=1), N_EXPERTS - 1)
    tile_first = ((j < total) & (j * MOE_TILE == off[tile_e])).astype(jnp.int32)
    xs = _dispatch(x, off, cnt, total, idx_tiles, tm, n_tiles)
    ys = _ffn(xs, nw, w1, w3, w2, tile_e, tile_first, total)
    return _combine(x, route, ys, off, idx_tiles, fw, final_norm, tm)


def _pack_in_weights(w_in_l):
    off = np.concatenate([[0], np.cumsum(IN_SPLITS)]).tolist()
    fq, fk, fv, ff, rq, rk, rv, rg, gq, gk, gv, gz, ga, gb = [(off[i], IN_SPLITS[i]) for i in range(len(IN_SPLITS))]
    half = HEAD64 // 2

    def permuted(seg):
        return [(seg[0] + (2 * p + hh) * HEAD64 + lo * half, half)
                for p in range(RET_HEADS // 2) for lo in range(2) for hh in range(2)]

    col_scale = np.ones((1, w_in_l.shape[1]), np.float32)
    col_scale[:, fq[0]:fq[0] + fq[1]] = LOG2E * HEAD64 ** -0.5
    w_bf = (w_in_l * col_scale).astype(BF16)
    cols = lambda segs: [w_bf[:, a:a + n] for a, n in segs]
    wf = jnp.concatenate(cols([fq, fk, fv]), axis=1)
    pad = jnp.zeros((D_MODEL, LANES - (FOX_HEADS + 2 * GDN_HEADS)), BF16)
    wr = jnp.concatenate(cols(permuted(rq) + permuted(rk) + [rv, rg, gq, gk, gv, gz, ff, ga, gb]) + [pad], axis=1)
    return wf, wr


def _lane_row(vals, offset):
    return jnp.zeros((1, LANES), F32).at[0, offset:offset + vals.shape[0]].set(vals.astype(F32))


def kernel(x, norm1_w, w_in, fox_forget_bias, gdn_conv_w, gdn_a_log, gdn_dt_bias, gdn_norm_w, w_out, norm2_w,
           router_group_w, router_group_b, router_expert_w, router_expert_b, expert_w1, expert_w3, expert_w2,
           final_norm_w):
    b, s, d = x.shape
    t = b * s
    depth = w_in.shape[0]
    fox_tk = 512
    xt = x.reshape(t, d)
    ret_tables = _ret_tables(s)
    for l in range(depth):
        wf, wr = _pack_in_weights(w_in[l])
        qkv, rest = _inproj(xt, norm1_w[l].reshape(1, d), wf, wr)
        rest3 = rest.reshape(b, s, REST_WIDTH)
        c = _fgate(rest3, _lane_row(fox_forget_bias[l], SMALL_FF)).reshape(b, FOX_HEADS, s // fox_tk, 1, fox_tk)
        o_fox = _fox(qkv.reshape(b, s, 3 * FOX_WIDTH), c, tk=fox_tk)
        o_ret = _ret(rest3, ret_tables)
        prep = _gdn_prep(rest3, gdn_conv_w[l].astype(F32), _lane_row(gdn_a_log[l], SMALL_GA),
                         _lane_row(gdn_dt_bias[l], SMALL_GA))
        o_gdn = _gdn_scan(*prep, rest3, gdn_norm_w[l].reshape(1, GDN_DIM).astype(F32))
        xt = _outproj(xt, o_fox.reshape(t, FOX_WIDTH), o_ret.reshape(t, RET_WIDTH),
                      o_gdn.reshape(t, GDN_WIDTH), w_out[l].astype(BF16))
        w_pack = jnp.concatenate([router_group_w[l], router_expert_w[l],
                                  jnp.zeros((d, LANES - N_GROUPS - N_EXPERTS), F32)], axis=1)
        b_pack = jnp.concatenate([router_group_b[l].reshape(-1), router_expert_b[l].reshape(-1),
                                  jnp.zeros((LANES - N_GROUPS - N_EXPERTS,), F32)]).reshape(1, LANES)
        xt = _moe(xt, norm2_w[l].reshape(1, d), w_pack, b_pack,
                  expert_w1[l].reshape(N_EXPERTS, d, EXPERT_FF),
                  expert_w3[l].reshape(N_EXPERTS, d, EXPERT_FF),
                  expert_w2[l].reshape(N_EXPERTS, EXPERT_FF, d),
                  final_norm_w.reshape(1, d), final_norm=(l == depth - 1))
    return xt.reshape(b, s, d)
```

```python
import functools
import math

import jax
import jax.numpy as jnp
import numpy as np
from jax import lax
from jax.experimental import pallas as pl
from jax.experimental.pallas import tpu as pltpu

F32 = jnp.float32
BF16 = jnp.bfloat16
HIGHEST = lax.Precision.HIGHEST

D_MODEL = 1024
FOX_HEADS = 4
RET_HEADS = 4
GDN_HEADS = 4
HEAD64 = 64
GDN_DIM = 128
FOX_WIDTH = FOX_HEADS * HEAD64
RET_WIDTH = RET_HEADS * HEAD64
GDN_WIDTH = GDN_HEADS * GDN_DIM
RET_CHUNK = 128
GDN_CHUNK = 64
GDN_PREP_GROUP = 8
CONV_WIDTH = 4
RET_ANGLE_BASE = 10000.0
N_GROUPS = 4
EXPERTS_PER_GROUP = 8
N_EXPERTS = N_GROUPS * EXPERTS_PER_GROUP
EXPERT_FF = 256
NORM_EPS = 1e-6
LOG2E = math.log2(math.e)
LANES = 128
IN_SPLITS = (FOX_WIDTH, FOX_WIDTH, FOX_WIDTH, FOX_HEADS,
             RET_WIDTH, RET_WIDTH, RET_WIDTH, RET_WIDTH,
             GDN_WIDTH, GDN_WIDTH, GDN_WIDTH, GDN_WIDTH, GDN_HEADS, GDN_HEADS)

REST_RQ, REST_RK, REST_RV, REST_RG = 0, 256, 512, 768
REST_GQ, REST_GK, REST_GV, REST_GZ = 1024, 1536, 2048, 2560
REST_SMALL = 3072
REST_WIDTH = 3200
SMALL_FF, SMALL_GA, SMALL_GB = 0, 4, 8
ROUTER_GRP, ROUTER_EXP = 0, 4
ROUTE_CLASS, ROUTE_RANK, ROUTE_GATE = 0, 1, 2
N_PAIRS = EXPERTS_PER_GROUP * (EXPERTS_PER_GROUP - 1) // 2
N_CLASSES = N_GROUPS * N_PAIRS
MOE_TILE = 256
ROW_WIDTH = D_MODEL + LANES

VMEM_LIMIT = 56 * 1024 * 1024


def _params(*sem):
    return pltpu.CompilerParams(dimension_semantics=sem, vmem_limit_bytes=VMEM_LIMIT)


def _mm(a, b):
    return jnp.dot(a.astype(BF16), b.astype(BF16), preferred_element_type=F32)


def _mm_nt(a, b):
    return lax.dot_general(a.astype(BF16), b.astype(BF16), (((1,), (1,)), ((), ())),
                           preferred_element_type=F32)


def _mm_tn(a, b):
    return lax.dot_general(a.astype(BF16), b.astype(BF16), (((0,), (0,)), ((), ())),
                           preferred_element_type=F32)


def _silu(x):
    return x * (1.0 / (1.0 + jnp.exp(-x)))


def _rms(x, w):
    return x * lax.rsqrt(jnp.mean(x * x, axis=-1, keepdims=True) + NORM_EPS) * w


def _inproj_kernel(x_ref, nw_ref, wf_ref, wr_ref, of_ref, or_ref):
    hn = _rms(x_ref[...], nw_ref[...]).astype(BF16)
    of_ref[...] = jnp.dot(hn, wf_ref[...], preferred_element_type=F32).astype(BF16)
    step = 640
    for c in range(0, REST_WIDTH, step):
        or_ref[:, c:c + step] = jnp.dot(hn, wr_ref[:, c:c + step], preferred_element_type=F32)


def _inproj(x, nw, wf, wr, tm=512):
    t = x.shape[0]
    return pl.pallas_call(
        _inproj_kernel,
        grid=(t // tm,),
        in_specs=[pl.BlockSpec((tm, D_MODEL), lambda i: (i, 0)),
                  pl.BlockSpec((1, D_MODEL), lambda i: (0, 0)),
                  pl.BlockSpec((D_MODEL, 3 * FOX_WIDTH), lambda i: (0, 0)),
                  pl.BlockSpec((D_MODEL, REST_WIDTH), lambda i: (0, 0))],
        out_specs=[pl.BlockSpec((tm, 3 * FOX_WIDTH), lambda i: (i, 0)),
                   pl.BlockSpec((tm, REST_WIDTH), lambda i: (i, 0))],
        out_shape=[jax.ShapeDtypeStruct((t, 3 * FOX_WIDTH), BF16),
                   jax.ShapeDtypeStruct((t, REST_WIDTH), F32)],
        compiler_params=_params("parallel"),
        name="inproj",
    )(x, nw, wf, wr)


def _fgate_kernel(sm_ref, bias_ref, sel_ref, tri_ref, c_ref):
    n_blk = sm_ref.shape[1] // LANES
    z = sm_ref[0] + bias_ref[...]
    lf = jnp.minimum(z, 0.0) - jnp.log1p(jnp.exp(-jnp.abs(z)))
    sel = sel_ref[...]
    tri = tri_ref[...]
    within = []
    for j in range(n_blk):
        blk = lf[j * LANES:(j + 1) * LANES, :]
        x = lax.dot_general(sel, blk, (((1,), (1,)), ((), ())), precision=HIGHEST, preferred_element_type=F32)
        within.append(jnp.dot(x, tri, precision=HIGHEST, preferred_element_type=F32))
    carry = jnp.zeros((sel.shape[0], 1), F32)
    for j in range(n_blk):
        cj = within[j] + carry
        for h in range(FOX_HEADS):
            c_ref[0, h, :, j * LANES:(j + 1) * LANES] = cj[h:h + 1, :]
        carry = cj[:, LANES - 1:LANES]


def _fgate(rest3, bias_row):
    b, s, _ = rest3.shape
    tri = (np.arange(LANES)[:, None] <= np.arange(LANES)[None, :]).astype(np.float32)
    sel = (np.arange(8)[:, None] == np.arange(LANES)[None, :]).astype(np.float32)
    sel[FOX_HEADS:] = 0.0
    return pl.pallas_call(
        _fgate_kernel,
        grid=(b,),
        in_specs=[pl.BlockSpec((1, s, LANES), lambda i: (i, 0, REST_SMALL // LANES)),
                  pl.BlockSpec((1, LANES), lambda i: (0, 0)),
                  pl.BlockSpec((8, LANES), lambda i: (0, 0)),
                  pl.BlockSpec((LANES, LANES), lambda i: (0, 0))],
        out_specs=pl.BlockSpec((1, FOX_HEADS, 1, s), lambda i: (i, 0, 0, 0)),
        out_shape=jax.ShapeDtypeStruct((b, FOX_HEADS, 1, s), F32),
        compiler_params=_params("parallel"),
        name="fgate",
    )(rest3, bias_row, jnp.asarray(sel), jnp.asarray(tri))


def _fox_kernel(q_ref, k_ref, v_ref, c_ref, o_ref, sa_ref, sb_ref, *, tq, tk):
    i = pl.program_id(2)
    lane = lax.broadcasted_iota(jnp.int32, (1, LANES), 1)
    first = lane < HEAD64
    q = q_ref[0]
    zero = jnp.zeros_like(q)
    qh = (jnp.where(first, q, zero), jnp.where(first, zero, q))
    nfull = (i * tq) // tk
    cbase = [c_ref[0, hh, nfull][:, 0:1] for hh in range(2)]
    qpos = i * tq + lax.broadcasted_iota(jnp.int32, (tq, 1), 0)
    den = (HEAD64, 0)
    lane_v = lax.broadcasted_iota(jnp.int32, (tk, LANES), 1)
    keep = (lane_v < HEAD64, lane_v >= HEAD64)
    ones_col = tuple(jnp.where(lane_v == d, 1.0, 0.0).astype(BF16) for d in den)

    def scores(j, s_ref):
        k0 = pl.multiple_of(j * tk, tk)
        k = k_ref[0, pl.ds(k0, tk), :]
        for hh in range(2):
            s = lax.dot_general(qh[hh], k, (((1,), (1,)), ((), ())), preferred_element_type=F32)
            s_ref[hh] = s + (cbase[hh] - c_ref[0, hh, j]) * LOG2E

    def update(j, s_ref, carry, masked):
        k0 = pl.multiple_of(j * tk, tk)
        v = v_ref[0, pl.ds(k0, tk), :]
        vh = tuple(jnp.where(keep[hh], v, ones_col[hh]) for hh in range(2))
        out = []
        for hh in range(2):
            m, acc = carry[hh]
            s = s_ref[hh]
            if masked:
                kpos = j * tk + lax.broadcasted_iota(jnp.int32, (1, tk), 1)
                s = jnp.where(kpos <= qpos, s, -jnp.inf)
            m_new = jnp.maximum(m, jnp.max(s, axis=-1, keepdims=True))
            alpha = jnp.exp2(m - m_new)
            p = jnp.exp2(s - m_new)
            acc = alpha * acc + jnp.dot(p.astype(BF16), vh[hh], preferred_element_type=F32)
            out.append((m_new, acc))
        return tuple(out)

    def pair(jj, carry):
        j = 2 * jj
        scores(j + 1, sb_ref)
        carry = update(j, sa_ref, carry, False)
        scores(j + 2, sa_ref)
        return update(j + 1, sb_ref, carry, False)

    def tail_even(carry):
        return update(nfull, sa_ref, carry, True)

    def tail_odd(carry):
        scores(nfull, sb_ref)
        carry = update(nfull - 1, sa_ref, carry, False)
        return update(nfull, sb_ref, carry, True)

    init = tuple((jnp.full((tq, 1), -jnp.inf, F32), jnp.zeros((tq, LANES), F32)) for _ in range(2))
    scores(0, sa_ref)
    carry = lax.fori_loop(0, nfull // 2, pair, init)
    carry = lax.cond(nfull % 2 == 1, tail_odd, tail_even, carry)
    acc0, acc1 = carry[0][1], carry[1][1]
    o0 = acc0 / acc0[:, den[0]:den[0] + 1]
    o1 = acc1 / acc1[:, den[1]:den[1] + 1]
    o_ref[0] = jnp.where(first, o0, o1).astype(BF16)


def _fox(qkv, c, tq=256, tk=512):
    b, s, _ = qkv.shape
    npair = FOX_HEADS // 2
    kern = functools.partial(_fox_kernel, tq=tq, tk=tk)
    return pl.pallas_call(
        kern,
        grid=(b, npair, s // tq),
        in_specs=[pl.BlockSpec((1, tq, LANES), lambda bi, p, i: (bi, i, p)),
                  pl.BlockSpec((1, s, LANES), lambda bi, p, i: (bi, 0, npair + p)),
                  pl.BlockSpec((1, s, LANES), lambda bi, p, i: (bi, 0, 2 * npair + p)),
                  pl.BlockSpec((1, 2, s // tk, 1, tk), lambda bi, p, i: (bi, p, 0, 0, 0))],
        out_specs=pl.BlockSpec((1, tq, LANES), lambda bi, p, i: (bi, i, p)),
        out_shape=jax.ShapeDtypeStruct((b, s, FOX_WIDTH), BF16),
        scratch_shapes=[pltpu.VMEM((2, tq, tk), F32), pltpu.VMEM((2, tq, tk), F32)],
        compiler_params=_params("parallel", "parallel", "arbitrary"),
        name="fox",
    )(qkv, qkv, qkv, c)


def _ret_kernel(q_ref, k_ref, v_ref, g_ref, cos_ref, sin_ref, dmat_ref, qdec_ref, kdec_ref, cd_ref, bm_ref,
                o_ref, state_ref, *, ts):
    @pl.when(pl.program_id(2) == 0)
    def _():
        state_ref[...] = jnp.zeros_like(state_ref)

    lane = lax.broadcasted_iota(jnp.int32, (1, LANES), 1)
    q_first = (lane % HEAD64) < (HEAD64 // 2)
    v_first = lane < HEAD64
    c_len = RET_CHUNK
    for c in range(ts // c_len):
        rows = slice(c * c_len, (c + 1) * c_len)
        cos = cos_ref[rows, :]
        sin = sin_ref[rows, :]
        q = q_ref[0, rows, :]
        k = k_ref[0, rows, :]
        v = v_ref[0, rows, :]
        qr = q * cos + pltpu.roll(q, LANES // 2, 1) * sin
        kr = k * cos + pltpu.roll(k, LANES // 2, 1) * sin
        q2 = jnp.concatenate([jnp.where(q_first, qr, 0.0), jnp.where(q_first, 0.0, qr)], axis=0)
        s = _mm_nt(q2, kr * (HEAD64 ** -0.5))
        s0 = s[:c_len] * dmat_ref[0]
        s1 = s[c_len:] * dmat_ref[1]
        o = _mm(s0, jnp.where(v_first, v, 0.0)) + _mm(s1, jnp.where(v_first, 0.0, v))
        state = state_ref[...]
        o = o + _mm(qr * qdec_ref[0], state)
        kv = _mm_tn(kr * kdec_ref[0], v)
        state_ref[...] = state * cd_ref[0] + kv * bm_ref[0]
        sq = o * o
        ms0 = jnp.sum(jnp.where(v_first, sq, 0.0), axis=-1, keepdims=True)
        ms1 = jnp.sum(jnp.where(v_first, 0.0, sq), axis=-1, keepdims=True)
        ms = jnp.where(v_first, ms0, ms1) * (1.0 / HEAD64)
        y = o * lax.rsqrt(ms + NORM_EPS) * _silu(g_ref[0, rows, :])
        o_ref[0, rows, :] = y.astype(BF16)


def _ret_tables(s):
    npair = RET_HEADS // 2
    half = HEAD64 // 2
    lane = np.arange(LANES)
    log_g = np.log1p(-np.exp2(-5.0 - np.arange(RET_HEADS, dtype=np.float32))).astype(np.float32)
    idx = np.arange(RET_CHUNK, dtype=np.float32)
    rel = idx[:, None] - idx[None, :]
    dmat = np.where(rel[None] >= 0, np.exp(np.maximum(rel, 0.0)[None] * log_g[:, None, None]), 0.0)
    qdec, kdec, cd, bm = [], [], [], []
    for p in range(npair):
        hq = 2 * p + ((lane % HEAD64) >= half)
        hv = 2 * p + (lane >= HEAD64)
        qdec.append(np.exp((idx[:, None] + 1.0) * log_g[hq][None, :]))
        kdec.append(np.exp((RET_CHUNK - 1 - idx)[:, None] * log_g[hq][None, :]) * HEAD64 ** -0.5)
        cd.append(np.broadcast_to(np.exp(RET_CHUNK * log_g[hq])[:, None], (LANES, LANES)))
        bm.append((hq[:, None] == hv[None, :]).astype(np.float32))
    tabs = [np.stack(a).astype(np.float32) for a in (qdec, kdec, cd, bm)]
    inv = 1.0 / (RET_ANGLE_BASE ** jnp.linspace(0.0, 1.0, half, dtype=F32))
    pos = jnp.arange(s, dtype=F32)
    ang = pos[:, None] * inv[None, :]
    cos = jnp.tile(jnp.cos(ang), (1, LANES // half))
    sin = jnp.tile(jnp.sin(ang), (1, LANES // half))
    sin = jnp.where(jnp.asarray(lane)[None, :] < LANES // 2, -sin, sin)
    return [jnp.asarray(dmat.astype(np.float32))] + [jnp.asarray(a) for a in tabs] + [cos, sin]


def _ret(rest3, tables, ts=1024):
    b, s, _ = rest3.shape
    dmat, qdec, kdec, cd, bm, cos, sin = tables
    npair = RET_HEADS // 2
    col = lambda off: (lambda bi, p, i: (bi, i, off // LANES + p))
    tab = lambda bi, p, i: (p, 0, 0)
    kern = functools.partial(_ret_kernel, ts=ts)
    return pl.pallas_call(
        kern,
        grid=(b, npair, s // ts),
        in_specs=[pl.BlockSpec((1, ts, LANES), col(REST_RQ)),
                  pl.BlockSpec((1, ts, LANES), col(REST_RK)),
                  pl.BlockSpec((1, ts, LANES), col(REST_RV)),
                  pl.BlockSpec((1, ts, LANES), col(REST_RG)),
                  pl.BlockSpec((ts, LANES), lambda bi, p, i: (i, 0)),
                  pl.BlockSpec((ts, LANES), lambda bi, p, i: (i, 0)),
                  pl.BlockSpec((2, RET_CHUNK, RET_CHUNK), tab),
                  pl.BlockSpec((1, RET_CHUNK, LANES), tab),
                  pl.BlockSpec((1, RET_CHUNK, LANES), tab),
                  pl.BlockSpec((1, LANES, LANES), tab),
                  pl.BlockSpec((1, LANES, LANES), tab)],
        out_specs=pl.BlockSpec((1, ts, LANES), lambda bi, p, i: (bi, i, p)),
        out_shape=jax.ShapeDtypeStruct((b, s, RET_WIDTH), BF16),
        scratch_shapes=[pltpu.VMEM((LANES, LANES), F32)],
        compiler_params=_params("parallel", "parallel", "arbitrary"),
        name="retention",
    )(rest3, rest3, rest3, rest3, cos, sin, dmat, qdec, kdec, cd, bm)


def _gdn_prep_kernel(q_ref, k_ref, v_ref, sm_ref, wq_ref, wk_ref, wv_ref, alog_ref, dtb_ref,
                     u0_ref, w_ref, qg_ref, kt_ref, at_ref, eg_ref):
    h = pl.program_id(1)
    c_len = GDN_CHUNK
    n_chunks = q_ref.shape[1] // c_len
    lane = lax.broadcasted_iota(jnp.int32, (1, LANES), 1)
    ri = lax.broadcasted_iota(jnp.int32, (c_len, c_len), 0)
    ci = lax.broadcasted_iota(jnp.int32, (c_len, c_len), 1)
    incl = ri >= ci
    strict = ri > ci
    eye = (ri == ci).astype(F32)
    neg_a = -jnp.exp(alog_ref[...])
    dtb = dtb_ref[...]

    grp = GDN_PREP_GROUP
    rows = grp * c_len

    def conv_silu(ref, w_ref_, n, r0):
        cur = ref[0, pl.ds(r0, rows), :]
        p0 = pl.multiple_of(jnp.maximum(r0 - 8, 0), 8)
        prev = ref[0, pl.ds(p0, 8), :]
        prev = jnp.where(jnp.broadcast_to(n > 0, prev.shape), prev, 0.0)
        xc = jnp.concatenate([prev, cur], axis=0)
        w = w_ref_[...]
        y = cur * w[CONV_WIDTH - 1:CONV_WIDTH, :]
        for j in range(CONV_WIDTH - 1):
            shifted = pltpu.roll(xc, CONV_WIDTH - 1 - j, 0)[8:, :]
            y = y + shifted * w[j:j + 1, :]
        return _silu(y)

    def group(n, carry):
        r0 = pl.multiple_of(n * rows, rows)
        cq = conv_silu(q_ref, wq_ref, n, r0)
        ck = conv_silu(k_ref, wk_ref, n, r0)
        cv = conv_silu(v_ref, wv_ref, n, r0)
        qn = cq * lax.rsqrt(jnp.sum(cq * cq, axis=-1, keepdims=True) + NORM_EPS) * (GDN_DIM ** -0.5)
        kn = ck * lax.rsqrt(jnp.sum(ck * ck, axis=-1, keepdims=True) + NORM_EPS)
        sm = sm_ref[0, pl.ds(r0, rows), :]
        z = sm + dtb
        g_all = neg_a * (jnp.maximum(z, 0.0) + jnp.log1p(jnp.exp(-jnp.abs(z))))
        beta_all = 1.0 / (1.0 + jnp.exp(-sm))
        g_col = jnp.sum(jnp.where(lane == SMALL_GA + h, g_all, 0.0), axis=-1, keepdims=True)
        beta = jnp.sum(jnp.where(lane == SMALL_GB + h, beta_all, 0.0), axis=-1, keepdims=True)
        kb = kn * beta
        vb = cv * beta
        chunks = [slice(g * c_len, (g + 1) * c_len) for g in range(grp)]
        g_row = [jnp.sum(g_col[c] * eye, axis=0, keepdims=True) for c in chunks]
        gc = [jnp.sum(jnp.where(incl, g_row[i], 0.0), axis=-1, keepdims=True) for i in range(grp)]
        gc_row = [jnp.sum(jnp.where(ri <= ci, g_col[c], 0.0), axis=0, keepdims=True) for c in chunks]
        decay = [jnp.where(incl, jnp.exp(jnp.where(incl, gc[i] - gc_row[i], 0.0)), 0.0) for i in range(grp)]
        both = [_mm_nt(jnp.concatenate([kb[c], qn[c]], axis=0), kn[c]) for c in chunks]
        low = [jnp.where(strict, both[i][:c_len] * decay[i], 0.0) for i in range(grp)]
        attn = [jnp.where(incl, both[i][c_len:] * decay[i], 0.0) for i in range(grp)]
        inv = [eye - low[i] for i in range(grp)]
        pw = low
        for _ in range(int(math.log2(c_len)) - 1):
            pw = [_mm(pw[i], pw[i]) for i in range(grp)]
            inv = [inv[i] + _mm(inv[i], pw[i]) for i in range(grp)]
        eg = [jnp.exp(gc[i]) for i in range(grp)]
        sol = [_mm(inv[i], jnp.concatenate([vb[c], kb[c] * eg[i]], axis=1)) for i, c in enumerate(chunks)]
        for i, c in enumerate(chunks):
            dst = pl.ds(pl.multiple_of(r0 + i * c_len, c_len), c_len)
            g_last = gc[i][c_len - 1:c_len, :]
            u0_ref[0, 0, dst, :] = sol[i][:, :GDN_DIM]
            w_ref[0, 0, dst, :] = sol[i][:, GDN_DIM:].astype(BF16)
            qg_ref[0, 0, dst, :] = (qn[c] * eg[i]).astype(BF16)
            kt_ref[0, 0, dst, :] = (kn[c] * jnp.exp(g_last - gc[i])).astype(BF16)
            at_ref[0, 0, dst, :] = attn[i].astype(BF16)
            eg_ref[0, 0, n * grp + i] = jnp.broadcast_to(jnp.exp(g_last), (1, LANES))
        return carry

    lax.fori_loop(0, n_chunks // grp, group, 0)


def _gdn_prep(rest3, conv_w, alog_l, dtb_l):
    b, s, _ = rest3.shape
    nh = GDN_HEADS
    n_chunks = s // GDN_CHUNK
    col = lambda off: (lambda bi, h: (bi, 0, off // LANES + h))
    wcol = lambda g: (lambda bi, h: (0, g * nh + h))
    const = lambda bi, h: (0, 0)
    row =pl.BlockSpec((1, 1, s, GDN_DIM), lambda bi, h: (bi, h, 0, 0))
    return pl.pallas_call(
        _gdn_prep_kernel,
        grid=(b, nh),
        in_specs=[pl.BlockSpec((1, s, LANES), col(REST_GQ)),
                  pl.BlockSpec((1, s, LANES), col(REST_GK)),
                  pl.BlockSpec((1, s, LANES), col(REST_GV)),
                  pl.BlockSpec((1, s, LANES), lambda bi, h: (bi, 0, REST_SMALL // LANES)),
                  pl.BlockSpec((CONV_WIDTH, LANES), wcol(0)),
                  pl.BlockSpec((CONV_WIDTH, LANES), wcol(1)),
                  pl.BlockSpec((CONV_WIDTH, LANES), wcol(2)),
                  pl.BlockSpec((1, LANES), const),
                  pl.BlockSpec((1, LANES), const)],
        out_specs=[row, row, row, row,
                   pl.BlockSpec((1, 1, s, GDN_CHUNK), lambda bi, h: (bi, h, 0, 0)),
                   pl.BlockSpec((1, 1, n_chunks, 1, LANES), lambda bi, h: (bi, h, 0, 0, 0))],
        out_shape=[jax.ShapeDtypeStruct((b, nh, s, GDN_DIM), F32),
                   jax.ShapeDtypeStruct((b, nh, s, GDN_DIM), BF16),
                   jax.ShapeDtypeStruct((b, nh, s, GDN_DIM), BF16),
                   jax.ShapeDtypeStruct((b, nh, s, GDN_DIM), BF16),
                   jax.ShapeDtypeStruct((b, nh, s, GDN_CHUNK), BF16),
                   jax.ShapeDtypeStruct((b, nh, n_chunks, 1, LANES), F32)],
        compiler_params=_params("parallel", "parallel"),
        name="gdn_prep",
    )(rest3, rest3, rest3, rest3, conv_w, conv_w, conv_w, alog_l, dtb_l)


def _gdn_scan_kernel(u0_ref, w_ref, qg_ref, kt_ref, at_ref, eg_ref, z_ref, nw_ref, o_ref, state_ref, *, ts):
    si = pl.program_id(1)

    @pl.when(si == 0)
    def _():
        state_ref[...] = jnp.zeros_like(state_ref)

    c_len = GDN_CHUNK
    per_tile = ts // c_len
    nw = nw_ref[...]
    for c in range(per_tile):
        rows = slice(c * c_len, (c + 1) * c_len)
        heads = range(GDN_HEADS)
        st = [state_ref[h] for h in heads]
        r = [_mm(jnp.concatenate([w_ref[0, h, rows, :], qg_ref[0, h, rows, :]], axis=0), st[h]) for h in heads]
        u = [(u0_ref[0, h, rows, :] - r[h][:c_len]).astype(BF16) for h in heads]
        ku = [_mm_tn(kt_ref[0, h, rows, :], u[h]) for h in heads]
        au = [jnp.dot(at_ref[0, h, rows, :], u[h], preferred_element_type=F32) for h in heads]
        for h in heads:
            state_ref[h] = st[h] * eg_ref[0, h, si * per_tile + c] + ku[h]
        for h in heads:
            cols = slice(h * GDN_DIM, (h + 1) * GDN_DIM)
            y = _rms(r[h][c_len:] + au[h], nw) * _silu(z_ref[0, rows, cols])
            o_ref[0, rows, cols] = y.astype(BF16)


def _gdn_scan(u0, w, qg, kt, at, eg, rest3, norm_w, ts=512):
    b, nh, s, _ = u0.shape
    n_chunks = s // GDN_CHUNK
    blk = lambda d: pl.BlockSpec((1, nh, ts, d), lambda bi, i: (bi, 0, i, 0))
    kern = functools.partial(_gdn_scan_kernel, ts=ts)
    return pl.pallas_call(
        kern,
        grid=(b, s // ts),
        in_specs=[blk(GDN_DIM), blk(GDN_DIM), blk(GDN_DIM), blk(GDN_DIM), blk(GDN_CHUNK),
                  pl.BlockSpec((1, nh, n_chunks, 1, LANES), lambda bi, i: (bi, 0, 0, 0, 0)),
                  pl.BlockSpec((1, ts, GDN_WIDTH), lambda bi, i: (bi, i, REST_GZ // GDN_WIDTH)),
                  pl.BlockSpec((1, GDN_DIM), lambda bi, i: (0, 0))],
        out_specs=pl.BlockSpec((1, ts, GDN_WIDTH), lambda bi, i: (bi, i, 0)),
        out_shape=jax.ShapeDtypeStruct((b, s, GDN_WIDTH), BF16),
        scratch_shapes=[pltpu.VMEM((nh, GDN_DIM, GDN_DIM), F32)],
        compiler_params=_params("parallel", "arbitrary"),
        name="gdn_scan",
    )(u0, w, qg, kt, at, eg, rest3, norm_w)


def _outproj_kernel(x_ref, of_ref, or_ref, og_ref, w_ref, o_ref):
    acc = x_ref[...]
    acc = acc + jnp.dot(of_ref[...], w_ref[0:FOX_WIDTH, :], preferred_element_type=F32)
    acc = acc + jnp.dot(or_ref[...], w_ref[FOX_WIDTH:FOX_WIDTH + RET_WIDTH, :], preferred_element_type=F32)
    acc = acc + jnp.dot(og_ref[...], w_ref[FOX_WIDTH + RET_WIDTH:, :], preferred_element_type=F32)
    o_ref[...] = acc


def _outproj(x, o_fox, o_ret, o_gdn, w_out, tm=512):
    t = x.shape[0]
    d_mix = w_out.shape[0]
    return pl.pallas_call(
        _outproj_kernel,
        grid=(t // tm,),
        in_specs=[pl.BlockSpec((tm, D_MODEL), lambda i: (i, 0)),
                  pl.BlockSpec((tm, FOX_WIDTH), lambda i: (i, 0)),
                  pl.BlockSpec((tm, RET_WIDTH), lambda i: (i, 0)),
                  pl.BlockSpec((tm, GDN_WIDTH), lambda i: (i, 0)),
                  pl.BlockSpec((d_mix, D_MODEL), lambda i: (0, 0))],
        out_specs=pl.BlockSpec((tm, D_MODEL), lambda i: (i, 0)),
        out_shape=jax.ShapeDtypeStruct((t, D_MODEL), F32),
        compiler_params=_params("parallel"),
        name="outproj",
    )(x, o_fox, o_ret, o_gdn, w_out)


def _router_kernel(x_ref, nw_ref, w_ref, b_ref, tri_ref, sel_ref, route_ref, idx_ref, cnt_ref, carry_ref):
    @pl.when(pl.program_id(0) == 0)
    def _():
        carry_ref[...] = jnp.zeros_like(carry_ref)

    tm = x_ref.shape[0]
    hn = _rms(x_ref[...], nw_ref[...])
    logits = jnp.dot(hn, w_ref[...], precision=HIGHEST, preferred_element_type=F32) + b_ref[...]
    lane = lax.broadcasted_iota(jnp.int32, logits.shape, 1).astype(F32)
    neg = -jnp.inf
    gl = jnp.where(lane < ROUTER_EXP, logits, neg)
    gmax = jnp.max(gl, axis=-1, keepdims=True)
    gidx = jnp.min(jnp.where(gl == gmax, lane, LANES), axis=-1, keepdims=True)
    grp_p = 1.0 / jnp.sum(jnp.exp(gl - gmax), axis=-1, keepdims=True)
    lo = ROUTER_EXP + gidx * EXPERTS_PER_GROUP
    el = jnp.where((lane >= lo) & (lane < lo + EXPERTS_PER_GROUP), logits, neg)
    m1 = jnp.max(el, axis=-1, keepdims=True)
    i1 = jnp.min(jnp.where(el == m1, lane, LANES), axis=-1, keepdims=True)
    el2 = jnp.where(lane == i1, neg, el)
    m2 = jnp.max(el2, axis=-1, keepdims=True)
    i2 = jnp.min(jnp.where(el2 == m2, lane, LANES), axis=-1, keepdims=True)
    e2 = jnp.exp(m2 - m1)
    w1 = grp_p / (1.0 + e2)
    ea = jnp.minimum(i1, i2) - ROUTER_EXP
    eb = jnp.maximum(i1, i2) - ROUTER_EXP
    first_low = i1 < i2
    ga = jnp.where(first_low, w1, w1 * e2)
    gb = jnp.where(first_low, w1 * e2, w1)
    la = ea - gidx * EXPERTS_PER_GROUP
    lb = eb - gidx * EXPERTS_PER_GROUP
    cls = gidx * N_PAIRS + la * (2 * EXPERTS_PER_GROUP - 1 - la) * 0.5 + (lb - la - 1.0)
    oh = (lane == cls).astype(F32)
    cum = jnp.dot(tri_ref[...], oh.astype(BF16), preferred_element_type=F32) + carry_ref[...]
    rank = jnp.sum(oh * (cum - oh), axis=-1, keepdims=True)
    carry_ref[...] = cum[tm - 1:tm, :]
    cnt_ref[...] = cum[tm - 1:tm, :]
    out = jnp.zeros_like(logits)
    for col, val in ((ROUTE_CLASS, cls), (ROUTE_RANK, rank), (ROUTE_GATE, ga), (ROUTE_GATE + 1, gb)):
        out = jnp.where(lane == col, val, out)
    route_ref[...] = out
    idx = lax.dot_general(sel_ref[...], out, (((1,), (1,)), ((), ())), precision=HIGHEST,
                          preferred_element_type=F32)
    idx_ref[...] = idx.astype(jnp.int32)


def _router(x, nw, w_pack, b_pack, tm=512):
    t = x.shape[0]
    tri = jnp.asarray((np.arange(tm)[:, None] >= np.arange(tm)[None, :]).astype(np.float32), dtype=BF16)
    sel = np.zeros((8, LANES), np.float32)
    for row, lane in enumerate((ROUTE_CLASS, ROUTE_RANK)):
        sel[row, lane] = 1.0
    return pl.pallas_call(
        _router_kernel,
        grid=(t // tm,),
        in_specs=[pl.BlockSpec((tm, D_MODEL), lambda i: (i, 0)),
                  pl.BlockSpec((1, D_MODEL), lambda i: (0, 0)),
                  pl.BlockSpec((D_MODEL, LANES), lambda i: (0, 0)),
                  pl.BlockSpec((1, LANES), lambda i: (0, 0)),
                  pl.BlockSpec((tm, tm), lambda i: (0, 0)),
                  pl.BlockSpec((8, LANES), lambda i: (0, 0))],
        out_specs=[pl.BlockSpec((tm, LANES), lambda i: (i, 0)),
                   pl.BlockSpec((8, tm), lambda i: (0, i)),
                   pl.BlockSpec((1, LANES), lambda i: (0, 0))],
        out_shape=[jax.ShapeDtypeStruct((t, LANES), F32),
                   jax.ShapeDtypeStruct((8, t), jnp.int32),
                   jax.ShapeDtypeStruct((1, LANES), F32)],
        scratch_shapes=[pltpu.VMEM((1, LANES), F32)],
        compiler_params=_params("arbitrary"),
        name="router",
    )(x, nw, w_pack, b_pack, tri, jnp.asarray(sel))


def _round_up_tile(v):
    shift = MOE_TILE.bit_length() - 1
    return lax.shift_left(lax.shift_right_logical(v + (MOE_TILE - 1), shift), shift)


def _dispatch_kernel(off_ref, cnt_ref, tot_ref, pos_ref, x_ref, route_ref, xs_ref, ring_ref, zero_ref, sem_ref, *,
                     tm, n_tiles):
    i = pl.program_id(0)
    n = pl.num_programs(0)
    slot = i % 2

    def row_copy(s, r, p):
        return pltpu.make_async_copy(ring_ref.at[s, pl.ds(r, 1)], xs_ref.at[pl.ds(p, 1)], sem_ref.at[s])

    def pad_row(p):
        return pltpu.make_async_copy(zero_ref.at[pl.ds(0, 1)], xs_ref.at[pl.ds(p, 1)], sem_ref.at[2])

    def pad_tile(j):
        return pltpu.make_async_copy(zero_ref, xs_ref.at[pl.ds(j * MOE_TILE, MOE_TILE)], sem_ref.at[3])

    @pl.when(i == 0)
    def _():
        zero_ref[...] = jnp.zeros_like(zero_ref)

        def start_rows(c, carry):
            def one(r, cc):
                pad_row(off_ref[c] + r).start()
                return cc

            return lax.fori_loop(cnt_ref[c], _round_up_tile(cnt_ref[c]), one, carry)

        def wait_rows(c, carry):
            def one(r, cc):
                pad_row(0).wait()
                return cc

            return lax.fori_loop(cnt_ref[c], _round_up_tile(cnt_ref[c]), one, carry)

        def start_tile(j, c):
            pad_tile(j).start()
            return c

        def wait_tile(j, c):
            pad_tile(0).wait()
            return c

        lax.fori_loop(0, N_CLASSES, start_rows, 0)
        lax.fori_loop(tot_ref[0], n_tiles, start_tile, 0)
        lax.fori_loop(0, N_CLASSES, wait_rows, 0)
        lax.fori_loop(tot_ref[0], n_tiles, wait_tile, 0)

    ring_ref[slot, :, :D_MODEL] = x_ref[...]
    ring_ref[slot, :, D_MODEL:] = route_ref[...]

    def issue(r, c):
        row_copy(slot, r, pos_ref[0, r]).start()
        return c

    lax.fori_loop(0, tm, issue, 0, unroll=8)

    def drain(s):
        def wait(r, c):
            row_copy(s, 0, 0).wait()
            return c

        lax.fori_loop(0, tm, wait, 0, unroll=8)

    @pl.when(i > 0)
    def _():
        drain(1 - slot)

    @pl.when(i == n - 1)
    def _():
        drain(slot)


def _dispatch(x, route, off, cnt, total, pos, tm, n_tiles):
    t = x.shape[0]
    kern = functools.partial(_dispatch_kernel, tm=tm, n_tiles=n_tiles)
    return pl.pallas_call(
        kern,
        grid_spec=pltpu.PrefetchScalarGridSpec(
            num_scalar_prefetch=3,
            grid=(t // tm,),
            in_specs=[pl.BlockSpec((1, tm), lambda i, *_: (0, i), memory_space=pltpu.SMEM),
                      pl.BlockSpec((tm, D_MODEL), lambda i, *_: (i, 0)),
                      pl.BlockSpec((tm, LANES), lambda i, *_: (i, 0))],
            out_specs=pl.BlockSpec(memory_space=pl.ANY),
            scratch_shapes=[pltpu.VMEM((2, tm, ROW_WIDTH), F32),
                            pltpu.VMEM((MOE_TILE, ROW_WIDTH), F32),
                            pltpu.SemaphoreType.DMA((4,))]),
        out_shape=jax.ShapeDtypeStruct((n_tiles * MOE_TILE, ROW_WIDTH), F32),
        compiler_params=_params("arbitrary"),
        name="dispatch",
    )(off, cnt, total, pos, x, route)


def _ffn_kernel(ta_ref, tb_ref, fa_ref, fb_ref, tot_ref, xs_ref, nw_ref, w1a_ref, w3a_ref, w2a_ref,
                w1b_ref, w3b_ref, w2b_ref, ys_ref, w1a, w3a, w2a, w1b, w3b, w2b):
    j = pl.program_id(0)

    @pl.when(fa_ref[j] == 1)
    def _():
        w1a[...] = w1a_ref[0].astype(BF16)
        w3a[...] = w3a_ref[0].astype(BF16)
        w2a[...] = w2a_ref[0].astype(BF16)

    @pl.when(fb_ref[j] == 1)
    def _():
        w1b[...] = w1b_ref[0].astype(BF16)
        w3b[...] = w3b_ref[0].astype(BF16)
        w2b[...] = w2b_ref[0].astype(BF16)

    @pl.when(j < tot_ref[0])
    def _():
        x = xs_ref[:, :D_MODEL]
        ga = xs_ref[:, D_MODEL + ROUTE_GATE:D_MODEL + ROUTE_GATE + 1]
        gb = xs_ref[:, D_MODEL + ROUTE_GATE + 1:D_MODEL + ROUTE_GATE + 2]
        hn = _rms(x, nw_ref[...]).astype(BF16)

        def expert(w1, w3, w2):
            a = jnp.dot(hn, w1[...], preferred_element_type=F32)
            u = jnp.dot(hn, w3[...], preferred_element_type=F32)
            return jnp.dot((_silu(a) * u).astype(BF16), w2[...], preferred_element_type=F32)

        ys_ref[...] = x + ga * expert(w1a, w3a, w2a) + gb * expert(w1b, w3b, w2b)

    @pl.when(j >= tot_ref[0])
    def _():
        ys_ref[...] = jnp.zeros_like(ys_ref)


def _ffn(xs, nw, w1, w3, w2, tile_a, tile_b, first_a, first_b, total):
    n_tiles = tile_a.shape[0]
    tok = lambda j, *_: (j, 0)
    wa = lambda j, ta, tb, fa, fb, tot: (ta[j], 0, 0)
    wb = lambda j, ta, tb, fa, fb, tot: (tb[j], 0, 0)
    up = lambda imap: pl.BlockSpec((1, D_MODEL, EXPERT_FF), imap)
    down = lambda imap: pl.BlockSpec((1, EXPERT_FF, D_MODEL), imap)
    return pl.pallas_call(
        _ffn_kernel,
        grid_spec=pltpu.PrefetchScalarGridSpec(
            num_scalar_prefetch=5,
            grid=(n_tiles,),
            in_specs=[pl.BlockSpec((MOE_TILE, ROW_WIDTH), tok),
                      pl.BlockSpec((1, D_MODEL), lambda j, *_: (0, 0)),
                      up(wa), up(wa), down(wa), up(wb), up(wb), down(wb)],
            out_specs=pl.BlockSpec((MOE_TILE, D_MODEL), tok),
            scratch_shapes=[pltpu.VMEM((D_MODEL, EXPERT_FF), BF16),
                            pltpu.VMEM((D_MODEL, EXPERT_FF), BF16),
                            pltpu.VMEM((EXPERT_FF, D_MODEL), BF16),
                            pltpu.VMEM((D_MODEL, EXPERT_FF), BF16),
                            pltpu.VMEM((D_MODEL, EXPERT_FF), BF16),
                            pltpu.VMEM((EXPERT_FF, D_MODEL), BF16)]),
        out_shape=jax.ShapeDtypeStruct((n_tiles * MOE_TILE, D_MODEL), F32),
        compiler_params=_params("arbitrary"),
        name="ffn",
    )(tile_a, tile_b, first_a, first_b, total, xs, nw, w1, w3, w2, w1, w3, w2)


def _gather_kernel(pos_ref, posn_ref, fw_ref, ys_ref, o_ref, ybuf, sem_ref, *, tm, final_norm):
    i = pl.program_id(0)
    n = pl.num_programs(0)
    slot = i % 2

    def row_copy(s, r, p):
        return pltpu.make_async_copy(ys_ref.at[pl.ds(p, 1)], ybuf.at[s, pl.ds(r, 1)], sem_ref.at[s])

    def issue(p_ref, s):
        def body(r, c):
            row_copy(s, r, p_ref[0, r]).start()
            return c

        lax.fori_loop(0, tm, body, 0, unroll=8)

    @pl.when(i == 0)
    def _():
        issue(pos_ref, 0)

    @pl.when(i + 1 < n)
    def _():
        issue(posn_ref, 1 - slot)

    def wait(r, c):
        row_copy(slot, 0, 0).wait()
        return c

    lax.fori_loop(0, tm, wait, 0, unroll=8)
    out = ybuf[slot]
    if final_norm:
        out = _rms(out, fw_ref[...])
    o_ref[...] = out


def _gather(ys, pos, fw, final_norm, tm):
    t = pos.shape[1]
    n = t // tm
    kern = functools.partial(_gather_kernel, tm=tm, final_norm=final_norm)
    smem = lambda imap: pl.BlockSpec((1, tm), imap, memory_space=pltpu.SMEM)
    return pl.pallas_call(
        kern,
        grid=(n,),
        in_specs=[smem(lambda i: (0, i)),
                  smem(lambda i: (0, jnp.minimum(i + 1, n - 1))),
                  pl.BlockSpec((1, D_MODEL), lambda i: (0, 0)),
                  pl.BlockSpec(memory_space=pl.ANY)],
        out_specs=pl.BlockSpec((tm, D_MODEL), lambda i: (i, 0)),
        out_shape=jax.ShapeDtypeStruct((t, D_MODEL), F32),
        scratch_shapes=[pltpu.VMEM((2, tm, D_MODEL), F32),
                        pltpu.SemaphoreType.DMA((2,))],
        compiler_params=_params("arbitrary"),
        name="gather",
    )(pos, pos, fw, ys)


def _class_experts():
    lo, hi = [], []
    for g in range(N_GROUPS):
        for la in range(EXPERTS_PER_GROUP):
            for lb in range(la + 1, EXPERTS_PER_GROUP):
                lo.append(g * EXPERTS_PER_GROUP + la)
                hi.append(g * EXPERTS_PER_GROUP + lb)
    return np.asarray(lo, np.int32), np.asarray(hi, np.int32)


def _lookup(table, index):
    k = table.shape[0]
    return jnp.sum(jnp.where(index[..., None] == jnp.arange(k, dtype=jnp.int32), table, 0), axis=-1)


def _moe(x, nw, w_pack, b_pack, w1, w3, w2, fw, final_norm, tm=256):
    t = x.shape[0]
    route, idx, cnt_row = _router(x, nw, w_pack, b_pack)
    cnt = cnt_row[0, :N_CLASSES].astype(jnp.int32)
    n_tiles = t // MOE_TILE + N_CLASSES
    nblk = (cnt + MOE_TILE - 1) // MOE_TILE
    cend = jnp.cumsum(nblk)
    total = cend[-1:]
    off = (cend - nblk) * MOE_TILE
    pos = (_lookup(off, idx[ROUTE_CLASS]) + idx[ROUTE_RANK]).reshape(1, t)
    j = jnp.arange(n_tiles, dtype=jnp.int32)
    tile_c = jnp.minimum(jnp.sum((j[:, None] >= cend[None, :]).astype(jnp.int32), axis=1), N_CLASSES - 1)
    lo, hi = _class_experts()
    tile_a = _lookup(jnp.asarray(lo), tile_c)
    tile_b = _lookup(jnp.asarray(hi), tile_c)
    valid = j < total
    changed = lambda e: (valid & ((j == 0) | (e != jnp.roll(e, 1)))).astype(jnp.int32)
    xs = _dispatch(x, route, off, cnt, total, pos, tm, n_tiles)
    ys = _ffn(xs, nw, w1, w3, w2, tile_a, tile_b, changed(tile_a), changed(tile_b), total)
    return _gather(ys, pos, fw, final_norm, tm)


def _pack_in_weights(w_in_l):
    off = np.concatenate([[0], np.cumsum(IN_SPLITS)]).tolist()
    fq, fk, fv, ff, rq, rk, rv, rg, gq, gk, gv, gz, ga, gb = [(off[i], IN_SPLITS[i]) for i in range(len(IN_SPLITS))]
    half = HEAD64 // 2

    def permuted(seg):
        return [(seg[0] + (2 * p + hh) * HEAD64 + lo * half, half)
                for p in range(RET_HEADS // 2) for lo in range(2) for hh in range(2)]

    col_scale = np.ones((1, w_in_l.shape[1]), np.float32)
    col_scale[:, fq[0]:fq[0] + fq[1]] = LOG2E * HEAD64 ** -0.5
    w_bf = (w_in_l * col_scale).astype(BF16)
    cols = lambda segs: [w_bf[:, a:a + n] for a, n in segs]
    wf = jnp.concatenate(cols([fq, fk, fv]), axis=1)
    pad = jnp.zeros((D_MODEL, LANES - (FOX_HEADS + 2 * GDN_HEADS)), BF16)
    wr = jnp.concatenate(cols(permuted(rq) + permuted(rk) + [rv, rg, gq, gk, gv, gz, ff, ga, gb]) + [pad], axis=1)
    return wf, wr


def _lane_row(vals, offset):
    return jnp.zeros((1, LANES), F32).at[0, offset:offset + vals.shape[0]].set(vals.astype(F32))


def kernel(x, norm1_w, w_in, fox_forget_bias, gdn_conv_w, gdn_a_log, gdn_dt_bias, gdn_norm_w, w_out, norm2_w,
           router_group_w, router_group_b, router_expert_w, router_expert_b, expert_w1, expert_w3, expert_w2,
           final_norm_w):
    b, s, d = x.shape
    t = b * s
    depth = w_in.shape[0]
    fox_tk = 512
    xt = x.reshape(t, d)
    ret_tables = _ret_tables(s)
    for l in range(depth):
        wf, wr = _pack_in_weights(w_in[l])
        qkv, rest = _inproj(xt, norm1_w[l].reshape(1, d), wf, wr)
        rest3 = rest.reshape(b, s, REST_WIDTH)
        c = _fgate(rest3, _lane_row(fox_forget_bias[l], SMALL_FF)).reshape(b, FOX_HEADS, s // fox_tk, 1, fox_tk)
        o_fox = _fox(qkv.reshape(b, s, 3 * FOX_WIDTH), c, tk=fox_tk)
        o_ret = _ret(rest3, ret_tables)
        prep = _gdn_prep(rest3, gdn_conv_w[l].astype(F32), _lane_row(gdn_a_log[l], SMALL_GA),
                         _lane_row(gdn_dt_bias[l], SMALL_GA))
        o_gdn = _gdn_scan(*prep, rest3, gdn_norm_w[l].reshape(1, GDN_DIM).astype(F32))
        xt = _outproj(xt, o_fox.reshape(t, FOX_WIDTH), o_ret.reshape(t, RET_WIDTH),
                      o_gdn.reshape(t, GDN_WIDTH), w_out[l].astype(BF16))
        w_pack = jnp.concatenate([router_group_w[l], router_expert_w[l],
                                  jnp.zeros((d, LANES - N_GROUPS - N_EXPERTS), F32)], axis=1)
        b_pack = jnp.concatenate([router_group_b[l].reshape(-1), router_expert_b[l].reshape(-1),
                                  jnp.zeros((LANES - N_GROUPS - N_EXPERTS,), F32)]).reshape(1, LANES)
        xt = _moe(xt, norm2_w[l].reshape(1, d), w_pack, b_pack,
                  expert_w1[l].reshape(N_EXPERTS, d, EXPERT_FF),
                  expert_w3[l].reshape(N_EXPERTS, d, EXPERT_FF),
                  expert_w2[l].reshape(N_EXPERTS, EXPERT_FF, d),
                  final_norm_w.reshape(1, d), final_norm=(l == depth - 1))
    return xt.reshape(b, s, d)
```

```python
import functools
import math

import jax
import jax.numpy as jnp
import numpy as np
from jax import lax
from jax.experimental import pallas as pl
from jax.experimental.pallas import tpu as pltpu

F32 = jnp.float32
BF16 = jnp.bfloat16
HIGHEST = lax.Precision.HIGHEST

D_MODEL = 1024
FOX_HEADS = 4
RET_HEADS = 4
GDN_HEADS = 4
HEAD64 = 64
GDN_DIM = 128
FOX_WIDTH = FOX_HEADS * HEAD64
RET_WIDTH = RET_HEADS * HEAD64
GDN_WIDTH = GDN_HEADS * GDN_DIM
RET_CHUNK = 128
GDN_CHUNK = 64
GDN_PREP_GROUP = 8
CONV_WIDTH = 4
RET_ANGLE_BASE = 10000.0
N_GROUPS = 4
EXPERTS_PER_GROUP = 8
N_EXPERTS = N_GROUPS * EXPERTS_PER_GROUP
EXPERT_FF = 256
NORM_EPS = 1e-6
LOG2E = math.log2(math.e)
LANES = 128
IN_SPLITS = (FOX_WIDTH, FOX_WIDTH, FOX_WIDTH, FOX_HEADS,
             RET_WIDTH, RET_WIDTH, RET_WIDTH, RET_WIDTH,
             GDN_WIDTH, GDN_WIDTH, GDN_WIDTH, GDN_WIDTH, GDN_HEADS, GDN_HEADS)

REST_RQ, REST_RK, REST_RV, REST_RG = 0, 256, 512, 768
REST_GQ, REST_GK, REST_GV, REST_GZ = 1024, 1536, 2048, 2560
REST_SMALL = 3072
REST_WIDTH = 3200
SMALL_FF, SMALL_GA, SMALL_GB = 0, 4, 8
ROUTER_GRP, ROUTER_EXP = 0, 4
ROUTE_CLASS, ROUTE_RANK, ROUTE_GATE = 0, 1, 2
N_PAIRS = EXPERTS_PER_GROUP * (EXPERTS_PER_GROUP - 1) // 2
N_CLASSES = N_GROUPS * N_PAIRS
MOE_TILE = 256
ROW_WIDTH = D_MODEL + LANES

VMEM_LIMIT = 56 * 1024 * 1024


def _params(*sem):
    return pltpu.CompilerParams(dimension_semantics=sem, vmem_limit_bytes=VMEM_LIMIT)


def _mm(a, b):
    return jnp.dot(a.astype(BF16), b.astype(BF16), preferred_element_type=F32)


def _mm_nt(a, b):
    return lax.dot_general(a.astype(BF16), b.astype(BF16), (((1,), (1,)), ((), ())),
                           preferred_element_type=F32)


def _mm_tn(a, b):
    return lax.dot_general(a.astype(BF16), b.astype(BF16), (((0,), (0,)), ((), ())),
                           preferred_element_type=F32)


def _silu(x):
    return x * (1.0 / (1.0 + jnp.exp(-x)))


def _rms(x, w):
    return x * lax.rsqrt(jnp.mean(x * x, axis=-1, keepdims=True) + NORM_EPS) * w


def _inproj_kernel(x_ref, nw_ref, wf_ref, wr_ref, of_ref, or_ref):
    hn = _rms(x_ref[...], nw_ref[...]).astype(BF16)
    of_ref[...] = jnp.dot(hn, wf_ref[...], preferred_element_type=F32).astype(BF16)
    step = 640
    for c in range(0, REST_WIDTH, step):
        or_ref[:, c:c + step] = jnp.dot(hn, wr_ref[:, c:c + step], preferred_element_type=F32)


def _inproj(x, nw, wf, wr, tm=512):
    t = x.shape[0]
    return pl.pallas_call(
        _inproj_kernel,
        grid=(t // tm,),
        in_specs=[pl.BlockSpec((tm, D_MODEL), lambda i: (i, 0)),
                  pl.BlockSpec((1, D_MODEL), lambda i: (0, 0)),
                  pl.BlockSpec((D_MODEL, 3 * FOX_WIDTH), lambda i: (0, 0)),
                  pl.BlockSpec((D_MODEL, REST_WIDTH), lambda i: (0, 0))],
        out_specs=[pl.BlockSpec((tm, 3 * FOX_WIDTH), lambda i: (i, 0)),
                   pl.BlockSpec((tm, REST_WIDTH), lambda i: (i, 0))],
        out_shape=[jax.ShapeDtypeStruct((t, 3 * FOX_WIDTH), BF16),
                   jax.ShapeDtypeStruct((t, REST_WIDTH), F32)],
        compiler_params=_params("parallel"),
        name="inproj",
    )(x, nw, wf, wr)


def _fgate_kernel(sm_ref, bias_ref, sel_ref, tri_ref, c_ref):
    n_blk = sm_ref.shape[1] // LANES
    z = sm_ref[0] + bias_ref[...]
    lf = jnp.minimum(z, 0.0) - jnp.log1p(jnp.exp(-jnp.abs(z)))
    sel = sel_ref[...]
    tri = tri_ref[...]
    within = []
    for j in range(n_blk):
        blk = lf[j * LANES:(j + 1) * LANES, :]
        x = lax.dot_general(sel, blk, (((1,), (1,)), ((), ())), precision=HIGHEST, preferred_element_type=F32)
        within.append(jnp.dot(x, tri, precision=HIGHEST, preferred_element_type=F32))
    carry = jnp.zeros((sel.shape[0], 1), F32)
    for j in range(n_blk):
        cj = within[j] + carry
        for h in range(FOX_HEADS):
            c_ref[0, h, :, j * LANES:(j + 1) * LANES] = cj[h:h + 1, :]
        carry = cj[:, LANES - 1:LANES]


def _fgate(rest3, bias_row):
    b, s, _ = rest3.shape
    tri = (np.arange(LANES)[:, None] <= np.arange(LANES)[None, :]).astype(np.float32)
    sel = (np.arange(8)[:, None] == np.arange(LANES)[None, :]).astype(np.float32)
    sel[FOX_HEADS:] = 0.0
    return pl.pallas_call(
        _fgate_kernel,
        grid=(b,),
        in_specs=[pl.BlockSpec((1, s, LANES), lambda i: (i, 0, REST_SMALL // LANES)),
                  pl.BlockSpec((1, LANES), lambda i: (0, 0)),
                  pl.BlockSpec((8, LANES), lambda i: (0, 0)),
                  pl.BlockSpec((LANES, LANES), lambda i: (0, 0))],
        out_specs=pl.BlockSpec((1, FOX_HEADS, 1, s), lambda i: (i, 0, 0, 0)),
        out_shape=jax.ShapeDtypeStruct((b, FOX_HEADS, 1, s), F32),
        compiler_params=_params("parallel"),
        name="fgate",
    )(rest3, bias_row, jnp.asarray(sel), jnp.asarray(tri))


def _fox_kernel(q_ref, k_ref, v_ref, c_ref, o_ref, sa_ref, sb_ref, *, tq, tk):
    i = pl.program_id(2)
    lane = lax.broadcasted_iota(jnp.int32, (1, LANES), 1)
    first = lane < HEAD64
    q = q_ref[0]
    zero = jnp.zeros_like(q)
    qh = (jnp.where(first, q, zero), jnp.where(first, zero, q))
    nfull = (i * tq) // tk
    cbase = [c_ref[0, hh, nfull][:, 0:1] for hh in range(2)]
    qpos = i * tq + lax.broadcasted_iota(jnp.int32, (tq, 1), 0)
    den = (HEAD64, 0)
    lane_v = lax.broadcasted_iota(jnp.int32, (tk, LANES), 1)
    keep = (lane_v < HEAD64, lane_v >= HEAD64)
    ones_col = tuple(jnp.where(lane_v == d, 1.0, 0.0).astype(BF16) for d in den)

    def scores(j, s_ref):
        k0 = pl.multiple_of(j * tk, tk)
        k = k_ref[0, pl.ds(k0, tk), :]
        for hh in range(2):
            s = lax.dot_general(qh[hh], k, (((1,), (1,)), ((), ())), preferred_element_type=F32)
            s_ref[hh] = s + (cbase[hh] - c_ref[0, hh, j]) * LOG2E

    def update(j, s_ref, carry, masked):
        k0 = pl.multiple_of(j * tk, tk)
        v = v_ref[0, pl.ds(k0, tk), :]
        vh = tuple(jnp.where(keep[hh], v, ones_col[hh]) for hh in range(2))
        out = []
        for hh in range(2):
            m, acc = carry[hh]
            s = s_ref[hh]
            if masked:
                kpos = j * tk + lax.broadcasted_iota(jnp.int32, (1, tk), 1)
                s = jnp.where(kpos <= qpos, s, -jnp.inf)
            m_new = jnp.maximum(m, jnp.max(s, axis=-1, keepdims=True))
            alpha = jnp.exp2(m - m_new)
            p = jnp.exp2(s - m_new)
            acc = alpha * acc + jnp.dot(p.astype(BF16), vh[hh], preferred_element_type=F32)
            out.append((m_new, acc))
        return tuple(out)

    def pair(jj, carry):
        j = 2 * jj
        scores(j + 1, sb_ref)
        carry = update(j, sa_ref, carry, False)
        scores(j + 2, sa_ref)
        return update(j + 1, sb_ref, carry, False)

    def tail_even(carry):
        return update(nfull, sa_ref, carry, True)

    def tail_odd(carry):
        scores(nfull, sb_ref)
        carry = update(nfull - 1, sa_ref, carry, False)
        return update(nfull, sb_ref, carry, True)

    init = tuple((jnp.full((tq, 1), -jnp.inf, F32), jnp.zeros((tq, LANES), F32)) for _ in range(2))
    scores(0, sa_ref)
    carry = lax.fori_loop(0, nfull // 2, pair, init)
    carry = lax.cond(nfull % 2 == 1, tail_odd, tail_even, carry)
    acc0, acc1 = carry[0][1], carry[1][1]
    o0 = acc0 / acc0[:, den[0]:den[0] + 1]
    o1 = acc1 / acc1[:, den[1]:den[1] + 1]
    o_ref[0] = jnp.where(first, o0, o1).astype(BF16)


def _fox(qkv, c, tq=256, tk=512):
    b, s, _ = qkv.shape
    npair = FOX_HEADS // 2
    kern = functools.partial(_fox_kernel, tq=tq, tk=tk)
    return pl.pallas_call(
        kern,
        grid=(b, npair, s // tq),
        in_specs=[pl.BlockSpec((1, tq, LANES), lambda bi, p, i: (bi, i, p)),
                  pl.BlockSpec((1, s, LANES), lambda bi, p, i: (bi, 0, npair + p)),
                  pl.BlockSpec((1, s, LANES), lambda bi, p, i: (bi, 0, 2 * npair + p)),
                  pl.BlockSpec((1, 2, s // tk, 1, tk), lambda bi, p, i: (bi, p, 0, 0, 0))],
        out_specs=pl.BlockSpec((1, tq, LANES), lambda bi, p, i: (bi, i, p)),
        out_shape=jax.ShapeDtypeStruct((b, s, FOX_WIDTH), BF16),
        scratch_shapes=[pltpu.VMEM((2, tq, tk), F32), pltpu.VMEM((2, tq, tk), F32)],
        compiler_params=_params("parallel", "parallel", "arbitrary"),
        name="fox",
    )(qkv, qkv, qkv, c)


def _ret_kernel(q_ref, k_ref, v_ref, g_ref, cos_ref, sin_ref, dmat_ref, qdec_ref, kdec_ref, cd_ref, bm_ref,
                o_ref, state_ref, *, ts):
    @pl.when(pl.program_id(2) == 0)
    def _():
        state_ref[...] = jnp.zeros_like(state_ref)

    lane = lax.broadcasted_iota(jnp.int32, (1, LANES), 1)
    q_first = (lane % HEAD64) < (HEAD64 // 2)
    v_first = lane < HEAD64
    c_len = RET_CHUNK
    for c in range(ts // c_len):
        rows = slice(c * c_len, (c + 1) * c_len)
        cos = cos_ref[rows, :]
        sin = sin_ref[rows, :]
        q = q_ref[0, rows, :]
        k = k_ref[0, rows, :]
        v = v_ref[0, rows, :]
        qr = q * cos + pltpu.roll(q, LANES // 2, 1) * sin
        kr = k * cos + pltpu.roll(k, LANES // 2, 1) * sin
        q2 = jnp.concatenate([jnp.where(q_first, qr, 0.0), jnp.where(q_first, 0.0, qr)], axis=0)
        s = _mm_nt(q2, kr * (HEAD64 ** -0.5))
        s0 = s[:c_len] * dmat_ref[0]
        s1 = s[c_len:] * dmat_ref[1]
        o = _mm(s0, jnp.where(v_first, v, 0.0)) + _mm(s1, jnp.where(v_first, 0.0, v))
        state = state_ref[...]
        o = o + _mm(qr * qdec_ref[0], state)
        kv = _mm_tn(kr * kdec_ref[0], v)
        state_ref[...] = state * cd_ref[0] + kv * bm_ref[0]
        sq = o * o
        ms0 = jnp.sum(jnp.where(v_first, sq, 0.0), axis=-1, keepdims=True)
        ms1 = jnp.sum(jnp.where(v_first, 0.0, sq), axis=-1, keepdims=True)
        ms = jnp.where(v_first, ms0, ms1) * (1.0 / HEAD64)
        y = o * lax.rsqrt(ms + NORM_EPS) * _silu(g_ref[0, rows, :])
        o_ref[0, rows, :] = y.astype(BF16)


def _ret_tables(s):
    npair = RET_HEADS // 2
    half = HEAD64 // 2
    lane = np.arange(LANES)
    log_g = np.log1p(-np.exp2(-5.0 - np.arange(RET_HEADS, dtype=np.float32))).astype(np.float32)
    idx = np.arange(RET_CHUNK, dtype=np.float32)
    rel = idx[:, None] - idx[None, :]
    dmat = np.where(rel[None] >= 0, np.exp(np.maximum(rel, 0.0)[None] * log_g[:, None, None]), 0.0)
    qdec, kdec, cd, bm = [], [], [], []
    for p in range(npair):
        hq = 2 * p + ((lane % HEAD64) >= half)
        hv = 2 * p + (lane >= HEAD64)
        qdec.append(np.exp((idx[:, None] + 1.0) * log_g[hq][None, :]))
        kdec.append(np.exp((RET_CHUNK - 1 - idx)[:, None] * log_g[hq][None, :]) * HEAD64 ** -0.5)
        cd.append(np.broadcast_to(np.exp(RET_CHUNK * log_g[hq])[:, None], (LANES, LANES)))
        bm.append((hq[:, None] == hv[None, :]).astype(np.float32))
    tabs = [np.stack(a).astype(np.float32) for a in (qdec, kdec, cd, bm)]
    inv = 1.0 / (RET_ANGLE_BASE ** jnp.linspace(0.0, 1.0, half, dtype=F32))
    pos = jnp.arange(s, dtype=F32)
    ang = pos[:, None] * inv[None, :]
    cos = jnp.tile(jnp.cos(ang), (1, LANES // half))
    sin = jnp.tile(jnp.sin(ang), (1, LANES // half))
    sin = jnp.where(jnp.asarray(lane)[None, :] < LANES // 2, -sin, sin)
    return [jnp.asarray(dmat.astype(np.float32))] + [jnp.asarray(a) for a in tabs] + [cos, sin]


def _ret(rest3, tables, ts=1024):
    b, s, _ = rest3.shape
    dmat, qdec, kdec, cd, bm, cos, sin = tables
    npair = RET_HEADS // 2
    col = lambda off: (lambda bi, p, i: (bi, i, off // LANES + p))
    tab = lambda bi, p, i: (p, 0, 0)
    kern = functools.partial(_ret_kernel, ts=ts)
    return pl.pallas_call(
        kern,
        grid=(b, npair, s // ts),
        in_specs=[pl.BlockSpec((1, ts, LANES), col(REST_RQ)),
                  pl.BlockSpec((1, ts, LANES), col(REST_RK)),
                  pl.BlockSpec((1, ts, LANES), col(REST_RV)),
                  pl.BlockSpec((1, ts, LANES), col(REST_RG)),
                  pl.BlockSpec((ts, LANES), lambda bi, p, i: (i, 0)),
                  pl.BlockSpec((ts, LANES), lambda bi, p, i: (i, 0)),
                  pl.BlockSpec((2, RET_CHUNK, RET_CHUNK), tab),
                  pl.BlockSpec((1, RET_CHUNK, LANES), tab),
                  pl.BlockSpec((1, RET_CHUNK, LANES), tab),
                  pl.BlockSpec((1, LANES, LANES), tab),
                  pl.BlockSpec((1, LANES, LANES), tab)],
        out_specs=pl.BlockSpec((1, ts, LANES), lambda bi, p, i: (bi, i, p)),
        out_shape=jax.ShapeDtypeStruct((b, s, RET_WIDTH), BF16),
        scratch_shapes=[pltpu.VMEM((LANES, LANES), F32)],
        compiler_params=_params("parallel", "parallel", "arbitrary"),
        name="retention",
    )(rest3, rest3, rest3, rest3, cos, sin, dmat, qdec, kdec, cd, bm)


def _gdn_prep_kernel(q_ref, k_ref, v_ref, sm_ref, wq_ref, wk_ref, wv_ref, alog_ref, dtb_ref,
                     u0_ref, w_ref, qg_ref, kt_ref, at_ref, eg_ref):
    h = pl.program_id(1)
    c_len = GDN_CHUNK
    n_chunks = q_ref.shape[1] // c_len
    lane = lax.broadcasted_iota(jnp.int32, (1, LANES), 1)
    ri = lax.broadcasted_iota(jnp.int32, (c_len, c_len), 0)
    ci = lax.broadcasted_iota(jnp.int32, (c_len, c_len), 1)
    incl = ri >= ci
    strict = ri > ci
    eye = (ri == ci).astype(F32)
    neg_a = -jnp.exp(alog_ref[...])
    dtb = dtb_ref[...]

    grp = GDN_PREP_GROUP
    rows = grp * c_len

    def conv_silu(ref, w_ref_, n, r0):
        cur = ref[0, pl.ds(r0, rows), :]
        p0 = pl.multiple_of(jnp.maximum(r0 - 8, 0), 8)
        prev = ref[0, pl.ds(p0, 8), :]
        prev = jnp.where(jnp.broadcast_to(n > 0, prev.shape), prev, 0.0)
        xc = jnp.concatenate([prev, cur], axis=0)
        w = w_ref_[...]
        y = cur * w[CONV_WIDTH - 1:CONV_WIDTH, :]
        for j in range(CONV_WIDTH - 1):
            shifted = pltpu.roll(xc, CONV_WIDTH - 1 - j, 0)[8:, :]
            y = y + shifted * w[j:j + 1, :]
        return _silu(y)

    def group(n, carry):
        r0 = pl.multiple_of(n * rows, rows)
        cq = conv_silu(q_ref, wq_ref, n, r0)
        ck = conv_silu(k_ref, wk_ref, n, r0)
        cv = conv_silu(v_ref, wv_ref, n, r0)
        qn = cq * lax.rsqrt(jnp.sum(cq * cq, axis=-1, keepdims=True) + NORM_EPS) * (GDN_DIM ** -0.5)
        kn = ck * lax.rsqrt(jnp.sum(ck * ck, axis=-1, keepdims=True) + NORM_EPS)
        sm = sm_ref[0, pl.ds(r0, rows), :]
        z = sm + dtb
        g_all = neg_a * (jnp.maximum(z, 0.0) + jnp.log1p(jnp.exp(-jnp.abs(z))))
        beta_all = 1.0 / (1.0 + jnp.exp(-sm))
        g_col = jnp.sum(jnp.where(lane == SMALL_GA + h, g_all, 0.0), axis=-1, keepdims=True)
        beta = jnp.sum(jnp.where(lane == SMALL_GB + h, beta_all, 0.0), axis=-1, keepdims=True)
        kb = kn * beta
        vb = cv * beta
        chunks = [slice(g * c_len, (g + 1) * c_len) for g in range(grp)]
        g_row = [jnp.sum(g_col[c] * eye, axis=0, keepdims=True) for c in chunks]
        gc = [jnp.sum(jnp.where(incl, g_row[i], 0.0), axis=-1, keepdims=True) for i in range(grp)]
        gc_row = [jnp.sum(jnp.where(ri <= ci, g_col[c], 0.0), axis=0, keepdims=True) for c in chunks]
        decay = [jnp.where(incl, jnp.exp(jnp.where(incl, gc[i] - gc_row[i], 0.0)), 0.0) for i in range(grp)]
        both = [_mm_nt(jnp.concatenate([kb[c], qn[c]], axis=0), kn[c]) for c in chunks]
        low = [jnp.where(strict, both[i][:c_len] * decay[i], 0.0) for i in range(grp)]
        attn = [jnp.where(incl, both[i][c_len:] * decay[i], 0.0) for i in range(grp)]
        inv = [eye - low[i] for i in range(grp)]
        pw = low
        for _ in range(int(math.log2(c_len)) - 1):
            pw = [_mm(pw[i], pw[i]) for i in range(grp)]
            inv = [inv[i] + _mm(inv[i], pw[i]) for i in range(grp)]
        eg = [jnp.exp(gc[i]) for i in range(grp)]
        sol = [_mm(inv[i], jnp.concatenate([vb[c], kb[c] * eg[i]], axis=1)) for i, c in enumerate(chunks)]
        for i, c in enumerate(chunks):
            dst = pl.ds(pl.multiple_of(r0 + i * c_len, c_len), c_len)
            g_last = gc[i][c_len - 1:c_len, :]
            u0_ref[0, 0, dst, :] = sol[i][:, :GDN_DIM]
            w_ref[0, 0, dst, :] = sol[i][:, GDN_DIM:].astype(BF16)
            qg_ref[0, 0, dst, :] = (qn[c] * eg[i]).astype(BF16)
            kt_ref[0, 0, dst, :] = (kn[c] * jnp.exp(g_last - gc[i])).astype(BF16)
            at_ref[0, 0, dst, :] = attn[i].astype(BF16)
            eg_ref[0, 0, n * grp + i] = jnp.broadcast_to(jnp.exp(g_last), (1, LANES))
        return carry

    lax.fori_loop(0, n_chunks // grp, group, 0)


def _gdn_prep(rest3, conv_w, alog_l, dtb_l):
    b, s, _ = rest3.shape
    nh = GDN_HEADS
    n_chunks = s // GDN_CHUNK
    col = lambda off: (lambda bi, h: (bi, 0, off // LANES + h))
    wcol = lambda g: (lambda bi, h: (0, g * nh + h))
    const = lambda bi, h: (0, 0)
    row =pl.BlockSpec((1, 1, s, GDN_DIM), lambda bi, h: (bi, h, 0, 0))
    return pl.pallas_call(
        _gdn_prep_kernel,
        grid=(b, nh),
        in_specs=[pl.BlockSpec((1, s, LANES), col(REST_GQ)),
                  pl.BlockSpec((1, s, LANES), col(REST_GK)),
                  pl.BlockSpec((1, s, LANES), col(REST_GV)),
                  pl.BlockSpec((1, s, LANES), lambda bi, h: (bi, 0, REST_SMALL // LANES)),
                  pl.BlockSpec((CONV_WIDTH, LANES), wcol(0)),
                  pl.BlockSpec((CONV_WIDTH, LANES), wcol(1)),
                  pl.BlockSpec((CONV_WIDTH, LANES), wcol(2)),
                  pl.BlockSpec((1, LANES), const),
                  pl.BlockSpec((1, LANES), const)],
        out_specs=[row, row, row, row,
                   pl.BlockSpec((1, 1, s, GDN_CHUNK), lambda bi, h: (bi, h, 0, 0)),
                   pl.BlockSpec((1, 1, n_chunks, 1, LANES), lambda bi, h: (bi, h, 0, 0, 0))],
        out_shape=[jax.ShapeDtypeStruct((b, nh, s, GDN_DIM), F32),
                   jax.ShapeDtypeStruct((b, nh, s, GDN_DIM), BF16),
                   jax.ShapeDtypeStruct((b, nh, s, GDN_DIM), BF16),
                   jax.ShapeDtypeStruct((b, nh, s, GDN_DIM), BF16),
                   jax.ShapeDtypeStruct((b, nh, s, GDN_CHUNK), BF16),
                   jax.ShapeDtypeStruct((b, nh, n_chunks, 1, LANES), F32)],
        compiler_params=_params("parallel", "parallel"),
        name="gdn_prep",
    )(rest3, rest3, rest3, rest3, conv_w, conv_w, conv_w, alog_l, dtb_l)


def _gdn_scan_kernel(u0_ref, w_ref, qg_ref, kt_ref, at_ref, eg_ref, z_ref, nw_ref, o_ref, state_ref, *, ts):
    si = pl.program_id(1)

    @pl.when(si == 0)
    def _():
        state_ref[...] = jnp.zeros_like(state_ref)

    c_len = GDN_CHUNK
    per_tile = ts // c_len
    nw = nw_ref[...]
    for c in range(per_tile):
        rows = slice(c * c_len, (c + 1) * c_len)
        heads = range(GDN_HEADS)
        st = [state_ref[h] for h in heads]
        r = [_mm(jnp.concatenate([w_ref[0, h, rows, :], qg_ref[0, h, rows, :]], axis=0), st[h]) for h in heads]
        u = [(u0_ref[0, h, rows, :] - r[h][:c_len]).astype(BF16) for h in heads]
        ku = [_mm_tn(kt_ref[0, h, rows, :], u[h]) for h in heads]
        au = [jnp.dot(at_ref[0, h, rows, :], u[h], preferred_element_type=F32) for h in heads]
        for h in heads:
            state_ref[h] = st[h] * eg_ref[0, h, si * per_tile + c] + ku[h]
        for h in heads:
            cols = slice(h * GDN_DIM, (h + 1) * GDN_DIM)
            y = _rms(r[h][c_len:] + au[h], nw) * _silu(z_ref[0, rows, cols])
            o_ref[0, rows, cols] = y.astype(BF16)


def _gdn_scan(u0, w, qg, kt, at, eg, rest3, norm_w, ts=512):
    b, nh, s, _ = u0.shape
    n_chunks = s // GDN_CHUNK
    blk = lambda d: pl.BlockSpec((1, nh, ts, d), lambda bi, i: (bi, 0, i, 0))
    kern = functools.partial(_gdn_scan_kernel, ts=ts)
    return pl.pallas_call(
        kern,
        grid=(b, s // ts),
        in_specs=[blk(GDN_DIM), blk(GDN_DIM), blk(GDN_DIM), blk(GDN_DIM), blk(GDN_CHUNK),
                  pl.BlockSpec((1, nh, n_chunks, 1, LANES), lambda bi, i: (bi, 0, 0, 0, 0)),
                  pl.BlockSpec((1, ts, GDN_WIDTH), lambda bi, i: (bi, i, REST_GZ // GDN_WIDTH)),
                  pl.BlockSpec((1, GDN_DIM), lambda bi, i: (0, 0))],
        out_specs=pl.BlockSpec((1, ts, GDN_WIDTH), lambda bi, i: (bi, i, 0)),
        out_shape=jax.ShapeDtypeStruct((b, s, GDN_WIDTH), BF16),
        scratch_shapes=[pltpu.VMEM((nh, GDN_DIM, GDN_DIM), F32)],
        compiler_params=_params("parallel", "arbitrary"),
        name="gdn_scan",
    )(u0, w, qg, kt, at, eg, rest3, norm_w)


def _outproj_kernel(x_ref, of_ref, or_ref, og_ref, w_ref, o_ref):
    acc = x_ref[...]
    acc = acc + jnp.dot(of_ref[...], w_ref[0:FOX_WIDTH, :], preferred_element_type=F32)
    acc = acc + jnp.dot(or_ref[...], w_ref[FOX_WIDTH:FOX_WIDTH + RET_WIDTH, :], preferred_element_type=F32)
    acc = acc + jnp.dot(og_ref[...], w_ref[FOX_WIDTH + RET_WIDTH:, :], preferred_element_type=F32)
    o_ref[...] = acc


def _outproj(x, o_fox, o_ret, o_gdn, w_out, tm=512):
    t = x.shape[0]
    d_mix = w_out.shape[0]
    return pl.pallas_call(
        _outproj_kernel,
        grid=(t // tm,),
        in_specs=[pl.BlockSpec((tm, D_MODEL), lambda i: (i, 0)),
                  pl.BlockSpec((tm, FOX_WIDTH), lambda i: (i, 0)),
                  pl.BlockSpec((tm, RET_WIDTH), lambda i: (i, 0)),
                  pl.BlockSpec((tm, GDN_WIDTH), lambda i: (i, 0)),
                  pl.BlockSpec((d_mix, D_MODEL), lambda i: (0, 0))],
        out_specs=pl.BlockSpec((tm, D_MODEL), lambda i: (i, 0)),
        out_shape=jax.ShapeDtypeStruct((t, D_MODEL), F32),
        compiler_params=_params("parallel"),
        name="outproj",
    )(x, o_fox, o_ret, o_gdn, w_out)


def _router_kernel(x_ref, nw_ref, w_ref, b_ref, tri_ref, sel_ref, route_ref, idx_ref, cnt_ref, carry_ref):
    @pl.when(pl.program_id(0) == 0)
    def _():
        carry_ref[...] = jnp.zeros_like(carry_ref)

    tm = x_ref.shape[0]
    hn = _rms(x_ref[...], nw_ref[...])
    logits = jnp.dot(hn, w_ref[...], precision=HIGHEST, preferred_element_type=F32) + b_ref[...]
    lane = lax.broadcasted_iota(jnp.int32, logits.shape, 1).astype(F32)
    neg = -jnp.inf
    gl = jnp.where(lane < ROUTER_EXP, logits, neg)
    gmax = jnp.max(gl, axis=-1, keepdims=True)
    gidx = jnp.min(jnp.where(gl == gmax, lane, LANES), axis=-1, keepdims=True)
    grp_p = 1.0 / jnp.sum(jnp.exp(gl - gmax), axis=-1, keepdims=True)
    lo = ROUTER_EXP + gidx * EXPERTS_PER_GROUP
    el = jnp.where((lane >= lo) & (lane < lo + EXPERTS_PER_GROUP), logits, neg)
    m1 = jnp.max(el, axis=-1, keepdims=True)
    i1 = jnp.min(jnp.where(el == m1, lane, LANES), axis=-1, keepdims=True)
    el2 = jnp.where(lane == i1, neg, el)
    m2 = jnp.max(el2, axis=-1, keepdims=True)
    i2 = jnp.min(jnp.where(el2 == m2, lane, LANES), axis=-1, keepdims=True)
    e2 = jnp.exp(m2 - m1)
    w1 = grp_p / (1.0 + e2)
    ea = jnp.minimum(i1, i2) - ROUTER_EXP
    eb = jnp.maximum(i1, i2) - ROUTER_EXP
    first_low = i1 < i2
    ga = jnp.where(first_low, w1, w1 * e2)
    gb = jnp.where(first_low, w1 * e2, w1)
    la = ea - gidx * EXPERTS_PER_GROUP
    lb = eb - gidx * EXPERTS_PER_GROUP
    cls = gidx * N_PAIRS + la * (2 * EXPERTS_PER_GROUP - 1 - la) * 0.5 + (lb - la - 1.0)
    oh = (lane == cls).astype(F32)
    cum = jnp.dot(tri_ref[...], oh.astype(BF16), preferred_element_type=F32) + carry_ref[...]
    rank = jnp.sum(oh * (cum - oh), axis=-1, keepdims=True)
    carry_ref[...] = cum[tm - 1:tm, :]
    cnt_ref[...] = cum[tm - 1:tm, :]
    out = jnp.zeros_like(logits)
    for col, val in ((ROUTE_CLASS, cls), (ROUTE_RANK, rank), (ROUTE_GATE, ga), (ROUTE_GATE + 1, gb)):
        out = jnp.where(lane == col, val, out)
    route_ref[...] = out
    idx = lax.dot_general(sel_ref[...], out, (((1,), (1,)), ((), ())), precision=HIGHEST,
                          preferred_element_type=F32)
    idx_ref[...] = idx.astype(jnp.int32)


def _router(x, nw, w_pack, b_pack, tm=512):
    t = x.shape[0]
    tri = jnp.asarray((np.arange(tm)[:, None] >= np.arange(tm)[None, :]).astype(np.float32), dtype=BF16)
    sel = np.zeros((8, LANES), np.float32)
    for row, lane in enumerate((ROUTE_CLASS, ROUTE_RANK)):
        sel[row, lane] = 1.0
    return pl.pallas_call(
        _router_kernel,
        grid=(t // tm,),
        in_specs=[pl.BlockSpec((tm, D_MODEL), lambda i: (i, 0)),
                  pl.BlockSpec((1, D_MODEL), lambda i: (0, 0)),
                  pl.BlockSpec((D_MODEL, LANES), lambda i: (0, 0)),
                  pl.BlockSpec((1, LANES), lambda i: (0, 0)),
                  pl.BlockSpec((tm, tm), lambda i: (0, 0)),
                  pl.BlockSpec((8, LANES), lambda i: (0, 0))],
        out_specs=[pl.BlockSpec((tm, LANES), lambda i: (i, 0)),
                   pl.BlockSpec((8, tm), lambda i: (0, i)),
                   pl.BlockSpec((1, LANES), lambda i: (0, 0))],
        out_shape=[jax.ShapeDtypeStruct((t, LANES), F32),
                   jax.ShapeDtypeStruct((8, t), jnp.int32),
                   jax.ShapeDtypeStruct((1, LANES), F32)],
        scratch_shapes=[pltpu.VMEM((1, LANES), F32)],
        compiler_params=_params("arbitrary"),
        name="router",
    )(x, nw, w_pack, b_pack, tri, jnp.asarray(sel))


def _round_up_tile(v):
    shift = MOE_TILE.bit_length() - 1
    return lax.shift_left(lax.shift_right_logical(v + (MOE_TILE - 1), shift), shift)


def _dispatch_kernel(off_ref, cnt_ref, tot_ref, pos_ref, x_ref, route_ref, xs_ref, ring_ref, zero_ref, sem_ref, *,
                     tm, n_tiles):
    i = pl.program_id(0)
    n = pl.num_programs(0)
    slot = i % 2

    def row_copy(s, r, p):
        return pltpu.make_async_copy(ring_ref.at[s, pl.ds(r, 1)], xs_ref.at[pl.ds(p, 1)], sem_ref.at[s])

    def pad_rows(p, size):
        return pltpu.make_async_copy(zero_ref.at[pl.ds(0, size)], xs_ref.at[pl.ds(p, size)], sem_ref.at[2])

    def pad_tile(j):
        return pltpu.make_async_copy(zero_ref, xs_ref.at[pl.ds(j * MOE_TILE, MOE_TILE)], sem_ref.at[3])

    def pad_class(c, start):
        lo = cnt_ref[c]
        lo8 = lax.shift_left(lax.shift_right_logical(lo + 7, 3), 3)
        rem = _round_up_tile(lo) - lo8

        def single(r, carry):
            cp = pad_rows(off_ref[c] + r, 1)
            cp.start() if start else cp.wait()
            return carry

        lax.fori_loop(lo, lo8, single, 0)
        for bit in range(3, MOE_TILE.bit_length() - 1):
            size = 1 << bit

            @pl.when(lax.bitwise_and(rem, size) != 0)
            def _():
                above = lax.shift_left(lax.shift_right_logical(rem, bit + 1), bit + 1)
                cp = pad_rows(pl.multiple_of(off_ref[c] + lo8 + above, 8), size)
                cp.start() if start else cp.wait()

    @pl.when(i == 0)
    def _():
        zero_ref[...] = jnp.zeros_like(zero_ref)

        def start_class(c, carry):
            pad_class(c, True)
            return carry

        def wait_class(c, carry):
            pad_class(c, False)
            return carry

        def start_tile(j, c):
            pad_tile(j).start()
            return c

        def wait_tile(j, c):
            pad_tile(0).wait()
            return c

        lax.fori_loop(0, N_CLASSES, start_class, 0)
        lax.fori_loop(tot_ref[0], n_tiles, start_tile, 0)
        lax.fori_loop(0, N_CLASSES, wait_class, 0)
        lax.fori_loop(tot_ref[0], n_tiles, wait_tile, 0)

    ring_ref[slot, :, :D_MODEL] = x_ref[...]
    ring_ref[slot, :, D_MODEL:] = route_ref[...]

    def issue(g, c):
        for u in range(2):
            r = 2 * g + u
            row_copy(slot, r, pos_ref[0, r]).start(priority=u)
        return c

    lax.fori_loop(0, tm // 2, issue, 0, unroll=4)

    def drain(s):
        def wait(r, c):
            row_copy(s, 0, 0).wait()
            return c

        lax.fori_loop(0, tm, wait, 0, unroll=8)

    @pl.when(i > 0)
    def _():
        drain(1 - slot)

    @pl.when(i == n - 1)
    def _():
        drain(slot)


def _dispatch(x, route, off, cnt, total, pos, tm, n_tiles):
    t = x.shape[0]
    kern = functools.partial(_dispatch_kernel, tm=tm, n_tiles=n_tiles)
    return pl.pallas_call(
        kern,
        grid_spec=pltpu.PrefetchScalarGridSpec(
            num_scalar_prefetch=3,
            grid=(t // tm,),
            in_specs=[pl.BlockSpec((1, tm), lambda i, *_: (0, i), memory_space=pltpu.SMEM),
                      pl.BlockSpec((tm, D_MODEL), lambda i, *_: (i, 0)),
                      pl.BlockSpec((tm, LANES), lambda i, *_: (i, 0))],
            out_specs=pl.BlockSpec(memory_space=pl.ANY),
            scratch_shapes=[pltpu.VMEM((2, tm, ROW_WIDTH), F32),
                            pltpu.VMEM((MOE_TILE, ROW_WIDTH), F32),
                            pltpu.SemaphoreType.DMA((4,))]),
        out_shape=jax.ShapeDtypeStruct((n_tiles * MOE_TILE, ROW_WIDTH), F32),
        compiler_params=_params("arbitrary"),
        name="dispatch",
    )(off, cnt, total, pos, x, route)


def _ffn_kernel(ta_ref, tb_ref, fa_ref, fb_ref, tot_ref, xs_ref, nw_ref, w1a_ref, w3a_ref, w2a_ref,
                w1b_ref, w3b_ref, w2b_ref, ys_ref, w1a, w3a, w2a, w1b, w3b, w2b):
    j = pl.program_id(0)

    @pl.when(fa_ref[j] == 1)
    def _():
        w1a[...] = w1a_ref[0].astype(BF16)
        w3a[...] = w3a_ref[0].astype(BF16)
        w2a[...] = w2a_ref[0].astype(BF16)

    @pl.when(fb_ref[j] == 1)
    def _():
        w1b[...] = w1b_ref[0].astype(BF16)
        w3b[...] = w3b_ref[0].astype(BF16)
        w2b[...] = w2b_ref[0].astype(BF16)

    @pl.when(j < tot_ref[0])
    def _():
        x = xs_ref[:, :D_MODEL]
        ga = xs_ref[:, D_MODEL + ROUTE_GATE:D_MODEL + ROUTE_GATE + 1]
        gb = xs_ref[:, D_MODEL + ROUTE_GATE + 1:D_MODEL + ROUTE_GATE + 2]
        hn = _rms(x, nw_ref[...]).astype(BF16)

        def expert(w1, w3, w2):
            a = jnp.dot(hn, w1[...], preferred_element_type=F32)
            u = jnp.dot(hn, w3[...], preferred_element_type=F32)
            return jnp.dot((_silu(a) * u).astype(BF16), w2[...], preferred_element_type=F32)

        ys_ref[...] = x + ga * expert(w1a, w3a, w2a) + gb * expert(w1b, w3b, w2b)

    @pl.when(j >= tot_ref[0])
    def _():
        ys_ref[...] = jnp.zeros_like(ys_ref)


def _ffn(xs, nw, w1, w3, w2, tile_a, tile_b, first_a, first_b, total):
    n_tiles = tile_a.shape[0]
    tok = lambda j, *_: (j, 0)
    wa = lambda j, ta, tb, fa, fb, tot: (ta[j], 0, 0)
    wb = lambda j, ta, tb, fa, fb, tot: (tb[j], 0, 0)
    up = lambda imap: pl.BlockSpec((1, D_MODEL, EXPERT_FF), imap)
    down = lambda imap: pl.BlockSpec((1, EXPERT_FF, D_MODEL), imap)
    return pl.pallas_call(
        _ffn_kernel,
        grid_spec=pltpu.PrefetchScalarGridSpec(
            num_scalar_prefetch=5,
            grid=(n_tiles,),
            in_specs=[pl.BlockSpec((MOE_TILE, ROW_WIDTH), tok),
                      pl.BlockSpec((1, D_MODEL), lambda j, *_: (0, 0)),
                      up(wa), up(wa), down(wa), up(wb), up(wb), down(wb)],
            out_specs=pl.BlockSpec((MOE_TILE, D_MODEL), tok),
            scratch_shapes=[pltpu.VMEM((D_MODEL, EXPERT_FF), BF16),
                            pltpu.VMEM((D_MODEL, EXPERT_FF), BF16),
                            pltpu.VMEM((EXPERT_FF, D_MODEL), BF16),
                            pltpu.VMEM((D_MODEL, EXPERT_FF), BF16),
                            pltpu.VMEM((D_MODEL, EXPERT_FF), BF16),
                            pltpu.VMEM((EXPERT_FF, D_MODEL), BF16)]),
        out_shape=jax.ShapeDtypeStruct((n_tiles * MOE_TILE, D_MODEL), F32),
        compiler_params=_params("arbitrary"),
        name="ffn",
    )(tile_a, tile_b, first_a, first_b, total, xs, nw, w1, w3, w2, w1, w3, w2)


def _gather_kernel(pos_ref, posn_ref, fw_ref, ys_ref, o_ref, ybuf, sem_ref, *, tm, final_norm):
    i = pl.program_id(0)
    n = pl.num_programs(0)
    slot = i % 2

    def row_copy(s, r, p):
        return pltpu.make_async_copy(ys_ref.at[pl.ds(p, 1)], ybuf.at[s, pl.ds(r, 1)], sem_ref.at[s])

    def issue(p_ref, s):
        def body(g, c):
            for u in range(2):
                r = 2 * g + u
                row_copy(s, r, p_ref[0, r]).start(priority=u)
            return c

        lax.fori_loop(0, tm // 2, body, 0, unroll=4)

    @pl.when(i == 0)
    def _():
        issue(pos_ref, 0)

    @pl.when(i + 1 < n)
    def _():
        issue(posn_ref, 1 - slot)

    def wait(r, c):
        row_copy(slot, 0, 0).wait()
        return c

    lax.fori_loop(0, tm, wait, 0, unroll=8)
    out = ybuf[slot]
    if final_norm:
        out = _rms(out, fw_ref[...])
    o_ref[...] = out


def _gather(ys, pos, fw, final_norm, tm):
    t = pos.shape[1]
    n = t // tm
    kern = functools.partial(_gather_kernel, tm=tm, final_norm=final_norm)
    smem = lambda imap: pl.BlockSpec((1, tm), imap, memory_space=pltpu.SMEM)
    return pl.pallas_call(
        kern,
        grid=(n,),
        in_specs=[smem(lambda i: (0, i)),
                  smem(lambda i: (0, jnp.minimum(i + 1, n - 1))),
                  pl.BlockSpec((1, D_MODEL), lambda i: (0, 0)),
                  pl.BlockSpec(memory_space=pl.ANY)],
        out_specs=pl.BlockSpec((tm, D_MODEL), lambda i: (i, 0)),
        out_shape=jax.ShapeDtypeStruct((t, D_MODEL), F32),
        scratch_shapes=[pltpu.VMEM((2, tm, D_MODEL), F32),
                        pltpu.SemaphoreType.DMA((2,))],
        compiler_params=_params("arbitrary"),
        name="gather",
    )(pos, pos, fw, ys)


def _class_experts():
    lo, hi = [], []
    for g in range(N_GROUPS):
        for la in range(EXPERTS_PER_GROUP):
            for lb in range(la + 1, EXPERTS_PER_GROUP):
                lo.append(g * EXPERTS_PER_GROUP + la)
                hi.append(g * EXPERTS_PER_GROUP + lb)
    return np.asarray(lo, np.int32), np.asarray(hi, np.int32)


def _lookup(table, index):
    k = table.shape[0]
    return jnp.sum(jnp.where(index[..., None] == jnp.arange(k, dtype=jnp.int32), table, 0), axis=-1)


def _moe(x, nw, w_pack, b_pack, w1, w3, w2, fw, final_norm, tm=256):
    t = x.shape[0]
    route, idx, cnt_row = _router(x, nw, w_pack, b_pack)
    cnt = cnt_row[0, :N_CLASSES].astype(jnp.int32)
    n_tiles = t // MOE_TILE + N_CLASSES
    nblk = (cnt + MOE_TILE - 1) // MOE_TILE
    cend = jnp.cumsum(nblk)
    total = cend[-1:]
    off = (cend - nblk) * MOE_TILE
    pos = (_lookup(off, idx[ROUTE_CLASS]) + idx[ROUTE_RANK]).reshape(1, t)
    j = jnp.arange(n_tiles, dtype=jnp.int32)
    tile_c = jnp.minimum(jnp.sum((j[:, None] >= cend[None, :]).astype(jnp.int32), axis=1), N_CLASSES - 1)
    lo, hi = _class_experts()
    tile_a = _lookup(jnp.asarray(lo), tile_c)
    tile_b = _lookup(jnp.asarray(hi), tile_c)
    valid = j < total
    changed = lambda e: (valid & ((j == 0) | (e != jnp.roll(e, 1)))).astype(jnp.int32)
    xs = _dispatch(x, route, off, cnt, total, pos, tm, n_tiles)
    ys = _ffn(xs, nw, w1, w3, w2, tile_a, tile_b, changed(tile_a), changed(tile_b), total)
    return _gather(ys, pos, fw, final_norm, tm)


def _pack_in_weights(w_in_l):
    off = np.concatenate([[0], np.cumsum(IN_SPLITS)]).tolist()
    fq, fk, fv, ff, rq, rk, rv, rg, gq, gk, gv, gz, ga, gb = [(off[i], IN_SPLITS[i]) for i in range(len(IN_SPLITS))]
    half = HEAD64 // 2

    def permuted(seg):
        return [(seg[0] + (2 * p + hh) * HEAD64 + lo * half, half)
                for p in range(RET_HEADS // 2) for lo in range(2) for hh in range(2)]

    col_scale = np.ones((1, w_in_l.shape[1]), np.float32)
    col_scale[:, fq[0]:fq[0] + fq[1]] = LOG2E * HEAD64 ** -0.5
    w_bf = (w_in_l * col_scale).astype(BF16)
    cols = lambda segs: [w_bf[:, a:a + n] for a, n in segs]
    wf = jnp.concatenate(cols([fq, fk, fv]), axis=1)
    pad = jnp.zeros((D_MODEL, LANES - (FOX_HEADS + 2 * GDN_HEADS)), BF16)
    wr = jnp.concatenate(cols(permuted(rq) + permuted(rk) + [rv, rg, gq, gk, gv, gz, ff, ga, gb]) + [pad], axis=1)
    return wf, wr


def _lane_row(vals, offset):
    return jnp.zeros((1, LANES), F32).at[0, offset:offset + vals.shape[0]].set(vals.astype(F32))


def kernel(x, norm1_w, w_in, fox_forget_bias, gdn_conv_w, gdn_a_log, gdn_dt_bias, gdn_norm_w, w_out, norm2_w,
           router_group_w, router_group_b, router_expert_w, router_expert_b, expert_w1, expert_w3, expert_w2,
           final_norm_w):
    b, s, d = x.shape
    t = b * s
    depth = w_in.shape[0]
    fox_tk = 512
    xt = x.reshape(t, d)
    ret_tables = _ret_tables(s)
    for l in range(depth):
        wf, wr = _pack_in_weights(w_in[l])
        qkv, rest = _inproj(xt, norm1_w[l].reshape(1, d), wf, wr)
        rest3 = rest.reshape(b, s, REST_WIDTH)
        c = _fgate(rest3, _lane_row(fox_forget_bias[l], SMALL_FF)).reshape(b, FOX_HEADS, s // fox_tk, 1, fox_tk)
        o_fox = _fox(qkv.reshape(b, s, 3 * FOX_WIDTH), c, tk=fox_tk)
        o_ret = _ret(rest3, ret_tables)
        prep = _gdn_prep(rest3, gdn_conv_w[l].astype(F32), _lane_row(gdn_a_log[l], SMALL_GA),
                         _lane_row(gdn_dt_bias[l], SMALL_GA))
        o_gdn = _gdn_scan(*prep, rest3, gdn_norm_w[l].reshape(1, GDN_DIM).astype(F32))
        xt = _outproj(xt, o_fox.reshape(t, FOX_WIDTH), o_ret.reshape(t, RET_WIDTH),
                      o_gdn.reshape(t, GDN_WIDTH), w_out[l].astype(BF16))
        w_pack = jnp.concatenate([router_group_w[l], router_expert_w[l],
                                  jnp.zeros((d, LANES - N_GROUPS - N_EXPERTS), F32)], axis=1)
        b_pack = jnp.concatenate([router_group_b[l].reshape(-1), router_expert_b[l].reshape(-1),
                                  jnp.zeros((LANES - N_GROUPS - N_EXPERTS,), F32)]).reshape(1, LANES)
        xt = _moe(xt, norm2_w[l].reshape(1, d), w_pack, b_pack,
                  expert_w1[l].reshape(N_EXPERTS, d, EXPERT_FF),
                  expert_w3[l].reshape(N_EXPERTS, d, EXPERT_FF),
                  expert_w2[l].reshape(N_EXPERTS, EXPERT_FF, d),
                  final_norm_w.reshape(1, d), final_norm=(l == depth - 1))
    return xt.reshape(b, s, d)
```

```python
import functools
import math

import jax
import jax.numpy as jnp
import numpy as np
from jax import lax
from jax.experimental import pallas as pl
from jax.experimental.pallas import tpu as pltpu

F32 = jnp.float32
BF16 = jnp.bfloat16
HIGHEST = lax.Precision.HIGHEST

D_MODEL = 1024
FOX_HEADS = 4
RET_HEADS = 4
GDN_HEADS = 4
HEAD64 = 64
GDN_DIM = 128
FOX_WIDTH = FOX_HEADS * HEAD64
RET_WIDTH = RET_HEADS * HEAD64
GDN_WIDTH = GDN_HEADS * GDN_DIM
RET_CHUNK = 128
GDN_CHUNK = 64
GDN_PREP_GROUP = 8
CONV_WIDTH = 4
RET_ANGLE_BASE = 10000.0
N_GROUPS = 4
EXPERTS_PER_GROUP = 8
N_EXPERTS = N_GROUPS * EXPERTS_PER_GROUP
EXPERT_FF = 256
NORM_EPS = 1e-6
LOG2E = math.log2(math.e)
LANES = 128
IN_SPLITS = (FOX_WIDTH, FOX_WIDTH, FOX_WIDTH, FOX_HEADS,
             RET_WIDTH, RET_WIDTH, RET_WIDTH, RET_WIDTH,
             GDN_WIDTH, GDN_WIDTH, GDN_WIDTH, GDN_WIDTH, GDN_HEADS, GDN_HEADS)

REST_RQ, REST_RK, REST_RV, REST_RG = 0, 256, 512, 768
REST_GQ, REST_GK, REST_GV, REST_GZ = 1024, 1536, 2048, 2560
REST_SMALL = 3072
REST_WIDTH = 3200
SMALL_FF, SMALL_GA, SMALL_GB = 0, 4, 8
ROUTER_GRP, ROUTER_EXP = 0, 4
ROUTE_CLASS, ROUTE_RANK, ROUTE_GATE = 0, 1, 2
N_PAIRS = EXPERTS_PER_GROUP * (EXPERTS_PER_GROUP - 1) // 2
N_CLASSES = N_GROUPS * N_PAIRS
MOE_TILE = 256
ROW_WIDTH = D_MODEL + LANES

VMEM_LIMIT = 56 * 1024 * 1024


def _params(*sem):
    return pltpu.CompilerParams(dimension_semantics=sem, vmem_limit_bytes=VMEM_LIMIT)


def _mm(a, b):
    return jnp.dot(a.astype(BF16), b.astype(BF16), preferred_element_type=F32)


def _mm_nt(a, b):
    return lax.dot_general(a.astype(BF16), b.astype(BF16), (((1,), (1,)), ((), ())),
                           preferred_element_type=F32)


def _mm_tn(a, b):
    return lax.dot_general(a.astype(BF16), b.astype(BF16), (((0,), (0,)), ((), ())),
                           preferred_element_type=F32)


def _sigmoid(x):
    return 0.5 + 0.5 * jnp.tanh(0.5 * x)


def _silu(x):
    return x * _sigmoid(x)


def _rms(x, w):
    return x * lax.rsqrt(jnp.mean(x * x, axis=-1, keepdims=True) + NORM_EPS) * w


def _inproj_kernel(x_ref, nw_ref, wf_ref, wr_ref, of_ref, or_ref):
    hn = _rms(x_ref[...], nw_ref[...]).astype(BF16)
    of_ref[...] = jnp.dot(hn, wf_ref[...], preferred_element_type=F32).astype(BF16)
    step = 640
    for c in range(0, REST_WIDTH, step):
        or_ref[:, c:c + step] = jnp.dot(hn, wr_ref[:, c:c + step], preferred_element_type=F32)


def _inproj(x, nw, wf, wr, tm=512):
    t = x.shape[0]
    return pl.pallas_call(
        _inproj_kernel,
        grid=(t // tm,),
        in_specs=[pl.BlockSpec((tm, D_MODEL), lambda i: (i, 0)),
                  pl.BlockSpec((1, D_MODEL), lambda i: (0, 0)),
                  pl.BlockSpec((D_MODEL, 3 * FOX_WIDTH), lambda i: (0, 0)),
                  pl.BlockSpec((D_MODEL, REST_WIDTH), lambda i: (0, 0))],
        out_specs=[pl.BlockSpec((tm, 3 * FOX_WIDTH), lambda i: (i, 0)),
                   pl.BlockSpec((tm, REST_WIDTH), lambda i: (i, 0))],
        out_shape=[jax.ShapeDtypeStruct((t, 3 * FOX_WIDTH), BF16),
                   jax.ShapeDtypeStruct((t, REST_WIDTH), F32)],
        compiler_params=_params("parallel"),
        name="inproj",
    )(x, nw, wf, wr)


def _fgate_kernel(sm_ref, bias_ref, sel_ref, tri_ref, c_ref):
    n_blk = sm_ref.shape[1] // LANES
    z = sm_ref[0] + bias_ref[...]
    lf = jnp.minimum(z, 0.0) - jnp.log1p(jnp.exp(-jnp.abs(z)))
    sel = sel_ref[...]
    tri = tri_ref[...]
    within = []
    for j in range(n_blk):
        blk = lf[j * LANES:(j + 1) * LANES, :]
        x = lax.dot_general(sel, blk, (((1,), (1,)), ((), ())), precision=HIGHEST, preferred_element_type=F32)
        within.append(jnp.dot(x, tri, precision=HIGHEST, preferred_element_type=F32))
    carry = jnp.zeros((sel.shape[0], 1), F32)
    for j in range(n_blk):
        cj = within[j] + carry
        for h in range(FOX_HEADS):
            c_ref[0, h, :, j * LANES:(j + 1) * LANES] = cj[h:h + 1, :]
        carry = cj[:, LANES - 1:LANES]


def _fgate(rest3, bias_row):
    b, s, _ = rest3.shape
    tri = (np.arange(LANES)[:, None] <= np.arange(LANES)[None, :]).astype(np.float32)
    sel = (np.arange(8)[:, None] == np.arange(LANES)[None, :]).astype(np.float32)
    sel[FOX_HEADS:] = 0.0
    return pl.pallas_call(
        _fgate_kernel,
        grid=(b,),
        in_specs=[pl.BlockSpec((1, s, LANES), lambda i: (i, 0, REST_SMALL // LANES)),
                  pl.BlockSpec((1, LANES), lambda i: (0, 0)),
                  pl.BlockSpec((8, LANES), lambda i: (0, 0)),
                  pl.BlockSpec((LANES, LANES), lambda i: (0, 0))],
        out_specs=pl.BlockSpec((1, FOX_HEADS, 1, s), lambda i: (i, 0, 0, 0)),
        out_shape=jax.ShapeDtypeStruct((b, FOX_HEADS, 1, s), F32),
        compiler_params=_params("parallel"),
        name="fgate",
    )(rest3, bias_row, jnp.asarray(sel), jnp.asarray(tri))


def _fox_kernel(q_ref, k_ref, v_ref, c_ref, o_ref, sa_ref, sb_ref, *, tq, tk):
    i = pl.program_id(2)
    lane = lax.broadcasted_iota(jnp.int32, (1, LANES), 1)
    first = lane < HEAD64
    q = q_ref[0]
    zero = jnp.zeros_like(q)
    qh = (jnp.where(first, q, zero), jnp.where(first, zero, q))
    nfull = (i * tq) // tk
    cbase = [c_ref[0, hh, nfull][:, 0:1] for hh in range(2)]
    qpos = i * tq + lax.broadcasted_iota(jnp.int32, (tq, 1), 0)
    den = (HEAD64, 0)
    lane_v = lax.broadcasted_iota(jnp.int32, (tk, LANES), 1)
    keep = (lane_v < HEAD64, lane_v >= HEAD64)
    ones_col = tuple(jnp.where(lane_v == d, 1.0, 0.0).astype(BF16) for d in den)

    def scores(j, s_ref):
        k0 = pl.multiple_of(j * tk, tk)
        k = k_ref[0, pl.ds(k0, tk), :]
        for hh in range(2):
            s = lax.dot_general(qh[hh], k, (((1,), (1,)), ((), ())), preferred_element_type=F32)
            s_ref[hh] = s + (cbase[hh] - c_ref[0, hh, j]) * LOG2E

    def update(j, s_ref, carry, masked):
        k0 = pl.multiple_of(j * tk, tk)
        v = v_ref[0, pl.ds(k0, tk), :]
        vh = tuple(jnp.where(keep[hh], v, ones_col[hh]) for hh in range(2))
        out = []
        for hh in range(2):
            m, acc = carry[hh]
            s = s_ref[hh]
            if masked:
                kpos = j * tk + lax.broadcasted_iota(jnp.int32, (1, tk), 1)
                s = jnp.where(kpos <= qpos, s, -jnp.inf)
            m_new = jnp.maximum(m, jnp.max(s, axis=-1, keepdims=True))
            alpha = jnp.exp2(m - m_new)
            p = jnp.exp2(s - m_new)
            acc = alpha * acc + jnp.dot(p.astype(BF16), vh[hh], preferred_element_type=F32)
            out.append((m_new, acc))
        return tuple(out)

    def pair(jj, carry):
        j = 2 * jj
        scores(j + 1, sb_ref)
        carry = update(j, sa_ref, carry, False)
        scores(j + 2, sa_ref)
        return update(j + 1, sb_ref, carry, False)

    def tail_even(carry):
        return update(nfull, sa_ref, carry, True)

    def tail_odd(carry):
        scores(nfull, sb_ref)
        carry = update(nfull - 1, sa_ref, carry, False)
        return update(nfull, sb_ref, carry, True)

    init = tuple((jnp.full((tq, 1), -jnp.inf, F32), jnp.zeros((tq, LANES), F32)) for _ in range(2))
    scores(0, sa_ref)
    carry = lax.fori_loop(0, nfull // 2, pair, init)
    carry = lax.cond(nfull % 2 == 1, tail_odd, tail_even, carry)
    acc0, acc1 = carry[0][1], carry[1][1]
    o0 = acc0 / acc0[:, den[0]:den[0] + 1]
    o1 = acc1 / acc1[:, den[1]:den[1] + 1]
    o_ref[0] = jnp.where(first, o0, o1).astype(BF16)


def _fox(qkv, c, tq=512, tk=512):
    b, s, _ = qkv.shape
    npair = FOX_HEADS // 2
    kern = functools.partial(_fox_kernel, tq=tq, tk=tk)
    return pl.pallas_call(
        kern,
        grid=(b, npair, s // tq),
        in_specs=[pl.BlockSpec((1, tq, LANES), lambda bi, p, i: (bi, i, p)),
                  pl.BlockSpec((1, s, LANES), lambda bi, p, i: (bi, 0, npair + p)),
                  pl.BlockSpec((1, s, LANES), lambda bi, p, i: (bi, 0, 2 * npair + p)),
                  pl.BlockSpec((1, 2, s // tk, 1, tk), lambda bi, p, i: (bi, p, 0, 0, 0))],
        out_specs=pl.BlockSpec((1, tq, LANES), lambda bi, p, i: (bi, i, p)),
        out_shape=jax.ShapeDtypeStruct((b, s, FOX_WIDTH), BF16),
        scratch_shapes=[pltpu.VMEM((2, tq, tk), F32), pltpu.VMEM((2, tq, tk), F32)],
        compiler_params=_params("parallel", "parallel", "arbitrary"),
        name="fox",
    )(qkv, qkv, qkv, c)


def _ret_kernel(q_ref, k_ref, v_ref, g_ref, cos_ref, sin_ref, dmat_ref, qdec_ref, kdec_ref, cd_ref, bm_ref,
                o_ref, state_ref, *, ts):
    @pl.when(pl.program_id(2) == 0)
    def _():
        state_ref[...] = jnp.zeros_like(state_ref)

    lane = lax.broadcasted_iota(jnp.int32, (1, LANES), 1)
    q_first = (lane % HEAD64) < (HEAD64 // 2)
    v_first = lane < HEAD64
    c_len = RET_CHUNK
    for c in range(ts // c_len):
        rows = slice(c * c_len, (c + 1) * c_len)
        cos = cos_ref[rows, :]
        sin = sin_ref[rows, :]
        q = q_ref[0, rows, :]
        k = k_ref[0, rows, :]
        v = v_ref[0, rows, :]
        qr = q * cos + pltpu.roll(q, LANES // 2, 1) * sin
        kr = k * cos + pltpu.roll(k, LANES // 2, 1) * sin
        q2 = jnp.concatenate([jnp.where(q_first, qr, 0.0), jnp.where(q_first, 0.0, qr)], axis=0)
        s = _mm_nt(q2, kr * (HEAD64 ** -0.5))
        s0 = s[:c_len] * dmat_ref[0]
        s1 = s[c_len:] * dmat_ref[1]
        o = _mm(s0, jnp.where(v_first, v, 0.0)) + _mm(s1, jnp.where(v_first, 0.0, v))
        state = state_ref[...]
        o = o + _mm(qr * qdec_ref[0], state)
        kv = _mm_tn(kr * kdec_ref[0], v)
        state_ref[...] = state * cd_ref[0] + kv * bm_ref[0]
        sq = o * o
        ms0 = jnp.sum(jnp.where(v_first, sq, 0.0), axis=-1, keepdims=True)
        ms1 = jnp.sum(jnp.where(v_first, 0.0, sq), axis=-1, keepdims=True)
        ms = jnp.where(v_first, ms0, ms1) * (1.0 / HEAD64)
        y = o * lax.rsqrt(ms + NORM_EPS) * _silu(g_ref[0, rows, :])
        o_ref[0, rows, :] = y.astype(BF16)


def _ret_tables(s):
    npair = RET_HEADS // 2
    half = HEAD64 // 2
    lane = np.arange(LANES)
    log_g = np.log1p(-np.exp2(-5.0 - np.arange(RET_HEADS, dtype=np.float32))).astype(np.float32)
    idx = np.arange(RET_CHUNK, dtype=np.float32)
    rel = idx[:, None] - idx[None, :]
    dmat = np.where(rel[None] >= 0, np.exp(np.maximum(rel, 0.0)[None] * log_g[:, None, None]), 0.0)
    qdec, kdec, cd, bm = [], [], [], []
    for p in range(npair):
        hq = 2 * p + ((lane % HEAD64) >= half)
        hv = 2 * p + (lane >= HEAD64)
        qdec.append(np.exp((idx[:, None] + 1.0) * log_g[hq][None, :]))
        kdec.append(np.exp((RET_CHUNK - 1 - idx)[:, None] * log_g[hq][None, :]) * HEAD64 ** -0.5)
        cd.append(np.broadcast_to(np.exp(RET_CHUNK * log_g[hq])[:, None], (LANES, LANES)))
        bm.append((hq[:, None] == hv[None, :]).astype(np.float32))
    tabs = [np.stack(a).astype(np.float32) for a in (qdec, kdec, cd, bm)]
    inv = 1.0 / (RET_ANGLE_BASE ** jnp.linspace(0.0, 1.0, half, dtype=F32))
    pos = jnp.arange(s, dtype=F32)
    ang = pos[:, None] * inv[None, :]
    cos = jnp.tile(jnp.cos(ang), (1, LANES // half))
    sin = jnp.tile(jnp.sin(ang), (1, LANES // half))
    sin = jnp.where(jnp.asarray(lane)[None, :] < LANES // 2, -sin, sin)
    return [jnp.asarray(dmat.astype(np.float32))] + [jnp.asarray(a) for a in tabs] + [cos, sin]


def _ret(rest3, tables, ts=1024):
    b, s, _ = rest3.shape
    dmat, qdec, kdec, cd, bm, cos, sin = tables
    npair = RET_HEADS // 2
    col = lambda off: (lambda bi, p, i: (bi, i, off // LANES + p))
    tab = lambda bi, p, i: (p, 0, 0)
    kern = functools.partial(_ret_kernel, ts=ts)
    return pl.pallas_call(
        kern,
        grid=(b, npair, s // ts),
        in_specs=[pl.BlockSpec((1, ts, LANES), col(REST_RQ)),
                  pl.BlockSpec((1, ts, LANES), col(REST_RK)),
                  pl.BlockSpec((1, ts, LANES), col(REST_RV)),
                  pl.BlockSpec((1, ts, LANES), col(REST_RG)),
                  pl.BlockSpec((ts, LANES), lambda bi, p, i: (i, 0)),
                  pl.BlockSpec((ts, LANES), lambda bi, p, i: (i, 0)),
                  pl.BlockSpec((2, RET_CHUNK, RET_CHUNK), tab),
                  pl.BlockSpec((1, RET_CHUNK, LANES), tab),
                  pl.BlockSpec((1, RET_CHUNK, LANES), tab),
                  pl.BlockSpec((1, LANES, LANES), tab),
                  pl.BlockSpec((1, LANES, LANES), tab)],
        out_specs=pl.BlockSpec((1, ts, LANES), lambda bi, p, i: (bi, i, p)),
        out_shape=jax.ShapeDtypeStruct((b, s, RET_WIDTH), BF16),
        scratch_shapes=[pltpu.VMEM((LANES, LANES), F32)],
        compiler_params=_params("parallel", "parallel", "arbitrary"),
        name="retention",
    )(rest3, rest3, rest3, rest3, cos, sin, dmat, qdec, kdec, cd, bm)


def _gdn_prep_kernel(q_ref, k_ref, v_ref, sm_ref, wq_ref, wk_ref, wv_ref, alog_ref, dtb_ref,
                     u0_ref, w_ref, qg_ref, kt_ref, at_ref, eg_ref):
    h = pl.program_id(1)
    c_len = GDN_CHUNK
    n_chunks = q_ref.shape[1] // c_len
    lane = lax.broadcasted_iota(jnp.int32, (1, LANES), 1)
    ri = lax.broadcasted_iota(jnp.int32, (c_len, c_len), 0)
    ci = lax.broadcasted_iota(jnp.int32, (c_len, c_len), 1)
    incl = ri >= ci
    strict = ri > ci
    eye = (ri == ci).astype(F32)
    neg_a = -jnp.exp(alog_ref[...])
    dtb = dtb_ref[...]

    grp = GDN_PREP_GROUP
    rows = grp * c_len

    def conv_silu(ref, w_ref_, n, r0):
        cur = ref[0, pl.ds(r0, rows), :]
        p0 = pl.multiple_of(jnp.maximum(r0 - 8, 0), 8)
        prev = ref[0, pl.ds(p0, 8), :]
        prev = jnp.where(jnp.broadcast_to(n > 0, prev.shape), prev, 0.0)
        xc = jnp.concatenate([prev, cur], axis=0)
        w = w_ref_[...]
        y = cur * w[CONV_WIDTH - 1:CONV_WIDTH, :]
        for j in range(CONV_WIDTH - 1):
            shifted = pltpu.roll(xc, CONV_WIDTH - 1 - j, 0)[8:, :]
            y = y + shifted * w[j:j + 1, :]
        return _silu(y)

    def group(n, carry):
        r0 = pl.multiple_of(n * rows, rows)
        cq = conv_silu(q_ref, wq_ref, n, r0)
        ck = conv_silu(k_ref, wk_ref, n, r0)
        cv = conv_silu(v_ref, wv_ref, n, r0)
        qn = cq * lax.rsqrt(jnp.sum(cq * cq, axis=-1, keepdims=True) + NORM_EPS) * (GDN_DIM ** -0.5)
        kn = ck * lax.rsqrt(jnp.sum(ck * ck, axis=-1, keepdims=True) + NORM_EPS)
        sm = sm_ref[0, pl.ds(r0, rows), :]
        z = sm + dtb
        g_all = neg_a * (jnp.maximum(z, 0.0) + jnp.log1p(jnp.exp(-jnp.abs(z))))
        beta_all = _sigmoid(sm)
        g_col = jnp.sum(jnp.where(lane == SMALL_GA + h, g_all, 0.0), axis=-1, keepdims=True)
        beta = jnp.sum(jnp.where(lane == SMALL_GB + h, beta_all, 0.0), axis=-1, keepdims=True)
        kb = kn * beta
        vb = cv * beta
        chunks = [slice(g * c_len, (g + 1) * c_len) for g in range(grp)]
        g_row = [jnp.sum(g_col[c] * eye, axis=0, keepdims=True) for c in chunks]
        gc = [jnp.sum(jnp.where(incl, g_row[i], 0.0), axis=-1, keepdims=True) for i in range(grp)]
        gc_row = [jnp.sum(jnp.where(ri <= ci, g_col[c], 0.0), axis=0, keepdims=True) for c in chunks]
        decay = [jnp.where(incl, jnp.exp(jnp.where(incl, gc[i] - gc_row[i], 0.0)), 0.0) for i in range(grp)]
        both = [_mm_nt(jnp.concatenate([kb[c], qn[c]], axis=0), kn[c]) for c in chunks]
        low = [jnp.where(strict, both[i][:c_len] * decay[i], 0.0) for i in range(grp)]
        attn = [jnp.where(incl, both[i][c_len:] * decay[i], 0.0) for i in range(grp)]
        inv = [eye - low[i] for i in range(grp)]
        pw = low
        for _ in range(int(math.log2(c_len)) - 1):
            pw = [_mm(pw[i], pw[i]) for i in range(grp)]
            inv = [inv[i] + _mm(inv[i], pw[i]) for i in range(grp)]
        eg = [jnp.exp(gc[i]) for i in range(grp)]
        sol = [_mm(inv[i], jnp.concatenate([vb[c], kb[c] * eg[i]], axis=1)) for i, c in enumerate(chunks)]
        for i, c in enumerate(chunks):
            dst = pl.ds(pl.multiple_of(r0 + i * c_len, c_len), c_len)
            g_last = gc[i][c_len - 1:c_len, :]
            u0_ref[0, 0, dst, :] = sol[i][:, :GDN_DIM]
            w_ref[0, 0, dst, :] = sol[i][:, GDN_DIM:].astype(BF16)
            qg_ref[0, 0, dst, :] = (qn[c] * eg[i]).astype(BF16)
            kt_ref[0, 0, dst, :] = (kn[c] * jnp.exp(g_last - gc[i])).astype(BF16)
            at_ref[0, 0, dst, :] = attn[i].astype(BF16)
            eg_ref[0, 0, n * grp + i] = jnp.broadcast_to(jnp.exp(g_last), (1, LANES))
        return carry

    lax.fori_loop(0, n_chunks // grp, group, 0)


def _gdn_prep(rest3, conv_w, alog_l, dtb_l):
    b, s, _ = rest3.shape
    nh = GDN_HEADS
    n_chunks = s // GDN_CHUNK
    col = lambda off: (lambda bi, h: (bi, 0, off // LANES + h))
    wcol = lambda g: (lambda bi, h: (0, g * nh + h))
    const = lambda bi, h: (0, 0)
    row =pl.BlockSpec((1, 1, s, GDN_DIM), lambda bi, h: (bi, h, 0, 0))
    return pl.pallas_call(
        _gdn_prep_kernel,
        grid=(b, nh),
        in_specs=[pl.BlockSpec((1, s, LANES), col(REST_GQ)),
                  pl.BlockSpec((1, s, LANES), col(REST_GK)),
                  pl.BlockSpec((1, s, LANES), col(REST_GV)),
                  pl.BlockSpec((1, s, LANES), lambda bi, h: (bi, 0, REST_SMALL // LANES)),
                  pl.BlockSpec((CONV_WIDTH, LANES), wcol(0)),
                  pl.BlockSpec((CONV_WIDTH, LANES), wcol(1)),
                  pl.BlockSpec((CONV_WIDTH, LANES), wcol(2)),
                  pl.BlockSpec((1, LANES), const),
                  pl.BlockSpec((1, LANES), const)],
        out_specs=[row, row, row, row,
                   pl.BlockSpec((1, 1, s, GDN_CHUNK), lambda bi, h: (bi, h, 0, 0)),
                   pl.BlockSpec((1, 1, n_chunks, 1, LANES), lambda bi, h: (bi, h, 0, 0, 0))],
        out_shape=[jax.ShapeDtypeStruct((b, nh, s, GDN_DIM), F32),
                   jax.ShapeDtypeStruct((b, nh, s, GDN_DIM), BF16),
                   jax.ShapeDtypeStruct((b, nh, s, GDN_DIM), BF16),
                   jax.ShapeDtypeStruct((b, nh, s, GDN_DIM), BF16),
                   jax.ShapeDtypeStruct((b, nh, s, GDN_CHUNK), BF16),
                   jax.ShapeDtypeStruct((b, nh, n_chunks, 1, LANES), F32)],
        compiler_params=_params("parallel", "parallel"),
        name="gdn_prep",
    )(rest3, rest3, rest3, rest3, conv_w, conv_w, conv_w, alog_l, dtb_l)


def _gdn_scan_kernel(u0_ref, w_ref, qg_ref, kt_ref, at_ref, eg_ref, z_ref, nw_ref, o_ref, state_ref, *, ts):
    si = pl.program_id(1)

    @pl.when(si == 0)
    def _():
        state_ref[...] = jnp.zeros_like(state_ref)

    c_len = GDN_CHUNK
    per_tile = ts // c_len
    nw = nw_ref[...]
    for c in range(per_tile):
        rows = slice(c * c_len, (c + 1) * c_len)
        heads = range(GDN_HEADS)
        st = [state_ref[h] for h in heads]
        r = [_mm(jnp.concatenate([w_ref[0, h, rows, :], qg_ref[0, h, rows, :]], axis=0), st[h]) for h in heads]
        u = [(u0_ref[0, h, rows, :] - r[h][:c_len]).astype(BF16) for h in heads]
        ku = [_mm_tn(kt_ref[0, h, rows, :], u[h]) for h in heads]
        au = [jnp.dot(at_ref[0, h, rows, :], u[h], preferred_element_type=F32) for h in heads]
        for h in heads:
            state_ref[h] = st[h] * eg_ref[0, h, si * per_tile + c] + ku[h]
        for h in heads:
            cols = slice(h * GDN_DIM, (h + 1) * GDN_DIM)
            y = _rms(r[h][c_len:] + au[h], nw) * _silu(z_ref[0, rows, cols])
            o_ref[0, rows, cols] = y.astype(BF16)


def _gdn_scan(u0, w, qg, kt, at, eg, rest3, norm_w, ts=512):
    b, nh, s, _ = u0.shape
    n_chunks = s // GDN_CHUNK
    blk = lambda d: pl.BlockSpec((1, nh, ts, d), lambda bi, i: (bi, 0, i, 0))
    kern = functools.partial(_gdn_scan_kernel, ts=ts)
    return pl.pallas_call(
        kern,
        grid=(b, s // ts),
        in_specs=[blk(GDN_DIM), blk(GDN_DIM), blk(GDN_DIM), blk(GDN_DIM), blk(GDN_CHUNK),
                  pl.BlockSpec((1, nh, n_chunks, 1, LANES), lambda bi, i: (bi, 0, 0, 0, 0)),
                  pl.BlockSpec((1, ts, GDN_WIDTH), lambda bi, i: (bi, i, REST_GZ // GDN_WIDTH)),
                  pl.BlockSpec((1, GDN_DIM), lambda bi, i: (0, 0))],
        out_specs=pl.BlockSpec((1, ts, GDN_WIDTH), lambda bi, i: (bi, i, 0)),
        out_shape=jax.ShapeDtypeStruct((b, s, GDN_WIDTH), BF16),
        scratch_shapes=[pltpu.VMEM((nh, GDN_DIM, GDN_DIM), F32)],
        compiler_params=_params("parallel", "arbitrary"),
        name="gdn_scan",
    )(u0, w, qg, kt, at, eg, rest3, norm_w)


def _outproj_kernel(x_ref, of_ref, or_ref, og_ref, w_ref, o_ref):
    acc = x_ref[...]
    acc = acc + jnp.dot(of_ref[...], w_ref[0:FOX_WIDTH, :], preferred_element_type=F32)
    acc = acc + jnp.dot(or_ref[...], w_ref[FOX_WIDTH:FOX_WIDTH + RET_WIDTH, :], preferred_element_type=F32)
    acc = acc + jnp.dot(og_ref[...], w_ref[FOX_WIDTH + RET_WIDTH:, :], preferred_element_type=F32)
    o_ref[...] = acc


def _outproj(x, o_fox, o_ret, o_gdn, w_out, tm=512):
    t = x.shape[0]
    d_mix = w_out.shape[0]
    return pl.pallas_call(
        _outproj_kernel,
        grid=(t // tm,),
        in_specs=[pl.BlockSpec((tm, D_MODEL), lambda i: (i, 0)),
                  pl.BlockSpec((tm, FOX_WIDTH), lambda i: (i, 0)),
                  pl.BlockSpec((tm, RET_WIDTH), lambda i: (i, 0)),
                  pl.BlockSpec((tm, GDN_WIDTH), lambda i: (i, 0)),
                  pl.BlockSpec((d_mix, D_MODEL), lambda i: (0, 0))],
        out_specs=pl.BlockSpec((tm, D_MODEL), lambda i: (i, 0)),
        out_shape=jax.ShapeDtypeStruct((t, D_MODEL), F32),
        compiler_params=_params("parallel"),
        name="outproj",
    )(x, o_fox, o_ret, o_gdn, w_out)


def _router_kernel(x_ref, nw_ref, w_ref, b_ref, tri_ref, sel_ref, route_ref, idx_ref, cnt_ref, carry_ref):
    @pl.when(pl.program_id(0) == 0)
    def _():
        carry_ref[...] = jnp.zeros_like(carry_ref)

    tm = x_ref.shape[0]
    hn = _rms(x_ref[...], nw_ref[...])
    logits = jnp.dot(hn, w_ref[...], precision=HIGHEST, preferred_element_type=F32) + b_ref[...]
    lane = lax.broadcasted_iota(jnp.int32, logits.shape, 1).astype(F32)
    neg = -jnp.inf
    gl = jnp.where(lane < ROUTER_EXP, logits, neg)
    gmax = jnp.max(gl, axis=-1, keepdims=True)
    gidx = jnp.min(jnp.where(gl == gmax, lane, LANES), axis=-1, keepdims=True)
    grp_p = 1.0 / jnp.sum(jnp.exp(gl - gmax), axis=-1, keepdims=True)
    lo = ROUTER_EXP + gidx * EXPERTS_PER_GROUP
    el = jnp.where((lane >= lo) & (lane < lo + EXPERTS_PER_GROUP), logits, neg)
    m1 = jnp.max(el, axis=-1, keepdims=True)
    i1 = jnp.min(jnp.where(el == m1, lane, LANES), axis=-1, keepdims=True)
    el2 = jnp.where(lane == i1, neg, el)
    m2 = jnp.max(el2, axis=-1, keepdims=True)
    i2 = jnp.min(jnp.where(el2 == m2, lane, LANES), axis=-1, keepdims=True)
    e2 = jnp.exp(m2 - m1)
    w1 = grp_p / (1.0 + e2)
    ea = jnp.minimum(i1, i2) - ROUTER_EXP
    eb = jnp.maximum(i1, i2) - ROUTER_EXP
    first_low = i1 < i2
    ga = jnp.where(first_low, w1, w1 * e2)
    gb = jnp.where(first_low, w1 * e2, w1)
    la = ea - gidx * EXPERTS_PER_GROUP
    lb = eb - gidx * EXPERTS_PER_GROUP
    cls = gidx * N_PAIRS + la * (2 * EXPERTS_PER_GROUP - 1 - la) * 0.5 + (lb - la - 1.0)
    oh = (lane == cls).astype(F32)
    cum = jnp.dot(tri_ref[...], oh.astype(BF16), preferred_element_type=F32) + carry_ref[...]
    rank = jnp.sum(oh * (cum - oh), axis=-1, keepdims=True)
    carry_ref[...] = cum[tm - 1:tm, :]
    cnt_ref[...] = cum[tm - 1:tm, :]
    out = jnp.zeros_like(logits)
    for col, val in ((ROUTE_CLASS, cls), (ROUTE_RANK, rank), (ROUTE_GATE, ga), (ROUTE_GATE + 1, gb)):
        out = jnp.where(lane == col, val, out)
    route_ref[...] = out
    idx = lax.dot_general(sel_ref[...], out, (((1,), (1,)), ((), ())), precision=HIGHEST,
                          preferred_element_type=F32)
    idx_ref[...] = idx.astype(jnp.int32)


def _router(x, nw, w_pack, b_pack, tm=512):
    t = x.shape[0]
    tri = jnp.asarray((np.arange(tm)[:, None] >= np.arange(tm)[None, :]).astype(np.float32), dtype=BF16)
    sel = np.zeros((8, LANES), np.float32)
    for row, lane in enumerate((ROUTE_CLASS, ROUTE_RANK)):
        sel[row, lane] = 1.0
    return pl.pallas_call(
        _router_kernel,
        grid=(t // tm,),
        in_specs=[pl.BlockSpec((tm, D_MODEL), lambda i: (i, 0)),
                  pl.BlockSpec((1, D_MODEL), lambda i: (0, 0)),
                  pl.BlockSpec((D_MODEL, LANES), lambda i: (0, 0)),
                  pl.BlockSpec((1, LANES), lambda i: (0, 0)),
                  pl.BlockSpec((tm, tm), lambda i: (0, 0)),
                  pl.BlockSpec((8, LANES), lambda i: (0, 0))],
        out_specs=[pl.BlockSpec((tm, LANES), lambda i: (i, 0)),
                   pl.BlockSpec((8, tm), lambda i: (0, i)),
                   pl.BlockSpec((1, LANES), lambda i: (0, 0))],
        out_shape=[jax.ShapeDtypeStruct((t, LANES), F32),
                   jax.ShapeDtypeStruct((8, t), jnp.int32),
                   jax.ShapeDtypeStruct((1, LANES), F32)],
        scratch_shapes=[pltpu.VMEM((1, LANES), F32)],
        compiler_params=_params("arbitrary"),
        name="router",
    )(x, nw, w_pack, b_pack, tri, jnp.asarray(sel))


def _round_up_tile(v):
    shift = MOE_TILE.bit_length() - 1
    return lax.shift_left(lax.shift_right_logical(v + (MOE_TILE - 1), shift), shift)


def _dispatch_kernel(off_ref, cnt_ref, tot_ref, pos_ref, x_ref, route_ref, xs_ref, ring_ref, zero_ref, sem_ref, *,
                     tm, n_tiles):
    i = pl.program_id(0)
    n = pl.num_programs(0)
    slot = i % 2

    def row_copy(s, r, p):
        return pltpu.make_async_copy(ring_ref.at[s, pl.ds(r, 1)], xs_ref.at[pl.ds(p, 1)], sem_ref.at[s])

    def pad_rows(p, size):
        return pltpu.make_async_copy(zero_ref.at[pl.ds(0, size)], xs_ref.at[pl.ds(p, size)], sem_ref.at[2])

    def pad_tile(j):
        return pltpu.make_async_copy(zero_ref, xs_ref.at[pl.ds(j * MOE_TILE, MOE_TILE)], sem_ref.at[3])

    def pad_class(c, start):
        lo = cnt_ref[c]
        lo8 = lax.shift_left(lax.shift_right_logical(lo + 7, 3), 3)
        rem = _round_up_tile(lo) - lo8

        def single(r, carry):
            cp = pad_rows(off_ref[c] + r, 1)
            cp.start() if start else cp.wait()
            return carry

        lax.fori_loop(lo, lo8, single, 0)
        for bit in range(3, MOE_TILE.bit_length() - 1):
            size = 1 << bit

            @pl.when(lax.bitwise_and(rem, size) != 0)
            def _():
                above = lax.shift_left(lax.shift_right_logical(rem, bit + 1), bit + 1)
                cp = pad_rows(pl.multiple_of(off_ref[c] + lo8 + above, 8), size)
                cp.start() if start else cp.wait()

    @pl.when(i == 0)
    def _():
        zero_ref[...] = jnp.zeros_like(zero_ref)

        def start_class(c, carry):
            pad_class(c, True)
            return carry

        def wait_class(c, carry):
            pad_class(c, False)
            return carry

        def start_tile(j, c):
            pad_tile(j).start()
            return c

        def wait_tile(j, c):
            pad_tile(0).wait()
            return c

        lax.fori_loop(0, N_CLASSES, start_class, 0)
        lax.fori_loop(tot_ref[0], n_tiles, start_tile, 0)
        lax.fori_loop(0, N_CLASSES, wait_class, 0)
        lax.fori_loop(tot_ref[0], n_tiles, wait_tile, 0)

    ring_ref[slot, :, :D_MODEL] = x_ref[...]
    ring_ref[slot, :, D_MODEL:] = route_ref[...]

    def issue(g, c):
        for u in range(2):
            r = 2 * g + u
            row_copy(slot, r, pos_ref[0, r]).start(priority=u)
        return c

    lax.fori_loop(0, tm // 2, issue, 0, unroll=4)

    def drain(s):
        def wait(r, c):
            row_copy(s, 0, 0).wait()
            return c

        lax.fori_loop(0, tm, wait, 0, unroll=8)

    @pl.when(i > 0)
    def _():
        drain(1 - slot)

    @pl.when(i == n - 1)
    def _():
        drain(slot)


def _dispatch(x, route, off, cnt, total, pos, tm, n_tiles):
    t = x.shape[0]
    kern = functools.partial(_dispatch_kernel, tm=tm, n_tiles=n_tiles)
    return pl.pallas_call(
        kern,
        grid_spec=pltpu.PrefetchScalarGridSpec(
            num_scalar_prefetch=3,
            grid=(t // tm,),
            in_specs=[pl.BlockSpec((1, tm), lambda i, *_: (0, i), memory_space=pltpu.SMEM),
                      pl.BlockSpec((tm, D_MODEL), lambda i, *_: (i, 0)),
                      pl.BlockSpec((tm, LANES), lambda i, *_: (i, 0))],
            out_specs=pl.BlockSpec(memory_space=pl.ANY),
            scratch_shapes=[pltpu.VMEM((2, tm, ROW_WIDTH), F32),
                            pltpu.VMEM((MOE_TILE, ROW_WIDTH), F32),
                            pltpu.SemaphoreType.DMA((4,))]),
        out_shape=jax.ShapeDtypeStruct((n_tiles * MOE_TILE, ROW_WIDTH), F32),
        compiler_params=_params("arbitrary"),
        name="dispatch",
    )(off, cnt, total, pos, x, route)


def _ffn_kernel(ta_ref, tb_ref, fa_ref, fb_ref, tot_ref, xs_ref, nw_ref, w1a_ref, w3a_ref, w2a_ref,
                w1b_ref, w3b_ref, w2b_ref, ys_ref, w1a, w3a, w2a, w1b, w3b, w2b):
    j = pl.program_id(0)

    @pl.when(fa_ref[j] == 1)
    def _():
        w1a[...] = w1a_ref[0].astype(BF16)
        w3a[...] = w3a_ref[0].astype(BF16)
        w2a[...] = w2a_ref[0].astype(BF16)

    @pl.when(fb_ref[j] == 1)
    def _():
        w1b[...] = w1b_ref[0].astype(BF16)
        w3b[...] = w3b_ref[0].astype(BF16)
        w2b[...] = w2b_ref[0].astype(BF16)

    @pl.when(j < tot_ref[0])
    def _():
        x = xs_ref[:, :D_MODEL]
        ga = xs_ref[:, D_MODEL + ROUTE_GATE:D_MODEL + ROUTE_GATE + 1]
        gb = xs_ref[:, D_MODEL + ROUTE_GATE + 1:D_MODEL + ROUTE_GATE + 2]
        hn = _rms(x, nw_ref[...]).astype(BF16)

        def expert(w1, w3, w2):
            a = jnp.dot(hn, w1[...], preferred_element_type=F32)
            u = jnp.dot(hn, w3[...], preferred_element_type=F32)
            return jnp.dot((_silu(a) * u).astype(BF16), w2[...], preferred_element_type=F32)

        ys_ref[...] = x + ga * expert(w1a, w3a, w2a) + gb * expert(w1b, w3b, w2b)

    @pl.when(j >= tot_ref[0])
    def _():
        ys_ref[...] = jnp.zeros_like(ys_ref)


def _ffn(xs, nw, w1, w3, w2, tile_a, tile_b, first_a, first_b, total):
    n_tiles = tile_a.shape[0]
    tok = lambda j, *_: (j, 0)
    wa = lambda j, ta, tb, fa, fb, tot: (ta[j], 0, 0)
    wb = lambda j, ta, tb, fa, fb, tot: (tb[j], 0, 0)
    up = lambda imap: pl.BlockSpec((1, D_MODEL, EXPERT_FF), imap)
    down = lambda imap: pl.BlockSpec((1, EXPERT_FF, D_MODEL), imap)
    return pl.pallas_call(
        _ffn_kernel,
        grid_spec=pltpu.PrefetchScalarGridSpec(
            num_scalar_prefetch=5,
            grid=(n_tiles,),
            in_specs=[pl.BlockSpec((MOE_TILE, ROW_WIDTH), tok),
                      pl.BlockSpec((1, D_MODEL), lambda j, *_: (0, 0)),
                      up(wa), up(wa), down(wa), up(wb), up(wb), down(wb)],
            out_specs=pl.BlockSpec((MOE_TILE, D_MODEL), tok),
            scratch_shapes=[pltpu.VMEM((D_MODEL, EXPERT_FF), BF16),
                            pltpu.VMEM((D_MODEL, EXPERT_FF), BF16),
                            pltpu.VMEM((EXPERT_FF, D_MODEL), BF16),
                            pltpu.VMEM((D_MODEL, EXPERT_FF), BF16),
                            pltpu.VMEM((D_MODEL, EXPERT_FF), BF16),
                            pltpu.VMEM((EXPERT_FF, D_MODEL), BF16)]),
        out_shape=jax.ShapeDtypeStruct((n_tiles * MOE_TILE, D_MODEL), F32),
        compiler_params=_params("arbitrary"),
        name="ffn",
    )(tile_a, tile_b, first_a, first_b, total, xs, nw, w1, w3, w2, w1, w3, w2)


def _gather_kernel(pos_ref, posn_ref, fw_ref, ys_ref, o_ref, ybuf, sem_ref, *, tm, final_norm):
    i = pl.program_id(0)
    n = pl.num_programs(0)
    slot = i % 2

    def row_copy(s, r, p):
        return pltpu.make_async_copy(ys_ref.at[pl.ds(p, 1)], ybuf.at[s, pl.ds(r, 1)], sem_ref.at[s])

    def issue(p_ref, s):
        def body(g, c):
            for u in range(2):
                r = 2 * g + u
                row_copy(s, r, p_ref[0, r]).start(priority=u)
            return c

        lax.fori_loop(0, tm // 2, body, 0, unroll=4)

    @pl.when(i == 0)
    def _():
        issue(pos_ref, 0)

    @pl.when(i + 1 < n)
    def _():
        issue(posn_ref, 1 - slot)

    def wait(r, c):
        row_copy(slot, 0, 0).wait()
        return c

    lax.fori_loop(0, tm, wait, 0, unroll=8)
    out = ybuf[slot]
    if final_norm:
        out = _rms(out, fw_ref[...])
    o_ref[...] = out


def _gather(ys, pos, fw, final_norm, tm):
    t = pos.shape[1]
    n = t // tm
    kern = functools.partial(_gather_kernel, tm=tm, final_norm=final_norm)
    smem = lambda imap: pl.BlockSpec((1, tm), imap, memory_space=pltpu.SMEM)
    return pl.pallas_call(
        kern,
        grid=(n,),
        in_specs=[smem(lambda i: (0, i)),
                  smem(lambda i: (0, jnp.minimum(i + 1, n - 1))),
                  pl.BlockSpec((1, D_MODEL), lambda i: (0, 0)),
                  pl.BlockSpec(memory_space=pl.ANY)],
        out_specs=pl.BlockSpec((tm, D_MODEL), lambda i: (i, 0)),
        out_shape=jax.ShapeDtypeStruct((t, D_MODEL), F32),
        scratch_shapes=[pltpu.VMEM((2, tm, D_MODEL), F32),
                        pltpu.SemaphoreType.DMA((2,))],
        compiler_params=_params("arbitrary"),
        name="gather",
    )(pos, pos, fw, ys)


def _class_experts():
    lo, hi = [], []
    for g in range(N_GROUPS):
        for la in range(EXPERTS_PER_GROUP):
            for lb in range(la + 1, EXPERTS_PER_GROUP):
                lo.append(g * EXPERTS_PER_GROUP + la)
                hi.append(g * EXPERTS_PER_GROUP + lb)
    return np.asarray(lo, np.int32), np.asarray(hi, np.int32)


def _lookup(table, index):
    k = table.shape[0]
    return jnp.sum(jnp.where(index[..., None] == jnp.arange(k, dtype=jnp.int32), table, 0), axis=-1)


def _moe(x, nw, w_pack, b_pack, w1, w3, w2, fw, final_norm, tm=512):
    t = x.shape[0]
    route, idx, cnt_row = _router(x, nw, w_pack, b_pack)
    cnt = cnt_row[0, :N_CLASSES].astype(jnp.int32)
    n_tiles = t // MOE_TILE + N_CLASSES
    nblk = (cnt + MOE_TILE - 1) // MOE_TILE
    cend = jnp.cumsum(nblk)
    total = cend[-1:]
    off = (cend - nblk) * MOE_TILE
    pos = (_lookup(off, idx[ROUTE_CLASS]) + idx[ROUTE_RANK]).reshape(1, t)
    j = jnp.arange(n_tiles, dtype=jnp.int32)
    tile_c = jnp.minimum(jnp.sum((j[:, None] >= cend[None, :]).astype(jnp.int32), axis=1), N_CLASSES - 1)
    lo, hi = _class_experts()
    tile_a = _lookup(jnp.asarray(lo), tile_c)
    tile_b = _lookup(jnp.asarray(hi), tile_c)
    valid = j < total
    changed = lambda e: (valid & ((j == 0) | (e != jnp.roll(e, 1)))).astype(jnp.int32)
    xs = _dispatch(x, route, off, cnt, total, pos, tm, n_tiles)
    ys = _ffn(xs, nw, w1, w3, w2, tile_a, tile_b, changed(tile_a), changed(tile_b), total)
    return _gather(ys, pos, fw, final_norm, tm)


def _pack_in_weights(w_in_l):
    off = np.concatenate([[0], np.cumsum(IN_SPLITS)]).tolist()
    fq, fk, fv, ff, rq, rk, rv, rg, gq, gk, gv, gz, ga, gb = [(off[i], IN_SPLITS[i]) for i in range(len(IN_SPLITS))]
    half = HEAD64 // 2

    def permuted(seg):
        return [(seg[0] + (2 * p + hh) * HEAD64 + lo * half, half)
                for p in range(RET_HEADS // 2) for lo in range(2) for hh in range(2)]

    col_scale = np.ones((1, w_in_l.shape[1]), np.float32)
    col_scale[:, fq[0]:fq[0] + fq[1]] = LOG2E * HEAD64 ** -0.5
    w_bf = (w_in_l * col_scale).astype(BF16)
    cols = lambda segs: [w_bf[:, a:a + n] for a, n in segs]
    wf = jnp.concatenate(cols([fq, fk, fv]), axis=1)
    pad = jnp.zeros((D_MODEL, LANES - (FOX_HEADS + 2 * GDN_HEADS)), BF16)
    wr = jnp.concatenate(cols(permuted(rq) + permuted(rk) + [rv, rg, gq, gk, gv, gz, ff, ga, gb]) + [pad], axis=1)
    return wf, wr


def _lane_row(vals, offset):
    return jnp.zeros((1, LANES), F32).at[0, offset:offset + vals.shape[0]].set(vals.astype(F32))


def kernel(x, norm1_w, w_in, fox_forget_bias, gdn_conv_w, gdn_a_log, gdn_dt_bias, gdn_norm_w, w_out, norm2_w,
           router_group_w, router_group_b, router_expert_w, router_expert_b, expert_w1, expert_w3, expert_w2,
           final_norm_w):
    b, s, d = x.shape
    t = b * s
    depth = w_in.shape[0]
    fox_tk = 512
    xt = x.reshape(t, d)
    ret_tables = _ret_tables(s)
    for l in range(depth):
        wf, wr = _pack_in_weights(w_in[l])
        qkv, rest = _inproj(xt, norm1_w[l].reshape(1, d), wf, wr)
        rest3 = rest.reshape(b, s, REST_WIDTH)
        c = _fgate(rest3, _lane_row(fox_forget_bias[l], SMALL_FF)).reshape(b, FOX_HEADS, s // fox_tk, 1, fox_tk)
        o_fox = _fox(qkv.reshape(b, s, 3 * FOX_WIDTH), c, tk=fox_tk)
        o_ret = _ret(rest3, ret_tables)
        prep = _gdn_prep(rest3, gdn_conv_w[l].astype(F32), _lane_row(gdn_a_log[l], SMALL_GA),
                         _lane_row(gdn_dt_bias[l], SMALL_GA))
        o_gdn = _gdn_scan(*prep, rest3, gdn_norm_w[l].reshape(1, GDN_DIM).astype(F32))
        xt = _outproj(xt, o_fox.reshape(t, FOX_WIDTH), o_ret.reshape(t, RET_WIDTH),
                      o_gdn.reshape(t, GDN_WIDTH), w_out[l].astype(BF16))
        w_pack = jnp.concatenate([router_group_w[l], router_expert_w[l],
                                  jnp.zeros((d, LANES - N_GROUPS - N_EXPERTS), F32)], axis=1)
        b_pack = jnp.concatenate([router_group_b[l].reshape(-1), router_expert_b[l].reshape(-1),
                                  jnp.zeros((LANES - N_GROUPS - N_EXPERTS,), F32)]).reshape(1, LANES)
        xt = _moe(xt, norm2_w[l].reshape(1, d), w_pack, b_pack,
                  expert_w1[l].reshape(N_EXPERTS, d, EXPERT_FF),
                  expert_w3[l].reshape(N_EXPERTS, d, EXPERT_FF),
                  expert_w2[l].reshape(N_EXPERTS, EXPERT_FF, d),
                  final_norm_w.reshape(1, d), final_norm=(l == depth - 1))
    return xt.reshape(b, s, d)
```

```python
import functools
import math

import jax
import jax.numpy as jnp
import numpy as np
from jax import lax
from jax.experimental import pallas as pl
from jax.experimental.pallas import tpu as pltpu

F32 = jnp.float32
BF16 = jnp.bfloat16
HIGHEST = lax.Precision.HIGHEST

D_MODEL = 1024
FOX_HEADS = 4
RET_HEADS = 4
GDN_HEADS = 4
HEAD64 = 64
GDN_DIM = 128
FOX_WIDTH = FOX_HEADS * HEAD64
RET_WIDTH = RET_HEADS * HEAD64
GDN_WIDTH = GDN_HEADS * GDN_DIM
RET_CHUNK = 128
GDN_CHUNK = 64
GDN_PREP_GROUP = 8
CONV_WIDTH = 4
RET_ANGLE_BASE = 10000.0
N_GROUPS = 4
EXPERTS_PER_GROUP = 8
N_EXPERTS = N_GROUPS * EXPERTS_PER_GROUP
EXPERT_FF = 256
NORM_EPS = 1e-6
LOG2E = math.log2(math.e)
LANES = 128
IN_SPLITS = (FOX_WIDTH, FOX_WIDTH, FOX_WIDTH, FOX_HEADS,
             RET_WIDTH, RET_WIDTH, RET_WIDTH, RET_WIDTH,
             GDN_WIDTH, GDN_WIDTH, GDN_WIDTH, GDN_WIDTH, GDN_HEADS, GDN_HEADS)

REST_RQ, REST_RK, REST_RV, REST_RG = 0, 256, 512, 768
REST_GQ, REST_GK, REST_GV, REST_GZ = 1024, 1536, 2048, 2560
REST_SMALL = 3072
REST_WIDTH = 3200
SMALL_FF, SMALL_GA, SMALL_GB = 0, 4, 8
ROUTER_GRP, ROUTER_EXP = 0, 4
ROUTE_CLASS, ROUTE_RANK, ROUTE_GATE = 0, 1, 2
N_PAIRS = EXPERTS_PER_GROUP * (EXPERTS_PER_GROUP - 1) // 2
N_CLASSES = N_GROUPS * N_PAIRS
MOE_TILE = 256
ROW_WIDTH = D_MODEL + LANES

VMEM_LIMIT = 56 * 1024 * 1024


def _params(*sem):
    return pltpu.CompilerParams(dimension_semantics=sem, vmem_limit_bytes=VMEM_LIMIT)


def _mm(a, b):
    return jnp.dot(a.astype(BF16), b.astype(BF16), preferred_element_type=F32)


def _mm_nt(a, b):
    return lax.dot_general(a.astype(BF16), b.astype(BF16), (((1,), (1,)), ((), ())),
                           preferred_element_type=F32)


def _mm_tn(a, b):
    return lax.dot_general(a.astype(BF16), b.astype(BF16), (((0,), (0,)), ((), ())),
                           preferred_element_type=F32)


def _sigmoid(x):
    return 0.5 + 0.5 * jnp.tanh(0.5 * x)


def _silu(x):
    return x * _sigmoid(x)


def _rms(x, w):
    return x * lax.rsqrt(jnp.mean(x * x, axis=-1, keepdims=True) + NORM_EPS) * w


def _inproj_kernel(x_ref, nw_ref, wf_ref, wr_ref, of_ref, or_ref):
    hn = _rms(x_ref[...], nw_ref[...]).astype(BF16)
    of_ref[...] = jnp.dot(hn, wf_ref[...], preferred_element_type=F32).astype(BF16)
    step = 640
    for c in range(0, REST_WIDTH, step):
        or_ref[:, c:c + step] = jnp.dot(hn, wr_ref[:, c:c + step], preferred_element_type=F32)


def _inproj(x, nw, wf, wr, tm=512):
    t = x.shape[0]
    return pl.pallas_call(
        _inproj_kernel,
        grid=(t // tm,),
        in_specs=[pl.BlockSpec((tm, D_MODEL), lambda i: (i, 0)),
                  pl.BlockSpec((1, D_MODEL), lambda i: (0, 0)),
                  pl.BlockSpec((D_MODEL, 3 * FOX_WIDTH), lambda i: (0, 0)),
                  pl.BlockSpec((D_MODEL, REST_WIDTH), lambda i: (0, 0))],
        out_specs=[pl.BlockSpec((tm, 3 * FOX_WIDTH), lambda i: (i, 0)),
                   pl.BlockSpec((tm, REST_WIDTH), lambda i: (i, 0))],
        out_shape=[jax.ShapeDtypeStruct((t, 3 * FOX_WIDTH), BF16),
                   jax.ShapeDtypeStruct((t, REST_WIDTH), F32)],
        compiler_params=_params("parallel"),
        name="inproj",
    )(x, nw, wf, wr)


def _fgate_kernel(sm_ref, bias_ref, sel_ref, tri_ref, c_ref):
    n_blk = sm_ref.shape[1] // LANES
    z = sm_ref[0] + bias_ref[...]
    lf = jnp.minimum(z, 0.0) - jnp.log1p(jnp.exp(-jnp.abs(z)))
    sel = sel_ref[...]
    tri = tri_ref[...]
    within = []
    for j in range(n_blk):
        blk = lf[j * LANES:(j + 1) * LANES, :]
        x = lax.dot_general(sel, blk, (((1,), (1,)), ((), ())), precision=HIGHEST, preferred_element_type=F32)
        within.append(jnp.dot(x, tri, precision=HIGHEST, preferred_element_type=F32))
    carry = jnp.zeros((sel.shape[0], 1), F32)
    for j in range(n_blk):
        cj = within[j] + carry
        for h in range(FOX_HEADS):
            c_ref[0, h, :, j * LANES:(j + 1) * LANES] = cj[h:h + 1, :]
        carry = cj[:, LANES - 1:LANES]


def _fgate(rest3, bias_row):
    b, s, _ = rest3.shape
    tri = (np.arange(LANES)[:, None] <= np.arange(LANES)[None, :]).astype(np.float32)
    sel = (np.arange(8)[:, None] == np.arange(LANES)[None, :]).astype(np.float32)
    sel[FOX_HEADS:] = 0.0
    return pl.pallas_call(
        _fgate_kernel,
        grid=(b,),
        in_specs=[pl.BlockSpec((1, s, LANES), lambda i: (i, 0, REST_SMALL // LANES)),
                  pl.BlockSpec((1, LANES), lambda i: (0, 0)),
                  pl.BlockSpec((8, LANES), lambda i: (0, 0)),
                  pl.BlockSpec((LANES, LANES), lambda i: (0, 0))],
        out_specs=pl.BlockSpec((1, FOX_HEADS, 1, s), lambda i: (i, 0, 0, 0)),
        out_shape=jax.ShapeDtypeStruct((b, FOX_HEADS, 1, s), F32),
        compiler_params=_params("parallel"),
        name="fgate",
    )(rest3, bias_row, jnp.asarray(sel), jnp.asarray(tri))


def _fox_kernel(q_ref, k_ref, v_ref, c_ref, o_ref, sa_ref, sb_ref, *, tq, tk):
    i = pl.program_id(2)
    lane = lax.broadcasted_iota(jnp.int32, (1, LANES), 1)
    first = lane < HEAD64
    q = q_ref[0]
    zero = jnp.zeros_like(q)
    qh = (jnp.where(first, q, zero), jnp.where(first, zero, q))
    nfull = (i * tq) // tk
    cbase = [c_ref[0, hh, nfull][:, 0:1] for hh in range(2)]
    qpos = i * tq + lax.broadcasted_iota(jnp.int32, (tq, 1), 0)
    den = (HEAD64, 0)
    lane_v = lax.broadcasted_iota(jnp.int32, (tk, LANES), 1)
    keep = (lane_v < HEAD64, lane_v >= HEAD64)
    ones_col = tuple(jnp.where(lane_v == d, 1.0, 0.0).astype(BF16) for d in den)

    def scores(j, s_ref):
        k0 = pl.multiple_of(j * tk, tk)
        k = k_ref[0, pl.ds(k0, tk), :]
        for hh in range(2):
            s = lax.dot_general(qh[hh], k, (((1,), (1,)), ((), ())), preferred_element_type=F32)
            s_ref[hh] = s + (cbase[hh] - c_ref[0, hh, j]) * LOG2E

    def update(j, s_ref, carry, masked):
        k0 = pl.multiple_of(j * tk, tk)
        v = v_ref[0, pl.ds(k0, tk), :]
        vh = tuple(jnp.where(keep[hh], v, ones_col[hh]) for hh in range(2))
        out = []
        for hh in range(2):
            m, acc = carry[hh]
            s = s_ref[hh]
            if masked:
                kpos = j * tk + lax.broadcasted_iota(jnp.int32, (1, tk), 1)
                s = jnp.where(kpos <= qpos, s, -jnp.inf)
            m_new = jnp.maximum(m, jnp.max(s, axis=-1, keepdims=True))
            alpha = jnp.exp2(m - m_new)
            p = jnp.exp2(s - m_new)
            acc = alpha * acc + jnp.dot(p.astype(BF16), vh[hh], preferred_element_type=F32)
            out.append((m_new, acc))
        return tuple(out)

    def pair(jj, carry):
        j = 2 * jj
        scores(j + 1, sb_ref)
        carry = update(j, sa_ref, carry, False)
        scores(j + 2, sa_ref)
        return update(j + 1, sb_ref, carry, False)

    def tail_even(carry):
        return update(nfull, sa_ref, carry, True)

    def tail_odd(carry):
        scores(nfull, sb_ref)
        carry = update(nfull - 1, sa_ref, carry, False)
        return update(nfull, sb_ref, carry, True)

    init = tuple((jnp.full((tq, 1), -jnp.inf, F32), jnp.zeros((tq, LANES), F32)) for _ in range(2))
    scores(0, sa_ref)
    carry = lax.fori_loop(0, nfull // 2, pair, init)
    carry = lax.cond(nfull % 2 == 1, tail_odd, tail_even, carry)
    acc0, acc1 = carry[0][1], carry[1][1]
    o0 = acc0 / acc0[:, den[0]:den[0] + 1]
    o1 = acc1 / acc1[:, den[1]:den[1] + 1]
    o_ref[0] = jnp.where(first, o0, o1).astype(BF16)


def _fox(qkv, c, tq=512, tk=512):
    b, s, _ = qkv.shape
    npair = FOX_HEADS // 2
    kern = functools.partial(_fox_kernel, tq=tq, tk=tk)
    return pl.pallas_call(
        kern,
        grid=(b, npair, s // tq),
        in_specs=[pl.BlockSpec((1, tq, LANES), lambda bi, p, i: (bi, i, p)),
                  pl.BlockSpec((1, s, LANES), lambda bi, p, i: (bi, 0, npair + p)),
                  pl.BlockSpec((1, s, LANES), lambda bi, p, i: (bi, 0, 2 * npair + p)),
                  pl.BlockSpec((1, 2, s // tk, 1, tk), lambda bi, p, i: (bi, p, 0, 0, 0))],
        out_specs=pl.BlockSpec((1, tq, LANES), lambda bi, p, i: (bi, i, p)),
        out_shape=jax.ShapeDtypeStruct((b, s, FOX_WIDTH), BF16),
        scratch_shapes=[pltpu.VMEM((2, tq, tk), F32), pltpu.VMEM((2, tq, tk), F32)],
        compiler_params=_params("parallel", "parallel", "arbitrary"),
        name="fox",
    )(qkv, qkv, qkv, c)


def _ret_kernel(q_ref, k_ref, v_ref, g_ref, cos_ref, sin_ref, dmat_ref, qdec_ref, kdec_ref, cd_ref, bm_ref,
                o_ref, state_ref, *, ts):
    @pl.when(pl.program_id(2) == 0)
    def _():
        state_ref[...] = jnp.zeros_like(state_ref)

    lane = lax.broadcasted_iota(jnp.int32, (1, LANES), 1)
    q_first = (lane % HEAD64) < (HEAD64 // 2)
    v_first = lane < HEAD64
    c_len = RET_CHUNK
    n_chunks = ts // c_len
    chunk_rows = [slice(c * c_len, (c + 1) * c_len) for c in range(n_chunks)]
    qr, kr, vs = [], [], []
    for rows in chunk_rows:
        cos = cos_ref[rows, :]
        sin = sin_ref[rows, :]
        q = q_ref[0, rows, :]
        k = k_ref[0, rows, :]
        qr.append(q * cos + pltpu.roll(q, LANES // 2, 1) * sin)
        kr.append(k * cos + pltpu.roll(k, LANES // 2, 1) * sin)
        vs.append(v_ref[0, rows, :])
    scores = [_mm_nt(jnp.concatenate([jnp.where(q_first, qr[c], 0.0), jnp.where(q_first, 0.0, qr[c])], axis=0),
                     kr[c] * (HEAD64 ** -0.5)) for c in range(n_chunks)]
    kvs = [_mm_tn(kr[c] * kdec_ref[0], vs[c]) for c in range(n_chunks)]
    intra = [_mm(scores[c][:c_len] * dmat_ref[0], jnp.where(v_first, vs[c], 0.0))
             + _mm(scores[c][c_len:] * dmat_ref[1], jnp.where(v_first, 0.0, vs[c])) for c in range(n_chunks)]
    states = []
    state = state_ref[...]
    for c in range(n_chunks):
        states.append(state)
        state = state * cd_ref[0] + kvs[c] * bm_ref[0]
    state_ref[...] = state
    for c, rows in enumerate(chunk_rows):
        o = intra[c] + _mm(qr[c] * qdec_ref[0], states[c])
        sq = o * o
        ms0 = jnp.sum(jnp.where(v_first, sq, 0.0), axis=-1, keepdims=True)
        ms1 = jnp.sum(jnp.where(v_first, 0.0, sq), axis=-1, keepdims=True)
        ms = jnp.where(v_first, ms0, ms1) * (1.0 / HEAD64)
        y = o * lax.rsqrt(ms + NORM_EPS) * _silu(g_ref[0, rows, :])
        o_ref[0, rows, :] = y.astype(BF16)


def _ret_tables(s):
    npair = RET_HEADS // 2
    half = HEAD64 // 2
    lane = np.arange(LANES)
    log_g = np.log1p(-np.exp2(-5.0 - np.arange(RET_HEADS, dtype=np.float32))).astype(np.float32)
    idx = np.arange(RET_CHUNK, dtype=np.float32)
    rel = idx[:, None] - idx[None, :]
    dmat = np.where(rel[None] >= 0, np.exp(np.maximum(rel, 0.0)[None] * log_g[:, None, None]), 0.0)
    qdec, kdec, cd, bm = [], [], [], []
    for p in range(npair):
        hq = 2 * p + ((lane % HEAD64) >= half)
        hv = 2 * p + (lane >= HEAD64)
        qdec.append(np.exp((idx[:, None] + 1.0) * log_g[hq][None, :]))
        kdec.append(np.exp((RET_CHUNK - 1 - idx)[:, None] * log_g[hq][None, :]) * HEAD64 ** -0.5)
        cd.append(np.broadcast_to(np.exp(RET_CHUNK * log_g[hq])[:, None], (LANES, LANES)))
        bm.append((hq[:, None] == hv[None, :]).astype(np.float32))
    tabs = [np.stack(a).astype(np.float32) for a in (qdec, kdec, cd, bm)]
    inv = 1.0 / (RET_ANGLE_BASE ** jnp.linspace(0.0, 1.0, half, dtype=F32))
    pos = jnp.arange(s, dtype=F32)
    ang = pos[:, None] * inv[None, :]
    cos = jnp.tile(jnp.cos(ang), (1, LANES // half))
    sin = jnp.tile(jnp.sin(ang), (1, LANES // half))
    sin = jnp.where(jnp.asarray(lane)[None, :] < LANES // 2, -sin, sin)
    return [jnp.asarray(dmat.astype(np.float32))] + [jnp.asarray(a) for a in tabs] + [cos, sin]


def _ret(rest3, tables, ts=1024):
    b, s, _ = rest3.shape
    dmat, qdec, kdec, cd, bm, cos, sin = tables
    npair = RET_HEADS // 2
    col = lambda off: (lambda bi, p, i: (bi, i, off // LANES + p))
    tab = lambda bi, p, i: (p, 0, 0)
    kern = functools.partial(_ret_kernel, ts=ts)
    return pl.pallas_call(
        kern,
        grid=(b, npair, s // ts),
        in_specs=[pl.BlockSpec((1, ts, LANES), col(REST_RQ)),
                  pl.BlockSpec((1, ts, LANES), col(REST_RK)),
                  pl.BlockSpec((1, ts, LANES), col(REST_RV)),
                  pl.BlockSpec((1, ts, LANES), col(REST_RG)),
                  pl.BlockSpec((ts, LANES), lambda bi, p, i: (i, 0)),
                  pl.BlockSpec((ts, LANES), lambda bi, p, i: (i, 0)),
                  pl.BlockSpec((2, RET_CHUNK, RET_CHUNK), tab),
                  pl.BlockSpec((1, RET_CHUNK, LANES), tab),
                  pl.BlockSpec((1, RET_CHUNK, LANES), tab),
                  pl.BlockSpec((1, LANES, LANES), tab),
                  pl.BlockSpec((1, LANES, LANES), tab)],
        out_specs=pl.BlockSpec((1, ts, LANES), lambda bi, p, i: (bi, i, p)),
        out_shape=jax.ShapeDtypeStruct((b, s, RET_WIDTH), BF16),
        scratch_shapes=[pltpu.VMEM((LANES, LANES), F32)],
        compiler_params=_params("parallel", "parallel", "arbitrary"),
        name="retention",
    )(rest3, rest3, rest3, rest3, cos, sin, dmat, qdec, kdec, cd, bm)


def _gdn_prep_kernel(q_ref, k_ref, v_ref, sm_ref, wq_ref, wk_ref, wv_ref, alog_ref, dtb_ref,
                     u0_ref, w_ref, qg_ref, kt_ref, at_ref, eg_ref):
    h = pl.program_id(1)
    c_len = GDN_CHUNK
    n_chunks = q_ref.shape[1] // c_len
    lane = lax.broadcasted_iota(jnp.int32, (1, LANES), 1)
    ri = lax.broadcasted_iota(jnp.int32, (c_len, c_len), 0)
    ci = lax.broadcasted_iota(jnp.int32, (c_len, c_len), 1)
    incl = ri >= ci
    strict = ri > ci
    eye = (ri == ci).astype(F32)
    neg_a = -jnp.exp(alog_ref[...])
    dtb = dtb_ref[...]

    grp = GDN_PREP_GROUP
    rows = grp * c_len

    def conv_silu(ref, w_ref_, n, r0):
        cur = ref[0, pl.ds(r0, rows), :]
        p0 = pl.multiple_of(jnp.maximum(r0 - 8, 0), 8)
        prev = ref[0, pl.ds(p0, 8), :]
        prev = jnp.where(jnp.broadcast_to(n > 0, prev.shape), prev, 0.0)
        xc = jnp.concatenate([prev, cur], axis=0)
        w = w_ref_[...]
        y = cur * w[CONV_WIDTH - 1:CONV_WIDTH, :]
        for j in range(CONV_WIDTH - 1):
            shifted = pltpu.roll(xc, CONV_WIDTH - 1 - j, 0)[8:, :]
            y = y + shifted * w[j:j + 1, :]
        return _silu(y)

    def group(n, carry):
        r0 = pl.multiple_of(n * rows, rows)
        cq = conv_silu(q_ref, wq_ref, n, r0)
        ck = conv_silu(k_ref, wk_ref, n, r0)
        cv = conv_silu(v_ref, wv_ref, n, r0)
        qn = cq * lax.rsqrt(jnp.sum(cq * cq, axis=-1, keepdims=True) + NORM_EPS) * (GDN_DIM ** -0.5)
        kn = ck * lax.rsqrt(jnp.sum(ck * ck, axis=-1, keepdims=True) + NORM_EPS)
        sm = sm_ref[0, pl.ds(r0, rows), :]
        z = sm + dtb
        g_all = neg_a * (jnp.maximum(z, 0.0) + jnp.log1p(jnp.exp(-jnp.abs(z))))
        beta_all = _sigmoid(sm)
        g_col = jnp.sum(jnp.where(lane == SMALL_GA + h, g_all, 0.0), axis=-1, keepdims=True)
        beta = jnp.sum(jnp.where(lane == SMALL_GB + h, beta_all, 0.0), axis=-1, keepdims=True)
        kb = kn * beta
        vb = cv * beta
        chunks = [slice(g * c_len, (g + 1) * c_len) for g in range(grp)]
        g_row = [jnp.sum(g_col[c] * eye, axis=0, keepdims=True) for c in chunks]
        gc = [jnp.sum(jnp.where(incl, g_row[i], 0.0), axis=-1, keepdims=True) for i in range(grp)]
        gc_row = [jnp.sum(jnp.where(ri <= ci, g_col[c], 0.0), axis=0, keepdims=True) for c in chunks]
        decay = [jnp.where(incl, jnp.exp(jnp.where(incl, gc[i] - gc_row[i], 0.0)), 0.0) for i in range(grp)]
        both = [_mm_nt(jnp.concatenate([kb[c], qn[c]], axis=0), kn[c]) for c in chunks]
        low = [jnp.where(strict, both[i][:c_len] * decay[i], 0.0) for i in range(grp)]
        attn = [jnp.where(incl, both[i][c_len:] * decay[i], 0.0) for i in range(grp)]
        inv = [eye - low[i] for i in range(grp)]
        pw = low
        for _ in range(int(math.log2(c_len)) - 1):
            pw = [_mm(pw[i], pw[i]) for i in range(grp)]
            inv = [inv[i] + _mm(inv[i], pw[i]) for i in range(grp)]
        eg = [jnp.exp(gc[i]) for i in range(grp)]
        sol = [_mm(inv[i], jnp.concatenate([vb[c], kb[c] * eg[i]], axis=1)) for i, c in enumerate(chunks)]
        for i, c in enumerate(chunks):
            dst = pl.ds(pl.multiple_of(r0 + i * c_len, c_len), c_len)
            g_last = gc[i][c_len - 1:c_len, :]
            u0_ref[0, 0, dst, :] = sol[i][:, :GDN_DIM]
            w_ref[0, 0, dst, :] = sol[i][:, GDN_DIM:].astype(BF16)
            qg_ref[0, 0, dst, :] = (qn[c] * eg[i]).astype(BF16)
            kt_ref[0, 0, dst, :] = (kn[c] * jnp.exp(g_last - gc[i])).astype(BF16)
            at_ref[0, 0, dst, :] = attn[i].astype(BF16)
            eg_ref[0, 0, n * grp + i] = jnp.broadcast_to(jnp.exp(g_last), (1, LANES))
        return carry

    lax.fori_loop(0, n_chunks // grp, group, 0)


def _gdn_prep(rest3, conv_w, alog_l, dtb_l):
    b, s, _ = rest3.shape
    nh = GDN_HEADS
    n_chunks = s // GDN_CHUNK
    col = lambda off: (lambda bi, h: (bi, 0, off // LANES + h))
    wcol = lambda g: (lambda bi, h: (0, g * nh + h))
    const = lambda bi, h: (0, 0)
    row =pl.BlockSpec((1, 1, s, GDN_DIM), lambda bi, h: (bi, h, 0, 0))
    return pl.pallas_call(
        _gdn_prep_kernel,
        grid=(b, nh),
        in_specs=[pl.BlockSpec((1, s, LANES), col(REST_GQ)),
                  pl.BlockSpec((1, s, LANES), col(REST_GK)),
                  pl.BlockSpec((1, s, LANES), col(REST_GV)),
                  pl.BlockSpec((1, s, LANES), lambda bi, h: (bi, 0, REST_SMALL // LANES)),
                  pl.BlockSpec((CONV_WIDTH, LANES), wcol(0)),
                  pl.BlockSpec((CONV_WIDTH, LANES), wcol(1)),
                  pl.BlockSpec((CONV_WIDTH, LANES), wcol(2)),
                  pl.BlockSpec((1, LANES), const),
                  pl.BlockSpec((1, LANES), const)],
        out_specs=[row, row, row, row,
                   pl.BlockSpec((1, 1, s, GDN_CHUNK), lambda bi, h: (bi, h, 0, 0)),
                   pl.BlockSpec((1, 1, n_chunks, 1, LANES), lambda bi, h: (bi, h, 0, 0, 0))],
        out_shape=[jax.ShapeDtypeStruct((b, nh, s, GDN_DIM), F32),
                   jax.ShapeDtypeStruct((b, nh, s, GDN_DIM), BF16),
                   jax.ShapeDtypeStruct((b, nh, s, GDN_DIM), BF16),
                   jax.ShapeDtypeStruct((b, nh, s, GDN_DIM), BF16),
                   jax.ShapeDtypeStruct((b, nh, s, GDN_CHUNK), BF16),
                   jax.ShapeDtypeStruct((b, nh, n_chunks, 1, LANES), F32)],
        compiler_params=_params("parallel", "parallel"),
        name="gdn_prep",
    )(rest3, rest3, rest3, rest3, conv_w, conv_w, conv_w, alog_l, dtb_l)


def _gdn_scan_kernel(u0_ref, w_ref, qg_ref, kt_ref, at_ref, eg_ref, z_ref, nw_ref, o_ref, state_ref, *, ts):
    si = pl.program_id(1)

    @pl.when(si == 0)
    def _():
        state_ref[...] = jnp.zeros_like(state_ref)

    c_len = GDN_CHUNK
    per_tile = ts // c_len
    nb = u0_ref.shape[0]
    nw = nw_ref[...]
    chains = [(bb, h) for bb in range(nb) for h in range(GDN_HEADS)]
    for c in range(per_tile):
        rows = slice(c * c_len, (c + 1) * c_len)
        st = [state_ref[bb, h] for bb, h in chains]
        r = [_mm(jnp.concatenate([w_ref[bb, h, rows, :], qg_ref[bb, h, rows, :]], axis=0), st[n])
             for n, (bb, h) in enumerate(chains)]
        u = [(u0_ref[bb, h, rows, :] - r[n][:c_len]).astype(BF16) for n, (bb, h) in enumerate(chains)]
        ku = [_mm_tn(kt_ref[bb, h, rows, :], u[n]) for n, (bb, h) in enumerate(chains)]
        au = [jnp.dot(at_ref[bb, h, rows, :], u[n], preferred_element_type=F32) for n, (bb, h) in enumerate(chains)]
        for n, (bb, h) in enumerate(chains):
            state_ref[bb, h] = st[n] * eg_ref[bb, h, si * per_tile + c] + ku[n]
        for n, (bb, h) in enumerate(chains):
            cols = slice(h * GDN_DIM, (h + 1) * GDN_DIM)
            y = _rms(r[n][c_len:] + au[n], nw) * _silu(z_ref[bb, rows, cols])
            o_ref[bb, rows, cols] = y.astype(BF16)


def _gdn_scan(u0, w, qg, kt, at, eg, rest3, norm_w, ts=512, nb=2):
    b, nh, s, _ = u0.shape
    n_chunks = s // GDN_CHUNK
    blk = lambda d: pl.BlockSpec((nb, nh, ts, d), lambda bi, i: (bi, 0, i, 0))
    kern = functools.partial(_gdn_scan_kernel, ts=ts)
    return pl.pallas_call(
        kern,
        grid=(b // nb, s // ts),
        in_specs=[blk(GDN_DIM), blk(GDN_DIM), blk(GDN_DIM), blk(GDN_DIM), blk(GDN_CHUNK),
                  pl.BlockSpec((nb, nh, n_chunks, 1, LANES), lambda bi, i: (bi, 0, 0, 0, 0)),
                  pl.BlockSpec((nb, ts, GDN_WIDTH), lambda bi, i: (bi, i, REST_GZ // GDN_WIDTH)),
                  pl.BlockSpec((1, GDN_DIM), lambda bi, i: (0, 0))],
        out_specs=pl.BlockSpec((nb, ts, GDN_WIDTH), lambda bi, i: (bi, i, 0)),
        out_shape=jax.ShapeDtypeStruct((b, s, GDN_WIDTH), BF16),
        scratch_shapes=[pltpu.VMEM((nb, nh, GDN_DIM, GDN_DIM), F32)],
        compiler_params=_params("parallel", "arbitrary"),
        name="gdn_scan",
    )(u0, w, qg, kt, at, eg, rest3, norm_w)


def _outproj_kernel(x_ref, of_ref, or_ref, og_ref, w_ref, o_ref):
    acc = x_ref[...]
    acc = acc + jnp.dot(of_ref[...], w_ref[0:FOX_WIDTH, :], preferred_element_type=F32)
    acc = acc + jnp.dot(or_ref[...], w_ref[FOX_WIDTH:FOX_WIDTH + RET_WIDTH, :], preferred_element_type=F32)
    acc = acc + jnp.dot(og_ref[...], w_ref[FOX_WIDTH + RET_WIDTH:, :], preferred_element_type=F32)
    o_ref[...] = acc


def _outproj(x, o_fox, o_ret, o_gdn, w_out, tm=512):
    t = x.shape[0]
    d_mix = w_out.shape[0]
    return pl.pallas_call(
        _outproj_kernel,
        grid=(t // tm,),
        in_specs=[pl.BlockSpec((tm, D_MODEL), lambda i: (i, 0)),
                  pl.BlockSpec((tm, FOX_WIDTH), lambda i: (i, 0)),
                  pl.BlockSpec((tm, RET_WIDTH), lambda i: (i, 0)),
                  pl.BlockSpec((tm, GDN_WIDTH), lambda i: (i, 0)),
                  pl.BlockSpec((d_mix, D_MODEL), lambda i: (0, 0))],
        out_specs=pl.BlockSpec((tm, D_MODEL), lambda i: (i, 0)),
        out_shape=jax.ShapeDtypeStruct((t, D_MODEL), F32),
        compiler_params=_params("parallel"),
        name="outproj",
    )(x, o_fox, o_ret, o_gdn, w_out)


def _router_kernel(x_ref, nw_ref, w_ref, b_ref, tri_ref, sel_ref, route_ref, idx_ref, cnt_ref, carry_ref):
    @pl.when(pl.program_id(0) == 0)
    def _():
        carry_ref[...] = jnp.zeros_like(carry_ref)

    tm = x_ref.shape[0]
    hn = _rms(x_ref[...], nw_ref[...])
    logits = jnp.dot(hn, w_ref[...], precision=HIGHEST, preferred_element_type=F32) + b_ref[...]
    lane = lax.broadcasted_iota(jnp.int32, logits.shape, 1).astype(F32)
    neg = -jnp.inf
    gl = jnp.where(lane < ROUTER_EXP, logits, neg)
    gmax = jnp.max(gl, axis=-1, keepdims=True)
    gidx = jnp.min(jnp.where(gl == gmax, lane, LANES), axis=-1, keepdims=True)
    grp_p = 1.0 / jnp.sum(jnp.exp(gl - gmax), axis=-1, keepdims=True)
    lo = ROUTER_EXP + gidx * EXPERTS_PER_GROUP
    el = jnp.where((lane >= lo) & (lane < lo + EXPERTS_PER_GROUP), logits, neg)
    m1 = jnp.max(el, axis=-1, keepdims=True)
    i1 = jnp.min(jnp.where(el == m1, lane, LANES), axis=-1, keepdims=True)
    el2 = jnp.where(lane == i1, neg, el)
    m2 = jnp.max(el2, axis=-1, keepdims=True)
    i2 = jnp.min(jnp.where(el2 == m2, lane, LANES), axis=-1, keepdims=True)
    e2 = jnp.exp(m2 - m1)
    w1 = grp_p / (1.0 + e2)
    ea = jnp.minimum(i1, i2) - ROUTER_EXP
    eb = jnp.maximum(i1, i2) - ROUTER_EXP
    first_low = i1 < i2
    ga = jnp.where(first_low, w1, w1 * e2)
    gb = jnp.where(first_low, w1 * e2, w1)
    la = ea - gidx * EXPERTS_PER_GROUP
    lb = eb - gidx * EXPERTS_PER_GROUP
    cls = gidx * N_PAIRS + la * (2 * EXPERTS_PER_GROUP - 1 - la) * 0.5 + (lb - la - 1.0)
    oh = (lane == cls).astype(F32)
    cum = jnp.dot(tri_ref[...], oh.astype(BF16), preferred_element_type=F32) + carry_ref[...]
    rank = jnp.sum(oh * (cum - oh), axis=-1, keepdims=True)
    carry_ref[...] = cum[tm - 1:tm, :]
    cnt_ref[...] = cum[tm - 1:tm, :]
    out = jnp.zeros_like(logits)
    for col, val in ((ROUTE_CLASS, cls), (ROUTE_RANK, rank), (ROUTE_GATE, ga), (ROUTE_GATE + 1, gb)):
        out = jnp.where(lane == col, val, out)
    route_ref[...] = out
    idx = lax.dot_general(sel_ref[...], out, (((1,), (1,)), ((), ())), precision=HIGHEST,
                          preferred_element_type=F32)
    idx_ref[...] = idx.astype(jnp.int32)


def _router(x, nw, w_pack, b_pack, tm=512):
    t = x.shape[0]
    tri = jnp.asarray((np.arange(tm)[:, None] >= np.arange(tm)[None, :]).astype(np.float32), dtype=BF16)
    sel = np.zeros((8, LANES), np.float32)
    for row, lane in enumerate((ROUTE_CLASS, ROUTE_RANK)):
        sel[row, lane] = 1.0
    return pl.pallas_call(
        _router_kernel,
        grid=(t // tm,),
        in_specs=[pl.BlockSpec((tm, D_MODEL), lambda i: (i, 0)),
                  pl.BlockSpec((1, D_MODEL), lambda i: (0, 0)),
                  pl.BlockSpec((D_MODEL, LANES), lambda i: (0, 0)),
                  pl.BlockSpec((1, LANES), lambda i: (0, 0)),
                  pl.BlockSpec((tm, tm), lambda i: (0, 0)),
                  pl.BlockSpec((8, LANES), lambda i: (0, 0))],
        out_specs=[pl.BlockSpec((tm, LANES), lambda i: (i, 0)),
                   pl.BlockSpec((8, tm), lambda i: (0, i)),
                   pl.BlockSpec((1, LANES), lambda i: (0, 0))],
        out_shape=[jax.ShapeDtypeStruct((t, LANES), F32),
                   jax.ShapeDtypeStruct((8, t), jnp.int32),
                   jax.ShapeDtypeStruct((1, LANES), F32)],
        scratch_shapes=[pltpu.VMEM((1, LANES), F32)],
        compiler_params=_params("arbitrary"),
        name="router",
    )(x, nw, w_pack, b_pack, tri, jnp.asarray(sel))


def _round_up_tile(v):
    shift = MOE_TILE.bit_length() - 1
    return lax.shift_left(lax.shift_right_logical(v + (MOE_TILE - 1), shift), shift)


def _dispatch_kernel(off_ref, cnt_ref, tot_ref, pos_ref, x_ref, route_ref, xs_ref, ring_ref, zero_ref, sem_ref, *,
                     tm, n_tiles):
    i = pl.program_id(0)
    n = pl.num_programs(0)
    slot = i % 2

    def row_copy(s, r, p):
        return pltpu.make_async_copy(ring_ref.at[s, pl.ds(r, 1)], xs_ref.at[pl.ds(p, 1)], sem_ref.at[s])

    def pad_rows(p, size):
        return pltpu.make_async_copy(zero_ref.at[pl.ds(0, size)], xs_ref.at[pl.ds(p, size)], sem_ref.at[2])

    def pad_tile(j):
        return pltpu.make_async_copy(zero_ref, xs_ref.at[pl.ds(j * MOE_TILE, MOE_TILE)], sem_ref.at[3])

    def pad_class(c, start):
        lo = cnt_ref[c]
        lo8 = lax.shift_left(lax.shift_right_logical(lo + 7, 3), 3)
        rem = _round_up_tile(lo) - lo8

        def single(r, carry):
            cp = pad_rows(off_ref[c] + r, 1)
            cp.start() if start else cp.wait()
            return carry

        lax.fori_loop(lo, lo8, single, 0)
        for bit in range(3, MOE_TILE.bit_length() - 1):
            size = 1 << bit

            @pl.when(lax.bitwise_and(rem, size) != 0)
            def _():
                above = lax.shift_left(lax.shift_right_logical(rem, bit + 1), bit + 1)
                cp = pad_rows(pl.multiple_of(off_ref[c] + lo8 + above, 8), size)
                cp.start() if start else cp.wait()

    @pl.when(i == 0)
    def _():
        zero_ref[...] = jnp.zeros_like(zero_ref)

        def start_class(c, carry):
            pad_class(c, True)
            return carry

        def wait_class(c, carry):
            pad_class(c, False)
            return carry

        def start_tile(j, c):
            pad_tile(j).start()
            return c

        def wait_tile(j, c):
            pad_tile(0).wait()
            return c

        lax.fori_loop(0, N_CLASSES, start_class, 0)
        lax.fori_loop(tot_ref[0], n_tiles, start_tile, 0)
        lax.fori_loop(0, N_CLASSES, wait_class, 0)
        lax.fori_loop(tot_ref[0], n_tiles, wait_tile, 0)

    ring_ref[slot, :, :D_MODEL] = x_ref[...]
    ring_ref[slot, :, D_MODEL:] = route_ref[...]

    def issue(g, c):
        for u in range(2):
            r = 2 * g + u
            row_copy(slot, r, pos_ref[0, r]).start(priority=u)
        return c

    lax.fori_loop(0, tm // 2, issue, 0, unroll=4)

    def drain(s):
        def wait(r, c):
            row_copy(s, 0, 0).wait()
            return c

        lax.fori_loop(0, tm, wait, 0, unroll=8)

    @pl.when(i > 0)
    def _():
        drain(1 - slot)

    @pl.when(i == n - 1)
    def _():
        drain(slot)


def _dispatch(x, route, off, cnt, total, pos, tm, n_tiles):
    t = x.shape[0]
    kern = functools.partial(_dispatch_kernel, tm=tm, n_tiles=n_tiles)
    return pl.pallas_call(
        kern,
        grid_spec=pltpu.PrefetchScalarGridSpec(
            num_scalar_prefetch=3,
            grid=(t // tm,),
            in_specs=[pl.BlockSpec((1, tm), lambda i, *_: (0, i), memory_space=pltpu.SMEM),
                      pl.BlockSpec((tm, D_MODEL), lambda i, *_: (i, 0)),
                      pl.BlockSpec((tm, LANES), lambda i, *_: (i, 0))],
            out_specs=pl.BlockSpec(memory_space=pl.ANY),
            scratch_shapes=[pltpu.VMEM((2, tm, ROW_WIDTH), F32),
                            pltpu.VMEM((MOE_TILE, ROW_WIDTH), F32),
                            pltpu.SemaphoreType.DMA((4,))]),
        out_shape=jax.ShapeDtypeStruct((n_tiles * MOE_TILE, ROW_WIDTH), F32),
        compiler_params=_params("arbitrary"),
        name="dispatch",
    )(off, cnt, total, pos, x, route)


def _ffn_kernel(ta_ref, tb_ref, fa_ref, fb_ref, tot_ref, xs_ref, nw_ref, w1a_ref, w3a_ref, w2a_ref,
                w1b_ref, w3b_ref, w2b_ref, ys_ref, w1a, w3a, w2a, w1b, w3b, w2b):
    j = pl.program_id(0)

    @pl.when(fa_ref[j] == 1)
    def _():
        w1a[...] = w1a_ref[0].astype(BF16)
        w3a[...] = w3a_ref[0].astype(BF16)
        w2a[...] = w2a_ref[0].astype(BF16)

    @pl.when(fb_ref[j] == 1)
    def _():
        w1b[...] = w1b_ref[0].astype(BF16)
        w3b[...] = w3b_ref[0].astype(BF16)
        w2b[...] = w2b_ref[0].astype(BF16)

    @pl.when(j < tot_ref[0])
    def _():
        x = xs_ref[:, :D_MODEL]
        ga = xs_ref[:, D_MODEL + ROUTE_GATE:D_MODEL + ROUTE_GATE + 1]
        gb = xs_ref[:, D_MODEL + ROUTE_GATE + 1:D_MODEL + ROUTE_GATE + 2]
        hn = _rms(x, nw_ref[...]).astype(BF16)

        def expert(w1, w3, w2):
            a = jnp.dot(hn, w1[...], preferred_element_type=F32)
            u = jnp.dot(hn, w3[...], preferred_element_type=F32)
            return jnp.dot((_silu(a) * u).astype(BF16), w2[...], preferred_element_type=F32)

        ys_ref[...] = x + ga * expert(w1a, w3a, w2a) + gb * expert(w1b, w3b, w2b)

    @pl.when(j >= tot_ref[0])
    def _():
        ys_ref[...] = jnp.zeros_like(ys_ref)


def _ffn(xs, nw, w1, w3, w2, tile_a, tile_b, first_a, first_b, total):
    n_tiles = tile_a.shape[0]
    tok = lambda j, *_: (j, 0)
    wa = lambda j, ta, tb, fa, fb, tot: (ta[j], 0, 0)
    wb = lambda j, ta, tb, fa, fb, tot: (tb[j], 0, 0)
    up = lambda imap: pl.BlockSpec((1, D_MODEL, EXPERT_FF), imap)
    down = lambda imap: pl.BlockSpec((1, EXPERT_FF, D_MODEL), imap)
    return pl.pallas_call(
        _ffn_kernel,
        grid_spec=pltpu.PrefetchScalarGridSpec(
            num_scalar_prefetch=5,
            grid=(n_tiles,),
            in_specs=[pl.BlockSpec((MOE_TILE, ROW_WIDTH), tok),
                      pl.BlockSpec((1, D_MODEL), lambda j, *_: (0, 0)),
                      up(wa), up(wa), down(wa), up(wb), up(wb), down(wb)],
            out_specs=pl.BlockSpec((MOE_TILE, D_MODEL), tok),
            scratch_shapes=[pltpu.VMEM((D_MODEL, EXPERT_FF), BF16),
                            pltpu.VMEM((D_MODEL, EXPERT_FF), BF16),
                            pltpu.VMEM((EXPERT_FF, D_MODEL), BF16),
                            pltpu.VMEM((D_MODEL, EXPERT_FF), BF16),
                            pltpu.VMEM((D_MODEL, EXPERT_FF), BF16),
                            pltpu.VMEM((EXPERT_FF, D_MODEL), BF16)]),
        out_shape=jax.ShapeDtypeStruct((n_tiles * MOE_TILE, D_MODEL), F32),
        compiler_params=_params("arbitrary"),
        name="ffn",
    )(tile_a, tile_b, first_a, first_b, total, xs, nw, w1, w3, w2, w1, w3, w2)


def _gather_kernel(pos_ref, posn_ref, fw_ref, ys_ref, o_ref, ybuf, sem_ref, *, tm, final_norm):
    i = pl.program_id(0)
    n = pl.num_programs(0)
    slot = i % 2

    def row_copy(s, r, p):
        return pltpu.make_async_copy(ys_ref.at[pl.ds(p, 1)], ybuf.at[s, pl.ds(r, 1)], sem_ref.at[s])

    def issue(p_ref, s):
        def body(g, c):
            for u in range(2):
                r = 2 * g + u
                row_copy(s, r, p_ref[0, r]).start(priority=u)
            return c

        lax.fori_loop(0, tm // 2, body, 0, unroll=4)

    @pl.when(i == 0)
    def _():
        issue(pos_ref, 0)

    @pl.when(i + 1 < n)
    def _():
        issue(posn_ref, 1 - slot)

    def wait(r, c):
        row_copy(slot, 0, 0).wait()
        return c

    lax.fori_loop(0, tm, wait, 0, unroll=8)
    out = ybuf[slot]
    if final_norm:
        out = _rms(out, fw_ref[...])
    o_ref[...] = out


def _gather(ys, pos, fw, final_norm, tm):
    t = pos.shape[1]
    n = t // tm
    kern = functools.partial(_gather_kernel, tm=tm, final_norm=final_norm)
    smem = lambda imap: pl.BlockSpec((1, tm), imap, memory_space=pltpu.SMEM)
    return pl.pallas_call(
        kern,
        grid=(n,),
        in_specs=[smem(lambda i: (0, i)),
                  smem(lambda i: (0, jnp.minimum(i + 1, n - 1))),
                  pl.BlockSpec((1, D_MODEL), lambda i: (0, 0)),
                  pl.BlockSpec(memory_space=pl.ANY)],
        out_specs=pl.BlockSpec((tm, D_MODEL), lambda i: (i, 0)),
        out_shape=jax.ShapeDtypeStruct((t, D_MODEL), F32),
        scratch_shapes=[pltpu.VMEM((2, tm, D_MODEL), F32),
                        pltpu.SemaphoreType.DMA((2,))],
        compiler_params=_params("arbitrary"),
        name="gather",
    )(pos, pos, fw, ys)


def _class_experts():
    lo, hi = [], []
    for g in range(N_GROUPS):
        for la in range(EXPERTS_PER_GROUP):
            for lb in range(la + 1, EXPERTS_PER_GROUP):
                lo.append(g * EXPERTS_PER_GROUP + la)
                hi.append(g * EXPERTS_PER_GROUP + lb)
    return np.asarray(lo, np.int32), np.asarray(hi, np.int32)


def _lookup(table, index):
    k = table.shape[0]
    return jnp.sum(jnp.where(index[..., None] == jnp.arange(k, dtype=jnp.int32), table, 0), axis=-1)


def _moe(x, nw, w_pack, b_pack, w1, w3, w2, fw, final_norm, tm=512):
    t = x.shape[0]
    route, idx, cnt_row = _router(x, nw, w_pack, b_pack)
    cnt = cnt_row[0, :N_CLASSES].astype(jnp.int32)
    n_tiles = t // MOE_TILE + N_CLASSES
    nblk = (cnt + MOE_TILE - 1) // MOE_TILE
    cend = jnp.cumsum(nblk)
    total = cend[-1:]
    off = (cend - nblk) * MOE_TILE
    pos = (_lookup(off, idx[ROUTE_CLASS]) + idx[ROUTE_RANK]).reshape(1, t)
    j = jnp.arange(n_tiles, dtype=jnp.int32)
    tile_c = jnp.minimum(jnp.sum((j[:, None] >= cend[None, :]).astype(jnp.int32), axis=1), N_CLASSES - 1)
    lo, hi = _class_experts()
    tile_a = _lookup(jnp.asarray(lo), tile_c)
    tile_b = _lookup(jnp.asarray(hi), tile_c)
    valid = j < total
    changed = lambda e: (valid & ((j == 0) | (e != jnp.roll(e, 1)))).astype(jnp.int32)
    xs = _dispatch(x, route, off, cnt, total, pos, tm, n_tiles)
    ys = _ffn(xs, nw, w1, w3, w2, tile_a, tile_b, changed(tile_a), changed(tile_b), total)
    return _gather(ys, pos, fw, final_norm, tm)


def _pack_in_weights(w_in_l):
    off = np.concatenate([[0], np.cumsum(IN_SPLITS)]).tolist()
    fq, fk, fv, ff, rq, rk, rv, rg, gq, gk, gv, gz, ga, gb = [(off[i], IN_SPLITS[i]) for i in range(len(IN_SPLITS))]
    half = HEAD64 // 2

    def permuted(seg):
        return [(seg[0] + (2 * p + hh) * HEAD64 + lo * half, half)
                for p in range(RET_HEADS // 2) for lo in range(2) for hh in range(2)]

    col_scale = np.ones((1, w_in_l.shape[1]), np.float32)
    col_scale[:, fq[0]:fq[0] + fq[1]] = LOG2E * HEAD64 ** -0.5
    w_bf = (w_in_l * col_scale).astype(BF16)
    cols = lambda segs: [w_bf[:, a:a + n] for a, n in segs]
    wf = jnp.concatenate(cols([fq, fk, fv]), axis=1)
    pad = jnp.zeros((D_MODEL, LANES - (FOX_HEADS + 2 * GDN_HEADS)), BF16)
    wr = jnp.concatenate(cols(permuted(rq) + permuted(rk) + [rv, rg, gq, gk, gv, gz, ff, ga, gb]) + [pad], axis=1)
    return wf, wr


def _lane_row(vals, offset):
    return jnp.zeros((1, LANES), F32).at[0, offset:offset + vals.shape[0]].set(vals.astype(F32))


def kernel(x, norm1_w, w_in, fox_forget_bias, gdn_conv_w, gdn_a_log, gdn_dt_bias, gdn_norm_w, w_out, norm2_w,
           router_group_w, router_group_b, router_expert_w, router_expert_b, expert_w1, expert_w3, expert_w2,
           final_norm_w):
    b, s, d = x.shape
    t = b * s
    depth = w_in.shape[0]
    fox_tk = 512
    xt = x.reshape(t, d)
    ret_tables = _ret_tables(s)
    for l in range(depth):
        wf, wr = _pack_in_weights(w_in[l])
        qkv, rest = _inproj(xt, norm1_w[l].reshape(1, d), wf, wr)
        rest3 = rest.reshape(b, s, REST_WIDTH)
        c = _fgate(rest3, _lane_row(fox_forget_bias[l], SMALL_FF)).reshape(b, FOX_HEADS, s // fox_tk, 1, fox_tk)
        o_fox = _fox(qkv.reshape(b, s, 3 * FOX_WIDTH), c, tk=fox_tk)
        o_ret = _ret(rest3, ret_tables)
        prep = _gdn_prep(rest3, gdn_conv_w[l].astype(F32), _lane_row(gdn_a_log[l], SMALL_GA),
                         _lane_row(gdn_dt_bias[l], SMALL_GA))
        o_gdn = _gdn_scan(*prep, rest3, gdn_norm_w[l].reshape(1, GDN_DIM).astype(F32))
        xt = _outproj(xt, o_fox.reshape(t, FOX_WIDTH), o_ret.reshape(t, RET_WIDTH),
                      o_gdn.reshape(t, GDN_WIDTH), w_out[l].astype(BF16))
        w_pack = jnp.concatenate([router_group_w[l], router_expert_w[l],
                                  jnp.zeros((d, LANES - N_GROUPS - N_EXPERTS), F32)], axis=1)
        b_pack = jnp.concatenate([router_group_b[l].reshape(-1), router_expert_b[l].reshape(-1),
                                  jnp.zeros((LANES - N_GROUPS - N_EXPERTS,), F32)]).reshape(1, LANES)
        xt = _moe(xt, norm2_w[l].reshape(1, d), w_pack, b_pack,
                  expert_w1[l].reshape(N_EXPERTS, d, EXPERT_FF),
                  expert_w3[l].reshape(N_EXPERTS, d, EXPERT_FF),
                  expert_w2[l].reshape(N_EXPERTS, EXPERT_FF, d),
                  final_norm_w.reshape(1, d), final_norm=(l == depth - 1))
    return xt.reshape(b, s, d)
```

```python
import functools
import math

import jax
import jax.numpy as jnp
import numpy as np
from jax import lax
from jax.experimental import pallas as pl
from jax.experimental.pallas import tpu as pltpu

F32 = jnp.float32
BF16 = jnp.bfloat16
HIGHEST = lax.Precision.HIGHEST

D_MODEL = 1024
FOX_HEADS = 4
RET_HEADS = 4
GDN_HEADS = 4
HEAD64 = 64
GDN_DIM = 128
FOX_WIDTH = FOX_HEADS * HEAD64
RET_WIDTH = RET_HEADS * HEAD64
GDN_WIDTH = GDN_HEADS * GDN_DIM
RET_CHUNK = 128
GDN_CHUNK = 64
GDN_PREP_GROUP = 16
CONV_WIDTH = 4
RET_ANGLE_BASE = 10000.0
N_GROUPS = 4
EXPERTS_PER_GROUP = 8
N_EXPERTS = N_GROUPS * EXPERTS_PER_GROUP
EXPERT_FF = 256
NORM_EPS = 1e-6
LOG2E = math.log2(math.e)
LANES = 128
IN_SPLITS = (FOX_WIDTH, FOX_WIDTH, FOX_WIDTH, FOX_HEADS,
             RET_WIDTH, RET_WIDTH, RET_WIDTH, RET_WIDTH,
             GDN_WIDTH, GDN_WIDTH, GDN_WIDTH, GDN_WIDTH, GDN_HEADS, GDN_HEADS)

REST_RQ, REST_RK, REST_RV, REST_RG = 0, 256, 512, 768
REST_GQ, REST_GK, REST_GV, REST_GZ = 1024, 1536, 2048, 2560
REST_SMALL = 3072
REST_WIDTH = 3200
SMALL_FF, SMALL_GA, SMALL_GB = 0, 4, 8
ROUTER_GRP, ROUTER_EXP = 0, 4
ROUTE_CLASS, ROUTE_RANK, ROUTE_GATE = 0, 1, 2
N_PAIRS = EXPERTS_PER_GROUP * (EXPERTS_PER_GROUP - 1) // 2
N_CLASSES = N_GROUPS * N_PAIRS
MOE_TILE = 256
ROW_WIDTH = D_MODEL + LANES

VMEM_LIMIT = 56 * 1024 * 1024


def _params(*sem):
    return pltpu.CompilerParams(dimension_semantics=sem, vmem_limit_bytes=VMEM_LIMIT)


def _mm(a, b):
    return jnp.dot(a.astype(BF16), b.astype(BF16), preferred_element_type=F32)


def _mm_nt(a, b):
    return lax.dot_general(a.astype(BF16), b.astype(BF16), (((1,), (1,)), ((), ())),
                           preferred_element_type=F32)


def _mm_tn(a, b):
    return lax.dot_general(a.astype(BF16), b.astype(BF16), (((0,), (0,)), ((), ())),
                           preferred_element_type=F32)


def _sigmoid(x):
    return 0.5 + 0.5 * jnp.tanh(0.5 * x)


def _silu(x):
    return x * _sigmoid(x)


def _rms(x, w):
    return x * lax.rsqrt(jnp.mean(x * x, axis=-1, keepdims=True) + NORM_EPS) * w


def _inproj_kernel(x_ref, nw_ref, wf_ref, wr_ref, of_ref, or_ref):
    hn = _rms(x_ref[...], nw_ref[...]).astype(BF16)
    of_ref[...] = jnp.dot(hn, wf_ref[...], preferred_element_type=F32).astype(BF16)
    step = 640
    for c in range(0, REST_WIDTH, step):
        or_ref[:, c:c + step] = jnp.dot(hn, wr_ref[:, c:c + step], preferred_element_type=F32)


def _inproj(x, nw, wf, wr, tm=512):
    t = x.shape[0]
    return pl.pallas_call(
        _inproj_kernel,
        grid=(t // tm,),
        in_specs=[pl.BlockSpec((tm, D_MODEL), lambda i: (i, 0)),
                  pl.BlockSpec((1, D_MODEL), lambda i: (0, 0)),
                  pl.BlockSpec((D_MODEL, 3 * FOX_WIDTH), lambda i: (0, 0)),
                  pl.BlockSpec((D_MODEL, REST_WIDTH), lambda i: (0, 0))],
        out_specs=[pl.BlockSpec((tm, 3 * FOX_WIDTH), lambda i: (i, 0)),
                   pl.BlockSpec((tm, REST_WIDTH), lambda i: (i, 0))],
        out_shape=[jax.ShapeDtypeStruct((t, 3 * FOX_WIDTH), BF16),
                   jax.ShapeDtypeStruct((t, REST_WIDTH), F32)],
        compiler_params=_params("parallel"),
        name="inproj",
    )(x, nw, wf, wr)


def _fgate_kernel(sm_ref, bias_ref, sel_ref, tri_ref, c_ref):
    n_blk = sm_ref.shape[1] // LANES
    z = sm_ref[0] + bias_ref[...]
    lf = jnp.minimum(z, 0.0) - jnp.log1p(jnp.exp(-jnp.abs(z)))
    sel = sel_ref[...]
    tri = tri_ref[...]
    within = []
    for j in range(n_blk):
        blk = lf[j * LANES:(j + 1) * LANES, :]
        x = lax.dot_general(sel, blk, (((1,), (1,)), ((), ())), precision=HIGHEST, preferred_element_type=F32)
        within.append(jnp.dot(x, tri, precision=HIGHEST, preferred_element_type=F32))
    carry = jnp.zeros((sel.shape[0], 1), F32)
    for j in range(n_blk):
        cj = within[j] + carry
        for h in range(FOX_HEADS):
            c_ref[0, h, :, j * LANES:(j + 1) * LANES] = cj[h:h + 1, :]
        carry = cj[:, LANES - 1:LANES]


def _fgate(rest3, bias_row):
    b, s, _ = rest3.shape
    tri = (np.arange(LANES)[:, None] <= np.arange(LANES)[None, :]).astype(np.float32)
    sel = (np.arange(8)[:, None] == np.arange(LANES)[None, :]).astype(np.float32)
    sel[FOX_HEADS:] = 0.0
    return pl.pallas_call(
        _fgate_kernel,
        grid=(b,),
        in_specs=[pl.BlockSpec((1, s, LANES), lambda i: (i, 0, REST_SMALL // LANES)),
                  pl.BlockSpec((1, LANES), lambda i: (0, 0)),
                  pl.BlockSpec((8, LANES), lambda i: (0, 0)),
                  pl.BlockSpec((LANES, LANES), lambda i: (0, 0))],
        out_specs=pl.BlockSpec((1, FOX_HEADS, 1, s), lambda i: (i, 0, 0, 0)),
        out_shape=jax.ShapeDtypeStruct((b, FOX_HEADS, 1, s), F32),
        compiler_params=_params("parallel"),
        name="fgate",
    )(rest3, bias_row, jnp.asarray(sel), jnp.asarray(tri))


def _fox_kernel(q_ref, k_ref, v_ref, c_ref, o_ref, sa_ref, sb_ref, *, tq, tk):
    i = pl.program_id(2)
    lane = lax.broadcasted_iota(jnp.int32, (1, LANES), 1)
    first = lane < HEAD64
    q = q_ref[0]
    zero = jnp.zeros_like(q)
    qh = (jnp.where(first, q, zero), jnp.where(first, zero, q))
    nfull = (i * tq) // tk
    cbase = [c_ref[0, hh, nfull][:, 0:1] for hh in range(2)]
    qpos = i * tq + lax.broadcasted_iota(jnp.int32, (tq, 1), 0)
    den = (HEAD64, 0)
    lane_v = lax.broadcasted_iota(jnp.int32, (tk, LANES), 1)
    keep = (lane_v < HEAD64, lane_v >= HEAD64)
    ones_col = tuple(jnp.where(lane_v == d, 1.0, 0.0).astype(BF16) for d in den)

    def scores(j, s_ref):
        k0 = pl.multiple_of(j * tk, tk)
        k = k_ref[0, pl.ds(k0, tk), :]
        for hh in range(2):
            s = lax.dot_general(qh[hh], k, (((1,), (1,)), ((), ())), preferred_element_type=F32)
            s_ref[hh] = s + (cbase[hh] - c_ref[0, hh, j]) * LOG2E

    def update(j, s_ref, carry, masked):
        k0 = pl.multiple_of(j * tk, tk)
        v = v_ref[0, pl.ds(k0, tk), :]
        vh = tuple(jnp.where(keep[hh], v, ones_col[hh]) for hh in range(2))
        out = []
        for hh in range(2):
            m, acc = carry[hh]
            s = s_ref[hh]
            if masked:
                kpos = j * tk + lax.broadcasted_iota(jnp.int32, (1, tk), 1)
                s = jnp.where(kpos <= qpos, s, -jnp.inf)
            m_new = jnp.maximum(m, jnp.max(s, axis=-1, keepdims=True))
            alpha = jnp.exp2(m - m_new)
            p = jnp.exp2(s - m_new)
            acc = alpha * acc + jnp.dot(p.astype(BF16), vh[hh], preferred_element_type=F32)
            out.append((m_new, acc))
        return tuple(out)

    def pair(jj, carry):
        j = 2 * jj
        scores(j + 1, sb_ref)
        carry = update(j, sa_ref, carry, False)
        scores(j + 2, sa_ref)
        return update(j + 1, sb_ref, carry, False)

    def tail_even(carry):
        return update(nfull, sa_ref, carry, True)

    def tail_odd(carry):
        scores(nfull, sb_ref)
        carry = update(nfull - 1, sa_ref, carry, False)
        return update(nfull, sb_ref, carry, True)

    init = tuple((jnp.full((tq, 1), -jnp.inf, F32), jnp.zeros((tq, LANES), F32)) for _ in range(2))
    scores(0, sa_ref)
    carry = lax.fori_loop(0, nfull // 2, pair, init)
    carry = lax.cond(nfull % 2 == 1, tail_odd, tail_even, carry)
    acc0, acc1 = carry[0][1], carry[1][1]
    o0 = acc0 / acc0[:, den[0]:den[0] + 1]
    o1 = acc1 / acc1[:, den[1]:den[1] + 1]
    o_ref[0] = jnp.where(first, o0, o1).astype(BF16)


def _fox(qkv, c, tq=512, tk=512):
    b, s, _ = qkv.shape
    npair = FOX_HEADS // 2
    kern = functools.partial(_fox_kernel, tq=tq, tk=tk)
    return pl.pallas_call(
        kern,
        grid=(b, npair, s // tq),
        in_specs=[pl.BlockSpec((1, tq, LANES), lambda bi, p, i: (bi, i, p)),
                  pl.BlockSpec((1, s, LANES), lambda bi, p, i: (bi, 0, npair + p)),
                  pl.BlockSpec((1, s, LANES), lambda bi, p, i: (bi, 0, 2 * npair + p)),
                  pl.BlockSpec((1, 2, s // tk, 1, tk), lambda bi, p, i: (bi, p, 0, 0, 0))],
        out_specs=pl.BlockSpec((1, tq, LANES), lambda bi, p, i: (bi, i, p)),
        out_shape=jax.ShapeDtypeStruct((b, s, FOX_WIDTH), BF16),
        scratch_shapes=[pltpu.VMEM((2, tq, tk), F32), pltpu.VMEM((2, tq, tk), F32)],
        compiler_params=_params("parallel", "parallel", "arbitrary"),
        name="fox",
    )(qkv, qkv, qkv, c)


def _ret_kernel(q_ref, k_ref, v_ref, g_ref, cos_ref, sin_ref, dmat_ref, qdec_ref, kdec_ref, cd_ref, bm_ref,
                o_ref, state_ref, *, ts):
    @pl.when(pl.program_id(2) == 0)
    def _():
        state_ref[...] = jnp.zeros_like(state_ref)

    lane = lax.broadcasted_iota(jnp.int32, (1, LANES), 1)
    q_first = (lane % HEAD64) < (HEAD64 // 2)
    v_first = lane < HEAD64
    c_len = RET_CHUNK
    n_chunks = ts // c_len
    chunk_rows = [slice(c * c_len, (c + 1) * c_len) for c in range(n_chunks)]
    qr, kr, vs = [], [], []
    for rows in chunk_rows:
        cos = cos_ref[rows, :]
        sin = sin_ref[rows, :]
        q = q_ref[0, rows, :]
        k = k_ref[0, rows, :]
        qr.append(q * cos + pltpu.roll(q, LANES // 2, 1) * sin)
        kr.append(k * cos + pltpu.roll(k, LANES // 2, 1) * sin)
        vs.append(v_ref[0, rows, :])
    scores = [_mm_nt(jnp.concatenate([jnp.where(q_first, qr[c], 0.0), jnp.where(q_first, 0.0, qr[c])], axis=0),
                     kr[c] * (HEAD64 ** -0.5)) for c in range(n_chunks)]
    kvs = [_mm_tn(kr[c] * kdec_ref[0], vs[c]) for c in range(n_chunks)]
    intra = [_mm(scores[c][:c_len] * dmat_ref[0], jnp.where(v_first, vs[c], 0.0))
             + _mm(scores[c][c_len:] * dmat_ref[1], jnp.where(v_first, 0.0, vs[c])) for c in range(n_chunks)]
    states = []
    state = state_ref[...]
    for c in range(n_chunks):
        states.append(state)
        state = state * cd_ref[0] + kvs[c] * bm_ref[0]
    state_ref[...] = state
    for c, rows in enumerate(chunk_rows):
        o = intra[c] + _mm(qr[c] * qdec_ref[0], states[c])
        sq = o * o
        ms0 = jnp.sum(jnp.where(v_first, sq, 0.0), axis=-1, keepdims=True)
        ms1 = jnp.sum(jnp.where(v_first, 0.0, sq), axis=-1, keepdims=True)
        ms = jnp.where(v_first, ms0, ms1) * (1.0 / HEAD64)
        y = o * lax.rsqrt(ms + NORM_EPS) * _silu(g_ref[0, rows, :])
        o_ref[0, rows, :] = y.astype(BF16)


def _ret_tables(s):
    npair = RET_HEADS // 2
    half = HEAD64 // 2
    lane = np.arange(LANES)
    log_g = np.log1p(-np.exp2(-5.0 - np.arange(RET_HEADS, dtype=np.float32))).astype(np.float32)
    idx = np.arange(RET_CHUNK, dtype=np.float32)
    rel = idx[:, None] - idx[None, :]
    dmat = np.where(rel[None] >= 0, np.exp(np.maximum(rel, 0.0)[None] * log_g[:, None, None]), 0.0)
    qdec, kdec, cd, bm = [], [], [], []
    for p in range(npair):
        hq = 2 * p + ((lane % HEAD64) >= half)
        hv = 2 * p + (lane >= HEAD64)
        qdec.append(np.exp((idx[:, None] + 1.0) * log_g[hq][None, :]))
        kdec.append(np.exp((RET_CHUNK - 1 - idx)[:, None] * log_g[hq][None, :]) * HEAD64 ** -0.5)
        cd.append(np.broadcast_to(np.exp(RET_CHUNK * log_g[hq])[:, None], (LANES, LANES)))
        bm.append((hq[:, None] == hv[None, :]).astype(np.float32))
    tabs = [np.stack(a).astype(np.float32) for a in (qdec, kdec, cd, bm)]
    inv = 1.0 / (RET_ANGLE_BASE ** jnp.linspace(0.0, 1.0, half, dtype=F32))
    pos = jnp.arange(s, dtype=F32)
    ang = pos[:, None] * inv[None, :]
    cos = jnp.tile(jnp.cos(ang), (1, LANES // half))
    sin = jnp.tile(jnp.sin(ang), (1, LANES // half))
    sin = jnp.where(jnp.asarray(lane)[None, :] < LANES // 2, -sin, sin)
    return [jnp.asarray(dmat.astype(np.float32))] + [jnp.asarray(a) for a in tabs] + [cos, sin]


def _ret(rest3, tables, ts=1024):
    b, s, _ = rest3.shape
    dmat, qdec, kdec, cd, bm, cos, sin = tables
    npair = RET_HEADS // 2
    col = lambda off: (lambda bi, p, i: (bi, i, off // LANES + p))
    tab = lambda bi, p, i: (p, 0, 0)
    kern = functools.partial(_ret_kernel, ts=ts)
    return pl.pallas_call(
        kern,
        grid=(b, npair, s // ts),
        in_specs=[pl.BlockSpec((1, ts, LANES), col(REST_RQ)),
                  pl.BlockSpec((1, ts, LANES), col(REST_RK)),
                  pl.BlockSpec((1, ts, LANES), col(REST_RV)),
                  pl.BlockSpec((1, ts, LANES), col(REST_RG)),
                  pl.BlockSpec((ts, LANES), lambda bi, p, i: (i, 0)),
                  pl.BlockSpec((ts, LANES), lambda bi, p, i: (i, 0)),
                  pl.BlockSpec((2, RET_CHUNK, RET_CHUNK), tab),
                  pl.BlockSpec((1, RET_CHUNK, LANES), tab),
                  pl.BlockSpec((1, RET_CHUNK, LANES), tab),
                  pl.BlockSpec((1, LANES, LANES), tab),
                  pl.BlockSpec((1, LANES, LANES), tab)],
        out_specs=pl.BlockSpec((1, ts, LANES), lambda bi, p, i: (bi, i, p)),
        out_shape=jax.ShapeDtypeStruct((b, s, RET_WIDTH), BF16),
        scratch_shapes=[pltpu.VMEM((LANES, LANES), F32)],
        compiler_params=_params("parallel", "parallel", "arbitrary"),
        name="retention",
    )(rest3, rest3, rest3, rest3, cos, sin, dmat, qdec, kdec, cd, bm)


def _gdn_prep_kernel(q_ref, k_ref, v_ref, sm_ref, wq_ref, wk_ref, wv_ref, alog_ref, dtb_ref,
                     u0_ref, w_ref, qg_ref, kt_ref, at_ref, eg_ref):
    h = pl.program_id(1)
    c_len = GDN_CHUNK
    n_chunks = q_ref.shape[1] // c_len
    lane = lax.broadcasted_iota(jnp.int32, (1, LANES), 1)
    ri = lax.broadcasted_iota(jnp.int32, (c_len, c_len), 0)
    ci = lax.broadcasted_iota(jnp.int32, (c_len, c_len), 1)
    incl = ri >= ci
    strict = ri > ci
    eye = (ri == ci).astype(F32)
    neg_a = -jnp.exp(alog_ref[...])
    dtb = dtb_ref[...]

    grp = GDN_PREP_GROUP
    rows = grp * c_len

    def conv_silu(ref, w_ref_, n, r0):
        cur = ref[0, pl.ds(r0, rows), :]
        p0 = pl.multiple_of(jnp.maximum(r0 - 8, 0), 8)
        prev = ref[0, pl.ds(p0, 8), :]
        prev = jnp.where(jnp.broadcast_to(n > 0, prev.shape), prev, 0.0)
        xc = jnp.concatenate([prev, cur], axis=0)
        w = w_ref_[...]
        y = cur * w[CONV_WIDTH - 1:CONV_WIDTH, :]
        for j in range(CONV_WIDTH - 1):
            shifted = pltpu.roll(xc, CONV_WIDTH - 1 - j, 0)[8:, :]
            y = y + shifted * w[j:j + 1, :]
        return _silu(y)

    def group(n, carry):
        r0 = pl.multiple_of(n * rows, rows)
        cq = conv_silu(q_ref, wq_ref, n, r0)
        ck = conv_silu(k_ref, wk_ref, n, r0)
        cv = conv_silu(v_ref, wv_ref, n, r0)
        qn = cq * lax.rsqrt(jnp.sum(cq * cq, axis=-1, keepdims=True) + NORM_EPS) * (GDN_DIM ** -0.5)
        kn = ck * lax.rsqrt(jnp.sum(ck * ck, axis=-1, keepdims=True) + NORM_EPS)
        sm = sm_ref[0, pl.ds(r0, rows), :]
        z = sm + dtb
        g_all = neg_a * (jnp.maximum(z, 0.0) + jnp.log1p(jnp.exp(-jnp.abs(z))))
        beta_all = _sigmoid(sm)
        g_col = jnp.sum(jnp.where(lane == SMALL_GA + h, g_all, 0.0), axis=-1, keepdims=True)
        beta = jnp.sum(jnp.where(lane == SMALL_GB + h, beta_all, 0.0), axis=-1, keepdims=True)
        kb = kn * beta
        vb = cv * beta
        chunks = [slice(g * c_len, (g + 1) * c_len) for g in range(grp)]
        g_row = [jnp.sum(g_col[c] * eye, axis=0, keepdims=True) for c in chunks]
        gc = [jnp.sum(jnp.where(incl, g_row[i], 0.0), axis=-1, keepdims=True) for i in range(grp)]
        gc_row = [jnp.sum(jnp.where(ri <= ci, g_col[c], 0.0), axis=0, keepdims=True) for c in chunks]
        decay = [jnp.where(incl, jnp.exp(jnp.where(incl, gc[i] - gc_row[i], 0.0)), 0.0) for i in range(grp)]
        both = [_mm_nt(jnp.concatenate([kb[c], qn[c]], axis=0), kn[c]) for c in chunks]
        low = [jnp.where(strict, both[i][:c_len] * decay[i], 0.0) for i in range(grp)]
        attn = [jnp.where(incl, both[i][c_len:] * decay[i], 0.0) for i in range(grp)]
        inv = [eye - low[i] for i in range(grp)]
        pw = low
        for _ in range(int(math.log2(c_len)) - 1):
            pw = [_mm(pw[i], pw[i]) for i in range(grp)]
            inv = [inv[i] + _mm(inv[i], pw[i]) for i in range(grp)]
        eg = [jnp.exp(gc[i]) for i in range(grp)]
        sol = [_mm(inv[i], jnp.concatenate([vb[c], kb[c] * eg[i]], axis=1)) for i, c in enumerate(chunks)]
        for i, c in enumerate(chunks):
            dst = pl.ds(pl.multiple_of(r0 + i * c_len, c_len), c_len)
            g_last = gc[i][c_len - 1:c_len, :]
            u0_ref[0, 0, dst, :] = sol[i][:, :GDN_DIM]
            w_ref[0, 0, dst, :] = sol[i][:, GDN_DIM:].astype(BF16)
            qg_ref[0, 0, dst, :] = (qn[c] * eg[i]).astype(BF16)
            kt_ref[0, 0, dst, :] = (kn[c] * jnp.exp(g_last - gc[i])).astype(BF16)
            at_ref[0, 0, dst, :] = attn[i].astype(BF16)
            eg_ref[0, 0, n * grp + i] = jnp.broadcast_to(jnp.exp(g_last), (1, LANES))
        return carry

    lax.fori_loop(0, n_chunks // grp, group, 0)


def _gdn_prep(rest3, conv_w, alog_l, dtb_l):
    b, s, _ = rest3.shape
    nh = GDN_HEADS
    n_chunks = s // GDN_CHUNK
    col = lambda off: (lambda bi, h: (bi, 0, off // LANES + h))
    wcol = lambda g: (lambda bi, h: (0, g * nh + h))
    const = lambda bi, h: (0, 0)
    row =pl.BlockSpec((1, 1, s, GDN_DIM), lambda bi, h: (bi, h, 0, 0))
    return pl.pallas_call(
        _gdn_prep_kernel,
        grid=(b, nh),
        in_specs=[pl.BlockSpec((1, s, LANES), col(REST_GQ)),
                  pl.BlockSpec((1, s, LANES), col(REST_GK)),
                  pl.BlockSpec((1, s, LANES), col(REST_GV)),
                  pl.BlockSpec((1, s, LANES), lambda bi, h: (bi, 0, REST_SMALL // LANES)),
                  pl.BlockSpec((CONV_WIDTH, LANES), wcol(0)),
                  pl.BlockSpec((CONV_WIDTH, LANES), wcol(1)),
                  pl.BlockSpec((CONV_WIDTH, LANES), wcol(2)),
                  pl.BlockSpec((1, LANES), const),
                  pl.BlockSpec((1, LANES), const)],
        out_specs=[row, row, row, row,
                   pl.BlockSpec((1, 1, s, GDN_CHUNK), lambda bi, h: (bi, h, 0, 0)),
                   pl.BlockSpec((1, 1, n_chunks, 1, LANES), lambda bi, h: (bi, h, 0, 0, 0))],
        out_shape=[jax.ShapeDtypeStruct((b, nh, s, GDN_DIM), F32),
                   jax.ShapeDtypeStruct((b, nh, s, GDN_DIM), BF16),
                   jax.ShapeDtypeStruct((b, nh, s, GDN_DIM), BF16),
                   jax.ShapeDtypeStruct((b, nh, s, GDN_DIM), BF16),
                   jax.ShapeDtypeStruct((b, nh, s, GDN_CHUNK), BF16),
                   jax.ShapeDtypeStruct((b, nh, n_chunks, 1, LANES), F32)],
        compiler_params=_params("parallel", "parallel"),
        name="gdn_prep",
    )(rest3, rest3, rest3, rest3, conv_w, conv_w, conv_w, alog_l, dtb_l)


def _gdn_scan_kernel(u0_ref, w_ref, qg_ref, kt_ref, at_ref, eg_ref, z_ref, nw_ref, o_ref, state_ref, *, ts):
    si = pl.program_id(1)

    @pl.when(si == 0)
    def _():
        state_ref[...] = jnp.zeros_like(state_ref)

    c_len = GDN_CHUNK
    per_tile = ts // c_len
    nb = u0_ref.shape[0]
    nw = nw_ref[...]
    chains = [(bb, h) for bb in range(nb) for h in range(GDN_HEADS)]
    for c in range(per_tile):
        rows = slice(c * c_len, (c + 1) * c_len)
        st = [state_ref[bb, h] for bb, h in chains]
        r = [_mm(jnp.concatenate([w_ref[bb, h, rows, :], qg_ref[bb, h, rows, :]], axis=0), st[n])
             for n, (bb, h) in enumerate(chains)]
        u = [(u0_ref[bb, h, rows, :] - r[n][:c_len]).astype(BF16) for n, (bb, h) in enumerate(chains)]
        ku = [_mm_tn(kt_ref[bb, h, rows, :], u[n]) for n, (bb, h) in enumerate(chains)]
        au = [jnp.dot(at_ref[bb, h, rows, :], u[n], preferred_element_type=F32) for n, (bb, h) in enumerate(chains)]
        for n, (bb, h) in enumerate(chains):
            state_ref[bb, h] = st[n] * eg_ref[bb, h, si * per_tile + c] + ku[n]
        for n, (bb, h) in enumerate(chains):
            cols = slice(h * GDN_DIM, (h + 1) * GDN_DIM)
            y = _rms(r[n][c_len:] + au[n], nw) * _silu(z_ref[bb, rows, cols])
            o_ref[bb, rows, cols] = y.astype(BF16)


def _gdn_scan(u0, w, qg, kt, at, eg, rest3, norm_w, ts=256, nb=4):
    b, nh, s, _ = u0.shape
    n_chunks = s // GDN_CHUNK
    blk = lambda d: pl.BlockSpec((nb, nh, ts, d), lambda bi, i: (bi, 0, i, 0))
    kern = functools.partial(_gdn_scan_kernel, ts=ts)
    return pl.pallas_call(
        kern,
        grid=(b // nb, s // ts),
        in_specs=[blk(GDN_DIM), blk(GDN_DIM), blk(GDN_DIM), blk(GDN_DIM), blk(GDN_CHUNK),
                  pl.BlockSpec((nb, nh, n_chunks, 1, LANES), lambda bi, i: (bi, 0, 0, 0, 0)),
                  pl.BlockSpec((nb, ts, GDN_WIDTH), lambda bi, i: (bi, i, REST_GZ // GDN_WIDTH)),
                  pl.BlockSpec((1, GDN_DIM), lambda bi, i: (0, 0))],
        out_specs=pl.BlockSpec((nb, ts, GDN_WIDTH), lambda bi, i: (bi, i, 0)),
        out_shape=jax.ShapeDtypeStruct((b, s, GDN_WIDTH), BF16),
        scratch_shapes=[pltpu.VMEM((nb, nh, GDN_DIM, GDN_DIM), F32)],
        compiler_params=_params("parallel", "arbitrary"),
        name="gdn_scan",
    )(u0, w, qg, kt, at, eg, rest3, norm_w)


def _outproj_kernel(x_ref, of_ref, or_ref, og_ref, w_ref, o_ref):
    acc = x_ref[...]
    acc = acc + jnp.dot(of_ref[...], w_ref[0:FOX_WIDTH, :], preferred_element_type=F32)
    acc = acc + jnp.dot(or_ref[...], w_ref[FOX_WIDTH:FOX_WIDTH + RET_WIDTH, :], preferred_element_type=F32)
    acc = acc + jnp.dot(og_ref[...], w_ref[FOX_WIDTH + RET_WIDTH:, :], preferred_element_type=F32)
    o_ref[...] = acc


def _outproj(x, o_fox, o_ret, o_gdn, w_out, tm=512):
    t = x.shape[0]
    d_mix = w_out.shape[0]
    return pl.pallas_call(
        _outproj_kernel,
        grid=(t // tm,),
        in_specs=[pl.BlockSpec((tm, D_MODEL), lambda i: (i, 0)),
                  pl.BlockSpec((tm, FOX_WIDTH), lambda i: (i, 0)),
                  pl.BlockSpec((tm, RET_WIDTH), lambda i: (i, 0)),
                  pl.BlockSpec((tm, GDN_WIDTH), lambda i: (i, 0)),
                  pl.BlockSpec((d_mix, D_MODEL), lambda i: (0, 0))],
        out_specs=pl.BlockSpec((tm, D_MODEL), lambda i: (i, 0)),
        out_shape=jax.ShapeDtypeStruct((t, D_MODEL), F32),
        compiler_params=_params("parallel"),
        name="outproj",
    )(x, o_fox, o_ret, o_gdn, w_out)


def _router_kernel(x_ref, nw_ref, w_ref, b_ref, tri_ref, sel_ref, route_ref, idx_ref, cnt_ref, carry_ref):
    @pl.when(pl.program_id(0) == 0)
    def _():
        carry_ref[...] = jnp.zeros_like(carry_ref)

    tm = x_ref.shape[0]
    hn = _rms(x_ref[...], nw_ref[...])
    logits = jnp.dot(hn, w_ref[...], precision=HIGHEST, preferred_element_type=F32) + b_ref[...]
    lane = lax.broadcasted_iota(jnp.int32, logits.shape, 1).astype(F32)
    neg = -jnp.inf
    gl = jnp.where(lane < ROUTER_EXP, logits, neg)
    gmax = jnp.max(gl, axis=-1, keepdims=True)
    gidx = jnp.min(jnp.where(gl == gmax, lane, LANES), axis=-1, keepdims=True)
    grp_p = 1.0 / jnp.sum(jnp.exp(gl - gmax), axis=-1, keepdims=True)
    lo = ROUTER_EXP + gidx * EXPERTS_PER_GROUP
    el = jnp.where((lane >= lo) & (lane < lo + EXPERTS_PER_GROUP), logits, neg)
    m1 = jnp.max(el, axis=-1, keepdims=True)
    i1 = jnp.min(jnp.where(el == m1, lane, LANES), axis=-1, keepdims=True)
    el2 = jnp.where(lane == i1, neg, el)
    m2 = jnp.max(el2, axis=-1, keepdims=True)
    i2 = jnp.min(jnp.where(el2 == m2, lane, LANES), axis=-1, keepdims=True)
    e2 = jnp.exp(m2 - m1)
    w1 = grp_p / (1.0 + e2)
    ea = jnp.minimum(i1, i2) - ROUTER_EXP
    eb = jnp.maximum(i1, i2) - ROUTER_EXP
    first_low = i1 < i2
    ga = jnp.where(first_low, w1, w1 * e2)
    gb = jnp.where(first_low, w1 * e2, w1)
    la = ea - gidx * EXPERTS_PER_GROUP
    lb = eb - gidx * EXPERTS_PER_GROUP
    cls = gidx * N_PAIRS + la * (2 * EXPERTS_PER_GROUP - 1 - la) * 0.5 + (lb - la - 1.0)
    oh = (lane == cls).astype(F32)
    cum = jnp.dot(tri_ref[...], oh.astype(BF16), preferred_element_type=F32) + carry_ref[...]
    rank = jnp.sum(oh * (cum - oh), axis=-1, keepdims=True)
    carry_ref[...] = cum[tm - 1:tm, :]
    cnt_ref[...] = cum[tm - 1:tm, :]
    out = jnp.zeros_like(logits)
    for col, val in ((ROUTE_CLASS, cls), (ROUTE_RANK, rank), (ROUTE_GATE, ga), (ROUTE_GATE + 1, gb)):
        out = jnp.where(lane == col, val, out)
    route_ref[...] = out
    idx = lax.dot_general(sel_ref[...], out, (((1,), (1,)), ((), ())), precision=HIGHEST,
                          preferred_element_type=F32)
    idx_ref[...] = idx.astype(jnp.int32)


def _router(x, nw, w_pack, b_pack, tm=512):
    t = x.shape[0]
    tri = jnp.asarray((np.arange(tm)[:, None] >= np.arange(tm)[None, :]).astype(np.float32), dtype=BF16)
    sel = np.zeros((8, LANES), np.float32)
    for row, lane in enumerate((ROUTE_CLASS, ROUTE_RANK)):
        sel[row, lane] = 1.0
    return pl.pallas_call(
        _router_kernel,
        grid=(t // tm,),
        in_specs=[pl.BlockSpec((tm, D_MODEL), lambda i: (i, 0)),
                  pl.BlockSpec((1, D_MODEL), lambda i: (0, 0)),
                  pl.BlockSpec((D_MODEL, LANES), lambda i: (0, 0)),
                  pl.BlockSpec((1, LANES), lambda i: (0, 0)),
                  pl.BlockSpec((tm, tm), lambda i: (0, 0)),
                  pl.BlockSpec((8, LANES), lambda i: (0, 0))],
        out_specs=[pl.BlockSpec((tm, LANES), lambda i: (i, 0)),
                   pl.BlockSpec((8, tm), lambda i: (0, i)),
                   pl.BlockSpec((1, LANES), lambda i: (0, 0))],
        out_shape=[jax.ShapeDtypeStruct((t, LANES), F32),
                   jax.ShapeDtypeStruct((8, t), jnp.int32),
                   jax.ShapeDtypeStruct((1, LANES), F32)],
        scratch_shapes=[pltpu.VMEM((1, LANES), F32)],
        compiler_params=_params("arbitrary"),
        name="router",
    )(x, nw, w_pack, b_pack, tri, jnp.asarray(sel))


def _round_up_tile(v):
    shift = MOE_TILE.bit_length() - 1
    return lax.shift_left(lax.shift_right_logical(v + (MOE_TILE - 1), shift), shift)


def _dispatch_kernel(off_ref, cnt_ref, tot_ref, pos_ref, x_ref, route_ref, xs_ref, ring_ref, zero_ref, sem_ref, *,
                     tm, n_tiles):
    i = pl.program_id(0)
    n = pl.num_programs(0)
    slot = i % 2

    def row_copy(s, r, p):
        return pltpu.make_async_copy(ring_ref.at[s, pl.ds(r, 1)], xs_ref.at[pl.ds(p, 1)], sem_ref.at[s])

    def pad_rows(p, size):
        return pltpu.make_async_copy(zero_ref.at[pl.ds(0, size)], xs_ref.at[pl.ds(p, size)], sem_ref.at[2])

    def pad_tile(j):
        return pltpu.make_async_copy(zero_ref, xs_ref.at[pl.ds(j * MOE_TILE, MOE_TILE)], sem_ref.at[3])

    def pad_class(c, start):
        lo = cnt_ref[c]
        lo8 = lax.shift_left(lax.shift_right_logical(lo + 7, 3), 3)
        rem = _round_up_tile(lo) - lo8

        def single(r, carry):
            cp = pad_rows(off_ref[c] + r, 1)
            cp.start() if start else cp.wait()
            return carry

        lax.fori_loop(lo, lo8, single, 0)
        for bit in range(3, MOE_TILE.bit_length() - 1):
            size = 1 << bit

            @pl.when(lax.bitwise_and(rem, size) != 0)
            def _():
                above = lax.shift_left(lax.shift_right_logical(rem, bit + 1), bit + 1)
                cp = pad_rows(pl.multiple_of(off_ref[c] + lo8 + above, 8), size)
                cp.start() if start else cp.wait()

    @pl.when(i == 0)
    def _():
        zero_ref[...] = jnp.zeros_like(zero_ref)

        def start_class(c, carry):
            pad_class(c, True)
            return carry

        def wait_class(c, carry):
            pad_class(c, False)
            return carry

        def start_tile(j, c):
            pad_tile(j).start()
            return c

        def wait_tile(j, c):
            pad_tile(0).wait()
            return c

        lax.fori_loop(0, N_CLASSES, start_class, 0)
        lax.fori_loop(tot_ref[0], n_tiles, start_tile, 0)
        lax.fori_loop(0, N_CLASSES, wait_class, 0)
        lax.fori_loop(tot_ref[0], n_tiles, wait_tile, 0)

    ring_ref[slot, :, :D_MODEL] = x_ref[...]
    ring_ref[slot, :, D_MODEL:] = route_ref[...]

    def issue(g, c):
        for u in range(2):
            r = 2 * g + u
            row_copy(slot, r, pos_ref[0, r]).start(priority=u)
        return c

    lax.fori_loop(0, tm // 2, issue, 0, unroll=4)

    def drain(s):
        def wait(r, c):
            row_copy(s, 0, 0).wait()
            return c

        lax.fori_loop(0, tm, wait, 0, unroll=8)

    @pl.when(i > 0)
    def _():
        drain(1 - slot)

    @pl.when(i == n - 1)
    def _():
        drain(slot)


def _dispatch(x, route, off, cnt, total, pos, tm, n_tiles):
    t = x.shape[0]
    kern = functools.partial(_dispatch_kernel, tm=tm, n_tiles=n_tiles)
    return pl.pallas_call(
        kern,
        grid_spec=pltpu.PrefetchScalarGridSpec(
            num_scalar_prefetch=3,
            grid=(t // tm,),
            in_specs=[pl.BlockSpec((1, tm), lambda i, *_: (0, i), memory_space=pltpu.SMEM),
                      pl.BlockSpec((tm, D_MODEL), lambda i, *_: (i, 0)),
                      pl.BlockSpec((tm, LANES), lambda i, *_: (i, 0))],
            out_specs=pl.BlockSpec(memory_space=pl.ANY),
            scratch_shapes=[pltpu.VMEM((2, tm, ROW_WIDTH), F32),
                            pltpu.VMEM((MOE_TILE, ROW_WIDTH), F32),
                            pltpu.SemaphoreType.DMA((4,))]),
        out_shape=jax.ShapeDtypeStruct((n_tiles * MOE_TILE, ROW_WIDTH), F32),
        compiler_params=_params("arbitrary"),
        name="dispatch",
    )(off, cnt, total, pos, x, route)


def _ffn_kernel(ta_ref, tb_ref, fa_ref, fb_ref, tot_ref, xs_ref, nw_ref, w1a_ref, w3a_ref, w2a_ref,
                w1b_ref, w3b_ref, w2b_ref, ys_ref, w1a, w3a, w2a, w1b, w3b, w2b):
    j = pl.program_id(0)

    @pl.when(fa_ref[j] == 1)
    def _():
        w1a[...] = w1a_ref[0].astype(BF16)
        w3a[...] = w3a_ref[0].astype(BF16)
        w2a[...] = w2a_ref[0].astype(BF16)

    @pl.when(fb_ref[j] == 1)
    def _():
        w1b[...] = w1b_ref[0].astype(BF16)
        w3b[...] = w3b_ref[0].astype(BF16)
        w2b[...] = w2b_ref[0].astype(BF16)

    @pl.when(j < tot_ref[0])
    def _():
        x = xs_ref[:, :D_MODEL]
        ga = xs_ref[:, D_MODEL + ROUTE_GATE:D_MODEL + ROUTE_GATE + 1]
        gb = xs_ref[:, D_MODEL + ROUTE_GATE + 1:D_MODEL + ROUTE_GATE + 2]
        hn = _rms(x, nw_ref[...]).astype(BF16)

        def expert(w1, w3, w2):
            a = jnp.dot(hn, w1[...], preferred_element_type=F32)
            u = jnp.dot(hn, w3[...], preferred_element_type=F32)
            return jnp.dot((_silu(a) * u).astype(BF16), w2[...], preferred_element_type=F32)

        ys_ref[...] = x + ga * expert(w1a, w3a, w2a) + gb * expert(w1b, w3b, w2b)

    @pl.when(j >= tot_ref[0])
    def _():
        ys_ref[...] = jnp.zeros_like(ys_ref)


def _ffn(xs, nw, w1, w3, w2, tile_a, tile_b, first_a, first_b, total):
    n_tiles = tile_a.shape[0]
    tok = lambda j, *_: (j, 0)
    wa = lambda j, ta, tb, fa, fb, tot: (ta[j], 0, 0)
    wb = lambda j, ta, tb, fa, fb, tot: (tb[j], 0, 0)
    up = lambda imap: pl.BlockSpec((1, D_MODEL, EXPERT_FF), imap)
    down = lambda imap: pl.BlockSpec((1, EXPERT_FF, D_MODEL), imap)
    return pl.pallas_call(
        _ffn_kernel,
        grid_spec=pltpu.PrefetchScalarGridSpec(
            num_scalar_prefetch=5,
            grid=(n_tiles,),
            in_specs=[pl.BlockSpec((MOE_TILE, ROW_WIDTH), tok),
                      pl.BlockSpec((1, D_MODEL), lambda j, *_: (0, 0)),
                      up(wa), up(wa), down(wa), up(wb), up(wb), down(wb)],
            out_specs=pl.BlockSpec((MOE_TILE, D_MODEL), tok),
            scratch_shapes=[pltpu.VMEM((D_MODEL, EXPERT_FF), BF16),
                            pltpu.VMEM((D_MODEL, EXPERT_FF), BF16),
                            pltpu.VMEM((EXPERT_FF, D_MODEL), BF16),
                            pltpu.VMEM((D_MODEL, EXPERT_FF), BF16),
                            pltpu.VMEM((D_MODEL, EXPERT_FF), BF16),
                            pltpu.VMEM((EXPERT_FF, D_MODEL), BF16)]),
        out_shape=jax.ShapeDtypeStruct((n_tiles * MOE_TILE, D_MODEL), F32),
        compiler_params=_params("arbitrary"),
        name="ffn",
    )(tile_a, tile_b, first_a, first_b, total, xs, nw, w1, w3, w2, w1, w3, w2)


def _gather_kernel(pos_ref, posn_ref, fw_ref, ys_ref, o_ref, ybuf, sem_ref, *, tm, final_norm):
    i = pl.program_id(0)
    n = pl.num_programs(0)
    slot = i % 2

    def row_copy(s, r, p):
        return pltpu.make_async_copy(ys_ref.at[pl.ds(p, 1)], ybuf.at[s, pl.ds(r, 1)], sem_ref.at[s])

    def issue(p_ref, s):
        def body(g, c):
            for u in range(2):
                r = 2 * g + u
                row_copy(s, r, p_ref[0, r]).start(priority=u)
            return c

        lax.fori_loop(0, tm // 2, body, 0, unroll=4)

    @pl.when(i == 0)
    def _():
        issue(pos_ref, 0)

    @pl.when(i + 1 < n)
    def _():
        issue(posn_ref, 1 - slot)

    def wait(r, c):
        row_copy(slot, 0, 0).wait()
        return c

    lax.fori_loop(0, tm, wait, 0, unroll=8)
    out = ybuf[slot]
    if final_norm:
        out = _rms(out, fw_ref[...])
    o_ref[...] = out


def _gather(ys, pos, fw, final_norm, tm):
    t = pos.shape[1]
    n = t // tm
    kern = functools.partial(_gather_kernel, tm=tm, final_norm=final_norm)
    smem = lambda imap: pl.BlockSpec((1, tm), imap, memory_space=pltpu.SMEM)
    return pl.pallas_call(
        kern,
        grid=(n,),
        in_specs=[smem(lambda i: (0, i)),
                  smem(lambda i: (0, jnp.minimum(i + 1, n - 1))),
                  pl.BlockSpec((1, D_MODEL), lambda i: (0, 0)),
                  pl.BlockSpec(memory_space=pl.ANY)],
        out_specs=pl.BlockSpec((tm, D_MODEL), lambda i: (i, 0)),
        out_shape=jax.ShapeDtypeStruct((t, D_MODEL), F32),
        scratch_shapes=[pltpu.VMEM((2, tm, D_MODEL), F32),
                        pltpu.SemaphoreType.DMA((2,))],
        compiler_params=_params("arbitrary"),
        name="gather",
    )(pos, pos, fw, ys)


def _class_experts():
    lo, hi = [], []
    for g in range(N_GROUPS):
        for la in range(EXPERTS_PER_GROUP):
            for lb in range(la + 1, EXPERTS_PER_GROUP):
                lo.append(g * EXPERTS_PER_GROUP + la)
                hi.append(g * EXPERTS_PER_GROUP + lb)
    return np.asarray(lo, np.int32), np.asarray(hi, np.int32)


def _lookup(table, index):
    k = table.shape[0]
    return jnp.sum(jnp.where(index[..., None] == jnp.arange(k, dtype=jnp.int32), table, 0), axis=-1)


def _moe(x, nw, w_pack, b_pack, w1, w3, w2, fw, final_norm, tm=512):
    t = x.shape[0]
    route, idx, cnt_row = _router(x, nw, w_pack, b_pack)
    cnt = cnt_row[0, :N_CLASSES].astype(jnp.int32)
    n_tiles = t // MOE_TILE + N_CLASSES
    nblk = (cnt + MOE_TILE - 1) // MOE_TILE
    cend = jnp.cumsum(nblk)
    total = cend[-1:]
    off = (cend - nblk) * MOE_TILE
    pos = (_lookup(off, idx[ROUTE_CLASS]) + idx[ROUTE_RANK]).reshape(1, t)
    j = jnp.arange(n_tiles, dtype=jnp.int32)
    tile_c = jnp.minimum(jnp.sum((j[:, None] >= cend[None, :]).astype(jnp.int32), axis=1), N_CLASSES - 1)
    lo, hi = _class_experts()
    tile_a = _lookup(jnp.asarray(lo), tile_c)
    tile_b = _lookup(jnp.asarray(hi), tile_c)
    valid = j < total
    changed = lambda e: (valid & ((j == 0) | (e != jnp.roll(e, 1)))).astype(jnp.int32)
    xs = _dispatch(x, route, off, cnt, total, pos, tm, n_tiles)
    ys = _ffn(xs, nw, w1, w3, w2, tile_a, tile_b, changed(tile_a), changed(tile_b), total)
    return _gather(ys, pos, fw, final_norm, tm)


def _pack_in_weights(w_in_l):
    off = np.concatenate([[0], np.cumsum(IN_SPLITS)]).tolist()
    fq, fk, fv, ff, rq, rk, rv, rg, gq, gk, gv, gz, ga, gb = [(off[i], IN_SPLITS[i]) for i in range(len(IN_SPLITS))]
    half = HEAD64 // 2

    def permuted(seg):
        return [(seg[0] + (2 * p + hh) * HEAD64 + lo * half, half)
                for p in range(RET_HEADS // 2) for lo in range(2) for hh in range(2)]

    col_scale = np.ones((1, w_in_l.shape[1]), np.float32)
    col_scale[:, fq[0]:fq[0] + fq[1]] = LOG2E * HEAD64 ** -0.5
    w_bf = (w_in_l * col_scale).astype(BF16)
    cols = lambda segs: [w_bf[:, a:a + n] for a, n in segs]
    wf = jnp.concatenate(cols([fq, fk, fv]), axis=1)
    pad = jnp.zeros((D_MODEL, LANES - (FOX_HEADS + 2 * GDN_HEADS)), BF16)
    wr = jnp.concatenate(cols(permuted(rq) + permuted(rk) + [rv, rg, gq, gk, gv, gz, ff, ga, gb]) + [pad], axis=1)
    return wf, wr


def _lane_row(vals, offset):
    return jnp.zeros((1, LANES), F32).at[0, offset:offset + vals.shape[0]].set(vals.astype(F32))


def kernel(x, norm1_w, w_in, fox_forget_bias, gdn_conv_w, gdn_a_log, gdn_dt_bias, gdn_norm_w, w_out, norm2_w,
           router_group_w, router_group_b, router_expert_w, router_expert_b, expert_w1, expert_w3, expert_w2,
           final_norm_w):
    b, s, d = x.shape
    t = b * s
    depth = w_in.shape[0]
    fox_tk = 512
    xt = x.reshape(t, d)
    ret_tables = _ret_tables(s)
    for l in range(depth):
        wf, wr = _pack_in_weights(w_in[l])
        qkv, rest = _inproj(xt, norm1_w[l].reshape(1, d), wf, wr)
        rest3 = rest.reshape(b, s, REST_WIDTH)
        c = _fgate(rest3, _lane_row(fox_forget_bias[l], SMALL_FF)).reshape(b, FOX_HEADS, s // fox_tk, 1, fox_tk)
        o_fox = _fox(qkv.reshape(b, s, 3 * FOX_WIDTH), c, tk=fox_tk)
        o_ret = _ret(rest3, ret_tables)
        prep = _gdn_prep(rest3, gdn_conv_w[l].astype(F32), _lane_row(gdn_a_log[l], SMALL_GA),
                         _lane_row(gdn_dt_bias[l], SMALL_GA))
        o_gdn = _gdn_scan(*prep, rest3, gdn_norm_w[l].reshape(1, GDN_DIM).astype(F32))
        xt = _outproj(xt, o_fox.reshape(t, FOX_WIDTH), o_ret.reshape(t, RET_WIDTH),
                      o_gdn.reshape(t, GDN_WIDTH), w_out[l].astype(BF16))
        w_pack = jnp.concatenate([router_group_w[l], router_expert_w[l],
                                  jnp.zeros((d, LANES - N_GROUPS - N_EXPERTS), F32)], axis=1)
        b_pack = jnp.concatenate([router_group_b[l].reshape(-1), router_expert_b[l].reshape(-1),
                                  jnp.zeros((LANES - N_GROUPS - N_EXPERTS,), F32)]).reshape(1, LANES)
        xt = _moe(xt, norm2_w[l].reshape(1, d), w_pack, b_pack,
                  expert_w1[l].reshape(N_EXPERTS, d, EXPERT_FF),
                  expert_w3[l].reshape(N_EXPERTS, d, EXPERT_FF),
                  expert_w2[l].reshape(N_EXPERTS, EXPERT_FF, d),
                  final_norm_w.reshape(1, d), final_norm=(l == depth - 1))
    return xt.reshape(b, s, d)
```
